```python
import jax, jax.numpy as jnp
from jax import lax
import numpy as np

D_MODEL = 2048
BATCH = 4
SEQ = 2048
DEPTH = 2

MIX_WIDTH = D_MODEL
N_MIXERS = 4
GROUP_WIDTH = MIX_WIDTH // N_MIXERS
GMLP_CHUNK = 128
GMLP_GROUPS = 8
GMLP_GROUP_DIM = GROUP_WIDTH // GMLP_GROUPS
DIFF_HEAD_DIM = 64
DIFF_HEADS = GROUP_WIDTH // (2 * DIFF_HEAD_DIM)
ATTN_BLOCK = 128
DIL_HEAD_DIM = 64
DIL_HEADS = GROUP_WIDTH // DIL_HEAD_DIM
DIL_PATTERNS = ((128, 1), (512, 4), (2048, 16))
CONV_WIDTH = 31
CONV_CH = GROUP_WIDTH
D_FF = 4 * D_MODEL
N_IN_SPLITS = 10
IN_WIDTH = N_IN_SPLITS * GROUP_WIDTH
N_ALIBI_HEADS = DIFF_HEADS + DIL_HEADS
NORM_EPS = 1e-6

kernel_name = "hybrid_parallel_gmlp_diffattn_dilated_conformer"


def rmsnorm(x, g):
    xf = x.astype(jnp.float32)
    y = xf * lax.rsqrt(jnp.mean(xf * xf, axis=-1, keepdims=True) + NORM_EPS)
    return (y * g.astype(jnp.float32)).astype(x.dtype)


def layernorm_noparam(x):
    xf = x.astype(jnp.float32)
    mu = jnp.mean(xf, axis=-1, keepdims=True)
    xc = xf - mu
    y = xc * lax.rsqrt(jnp.mean(xc * xc, axis=-1, keepdims=True) + NORM_EPS)
    return y.astype(x.dtype)


def alibi_slopes():
    i = jnp.arange(1, N_ALIBI_HEADS + 1, dtype=jnp.float32)
    s = 2.0 ** (-8.0 * i / N_ALIBI_HEADS)
    diff_idx = np.arange(0, N_ALIBI_HEADS, 3)
    dil_idx = np.array([j for j in range(N_ALIBI_HEADS) if j % 3 != 0])
    return s[diff_idx], s[dil_idx]


def gmlp_mixer(u, v, w_s, b_s):
    B, S, _ = u.shape
    u = jax.nn.gelu(u)
    v = layernorm_noparam(jax.nn.gelu(v))
    nc = S // GMLP_CHUNK
    vc = v.reshape(B, nc, GMLP_CHUNK, GMLP_GROUPS, GMLP_GROUP_DIM)
    causal = jnp.tril(jnp.ones((GMLP_CHUNK, GMLP_CHUNK), dtype=bool))
    w = jnp.where(causal[None], w_s, jnp.zeros_like(w_s))
    z = jnp.einsum('gts,bcsgd->bctgd', w, vc) + b_s.T[None, None, :, :, None]
    return u * z.reshape(B, S, GROUP_WIDTH)


def diff_attention(q, k, v, lam, lam_init, subln_g, slopes):
    B, S = q.shape[:2]
    nb = S // ATTN_BLOCK
    scale = DIFF_HEAD_DIM ** -0.5
    key_pos = jnp.arange(S)
    qb = q.reshape(B, nb, ATTN_BLOCK, DIFF_HEADS, 2, DIFF_HEAD_DIM).transpose(1, 0, 2, 3, 4, 5)

    def block(args):
        qi, start = args
        s = jnp.einsum('bqhcd,bkhcd->bhcqk', qi, k).astype(jnp.float32) * scale
        dist = (start + jnp.arange(ATTN_BLOCK))[:, None] - key_pos[None, :]
        bias = -slopes[:, None, None, None] * dist.astype(jnp.float32)
        s = jnp.where(dist >= 0, s + bias, -jnp.inf)
        p = jax.nn.softmax(s, axis=-1)
        a = p[:, :, 0] - lam * p[:, :, 1]
        return jnp.einsum('bhqk,bkhe->bqhe', a.astype(v.dtype), v)

    o = lax.map(block, (qb, jnp.arange(nb) * ATTN_BLOCK))
    o = o.transpose(1, 0, 2, 3, 4).reshape(B, S, DIFF_HEADS, 2 * DIFF_HEAD_DIM)
    o = rmsnorm(o, subln_g) * (1.0 - lam_init)
    return o.reshape(B, S, GROUP_WIDTH)


def dilated_attention(q, k, v, slopes):
    B, S, H, dh = q.shape
    scale = dh ** -0.5
    results = []
    for window, dil in DIL_PATTERNS:
        n = window // dil
        seg = n * dil
        sp = -(-S // seg) * seg
        nb = sp // seg

        def to_classes(t):
            t = jnp.pad(t, ((0, 0), (0, sp - S), (0, 0), (0, 0)))
            t = t.reshape(B, sp // dil, dil, H, dh).transpose(0, 2, 3, 1, 4)
            return t.reshape(B, dil, H, nb, n, dh)

        def with_prev(t):
            prev = jnp.pad(t, ((0, 0), (0, 0), (0, 0), (1, 0), (0, 0), (0, 0)))[:, :, :, :-1]
            return jnp.concatenate([prev, t], axis=4)

        def from_classes(t):
            e = t.shape[-1]
            t = t.reshape(B, dil, H, sp // dil, e).transpose(0, 3, 1, 2, 4).reshape(B, sp, H, e)
            return t[:, :S]

        qc = to_classes(q)
        kb = with_prev(to_classes(k))
        vb = with_prev(to_classes(v))
        s = jnp.einsum('brhcqd,brhckd->brhcqk', qc, kb).astype(jnp.float32) * scale
        step = n + jnp.arange(n)[:, None] - jnp.arange(2 * n)[None, :]
        has_key = (jnp.arange(nb)[:, None, None] > 0) | (jnp.arange(2 * n)[None, None, :] >= n)
        valid = (step >= 0) & (step <= n) & has_key
        bias = -slopes[:, None, None, None] * (step * dil).astype(jnp.float32)
        s = jnp.where(valid, s + bias, -jnp.inf)
        m = jnp.max(s, axis=-1, keepdims=True)
        p = jnp.exp(s - m)
        l = jnp.sum(p, axis=-1, keepdims=True)
        o = jnp.einsum('brhcqk,brhckd->brhcqd', p.astype(v.dtype), vb).astype(jnp.float32)
        results.append((from_classes(m), from_classes(l), from_classes(o)))
    m_all = results[0][0]
    for m_i, _, _ in results[1:]:
        m_all = jnp.maximum(m_all, m_i)
    num = sum(jnp.exp(m_i - m_all) * o_i for m_i, _, o_i in results)
    den = sum(jnp.exp(m_i - m_all) * l_i for m_i, l_i, _ in results)
    return (num / den).astype(q.dtype).reshape(B, S, GROUP_WIDTH)


def conformer_conv(a, gate, w_dw, b_dw, norm_g):
    h = a * jax.nn.sigmoid(gate)
    h = lax.conv_general_dilated(
        h, w_dw[:, None, :].astype(h.dtype), window_strides=(1,), padding=[(CONV_WIDTH - 1, 0)],
        dimension_numbers=('NWC', 'WIO', 'NWC'), feature_group_count=CONV_CH)
    h = rmsnorm(h + b_dw, norm_g)
    return jax.nn.silu(h)


def setup_inputs(seed: int = 0) -> dict:
    key = jax.random.key(seed)
    ks = jax.random.split(key, 20)
    L, f32 = DEPTH, jnp.float32
    nrm = lambda k, shape, scale: jax.random.normal(k, shape, f32) * scale
    gain = lambda k, shape: 1.0 + 0.1 * jax.random.normal(k, shape, f32)
    return {
        "x": jax.random.normal(ks[0], (BATCH, SEQ, D_MODEL), f32),
        "g_mix_pre": gain(ks[1], (L, D_MODEL)),
        "g_mix_post": gain(ks[2], (L, D_MODEL)),
        "w_in": nrm(ks[3], (L, D_MODEL, IN_WIDTH), D_MODEL ** -0.5),
        "gmlp_w": nrm(ks[4], (L, GMLP_GROUPS, GMLP_CHUNK, GMLP_CHUNK), GMLP_CHUNK ** -0.5),
        "gmlp_b": gain(ks[5], (L, GMLP_GROUPS, GMLP_CHUNK)),
        "diff_lam": nrm(ks[6], (L, 4, DIFF_HEAD_DIM), 0.1),
        "diff_subln": gain(ks[7], (L, 2 * DIFF_HEAD_DIM)),
        "conv_w": nrm(ks[8], (L, CONV_WIDTH, CONV_CH), CONV_WIDTH ** -0.5),
        "conv_b": nrm(ks[9], (L, CONV_CH), 0.02),
        "conv_norm": gain(ks[10], (L, CONV_CH)),
        "w_out": nrm(ks[11], (L, MIX_WIDTH, D_MODEL), MIX_WIDTH ** -0.5),
        "g_ffn_pre": gain(ks[12], (L, D_MODEL)),
        "g_ffn_post": gain(ks[13], (L, D_MODEL)),
        "w_ff1": nrm(ks[14], (L, D_MODEL, D_FF), D_MODEL ** -0.5),
        "w_ff2": nrm(ks[15], (L, D_FF, D_MODEL), D_FF ** -0.5),
    }


def reference(x, g_mix_pre, g_mix_post, w_in, gmlp_w, gmlp_b, diff_lam, diff_subln,
              conv_w, conv_b, conv_norm, w_out, g_ffn_pre, g_ffn_post, w_ff1, w_ff2):
    B, S, _ = x.shape
    slopes_diff, slopes_dil = alibi_slopes()
    for li in range(DEPTH):
        h = rmsnorm(x, g_mix_pre[li])
        proj = jnp.einsum('bsd,de->bse', h, w_in[li])
        a_u, a_v, b_q, b_k, b_v, c_q, c_k, c_v, d_a, d_g = jnp.split(proj, N_IN_SPLITS, axis=-1)

        out_a = gmlp_mixer(a_u, a_v, gmlp_w[li], gmlp_b[li])

        lam_init = 0.8 - 0.6 * float(np.exp(-0.3 * li))
        lp = diff_lam[li].astype(jnp.float32)
        lam = jnp.exp(jnp.sum(lp[0] * lp[1])) - jnp.exp(jnp.sum(lp[2] * lp[3])) + lam_init
        out_b = diff_attention(
            b_q.reshape(B, S, DIFF_HEADS, 2, DIFF_HEAD_DIM),
            b_k.reshape(B, S, DIFF_HEADS, 2, DIFF_HEAD_DIM),
            b_v.reshape(B, S, DIFF_HEADS, 2 * DIFF_HEAD_DIM),
            lam, lam_init, diff_subln[li], slopes_diff)

        out_c = dilated_attention(
            c_q.reshape(B, S, DIL_HEADS, DIL_HEAD_DIM),
            c_k.reshape(B, S, DIL_HEADS, DIL_HEAD_DIM),
            c_v.reshape(B, S, DIL_HEADS, DIL_HEAD_DIM),
            slopes_dil)

        out_d = conformer_conv(d_a, d_g, conv_w[li], conv_b[li], conv_norm[li])

        mixed = jnp.concatenate([out_a, out_b, out_c, out_d], axis=-1)
        x = x + rmsnorm(jnp.einsum('bse,ed->bsd', mixed, w_out[li]), g_mix_post[li])

        h = rmsnorm(x, g_ffn_pre[li])
        f = jnp.square(jax.nn.relu(jnp.einsum('bsd,df->bsf', h, w_ff1[li])))
        x = x + rmsnorm(jnp.einsum('bsf,fd->bsd', f, w_ff2[li]), g_ffn_post[li])
    return x
```

```python
import functools

import numpy as np
import jax
import jax.numpy as jnp
from jax import lax
from jax.experimental import pallas as pl
from jax.experimental.pallas import tpu as pltpu

F32 = jnp.float32
BF16 = jnp.bfloat16

D_MODEL = 2048
GROUP_WIDTH = 512
N_IN_SPLITS = 10
D_FF = 4 * D_MODEL
NORM_EPS = 1e-6
GMLP_CHUNK = 128
GMLP_GROUPS = 8
DIFF_HEADS = 4
DIL_HEADS = 8
HEAD_DIM = 64
DIL_PATTERNS = ((128, 1), (512, 4), (2048, 16))
DIL_BLOCK = 128
CONV_WIDTH = 31
N_ALIBI_HEADS = DIFF_HEADS + DIL_HEADS
ATTN_SCALE = HEAD_DIM ** -0.5
MASKED = -1e30

LANES = 128
VMEM_LIMIT = 48 * 1024 * 1024

TM_PROJ = 512
TM_OUT = 512
TM_FFN = 512
TF_FFN = 512
GMLP_ROWS = 512
DIFF_TQ = 256
DIFF_TK = 128
CONV_ROWS = 256
CONV_HALO = 32


def _alibi_slopes():
    i = np.arange(1, N_ALIBI_HEADS + 1, dtype=np.float64)
    s = 2.0 ** (-8.0 * i / N_ALIBI_HEADS)
    diff_idx = np.arange(0, N_ALIBI_HEADS, 3)
    dil_idx = np.array([j for j in range(N_ALIBI_HEADS) if j % 3 != 0])
    return s[diff_idx], s[dil_idx]


def _rms(x, g):
    return x * lax.rsqrt(jnp.mean(x * x, axis=-1, keepdims=True) + NORM_EPS) * g


def _params(sem):
    return pltpu.CompilerParams(dimension_semantics=sem, vmem_limit_bytes=VMEM_LIMIT)


def _proj_kernel(x_ref, g_ref, w_ref, of_ref, ob_ref, h_ref):
    j = pl.program_id(1)

    @pl.when(j == 0)
    def _():
        h_ref[...] = _rms(x_ref[...], g_ref[...]).astype(BF16)

    y = jnp.dot(h_ref[...], w_ref[...], preferred_element_type=F32)

    @pl.when(j < 4)
    def _():
        of_ref[...] = y

    @pl.when(j >= 4)
    def _():
        ob_ref[...] = y.astype(BF16)


def _in_proj(x, g, w):
    m = x.shape[0]
    tm, tn = TM_PROJ, GROUP_WIDTH
    wcol = lambda i, j: (0, jnp.where(j < 2, j, jnp.where(j < 4, j + 6, j - 2)))
    return pl.pallas_call(
        _proj_kernel,
        grid=(m // tm, N_IN_SPLITS),
        in_specs=[
            pl.BlockSpec((tm, D_MODEL), lambda i, j: (i, 0)),
            pl.BlockSpec((1, D_MODEL), lambda i, j: (0, 0)),
            pl.BlockSpec((D_MODEL, tn), wcol),
        ],
        out_specs=[
            pl.BlockSpec((tm, tn), lambda i, j: (i, jnp.minimum(j, 3))),
            pl.BlockSpec((tm, tn), lambda i, j: (i, jnp.maximum(j - 4, 0))),
        ],
        out_shape=[
            jax.ShapeDtypeStruct((m, 4 * GROUP_WIDTH), F32),
            jax.ShapeDtypeStruct((m, 6 * GROUP_WIDTH), BF16),
        ],
        scratch_shapes=[pltpu.VMEM((tm, D_MODEL), BF16)],
        compiler_params=_params(("parallel", "arbitrary")),
        name="in_proj",
    )(x, g, w)


def _gelu(x):
    return jax.nn.gelu(x, approximate=True)


def _gmlp_kernel(u_ref, v_ref, w_ref, b_ref, o_ref):
    c = GMLP_CHUNK
    t_idx = lax.broadcasted_iota(jnp.int32, w_ref.shape, 0)
    s_idx = lax.broadcasted_iota(jnp.int32, w_ref.shape, 1) % c
    w = jnp.where(s_idx <= t_idx, w_ref[...], 0.0).astype(BF16)
    bias = b_ref[...]
    lane = lax.broadcasted_iota(jnp.int32, (c, LANES), 1)
    low = lane < HEAD_DIM
    for ci in range(GMLP_ROWS // c):
        rows = slice(ci * c, (ci + 1) * c)
        u = _gelu(u_ref[rows, :])
        v = _gelu(v_ref[rows, :])
        mu = jnp.mean(v, axis=-1, keepdims=True)
        vc = v - mu
        v = vc * lax.rsqrt(jnp.mean(vc * vc, axis=-1, keepdims=True) + NORM_EPS)
        zs = []
        for p in range(GMLP_GROUPS // 2):
            vp = v[:, p * LANES:(p + 1) * LANES]
            rhs = jnp.concatenate([jnp.where(low, vp, 0.0), jnp.where(low, 0.0, vp)], axis=0)
            zs.append(jnp.dot(w[:, p * 2 * c:(p + 1) * 2 * c], rhs.astype(BF16),
                              preferred_element_type=F32))
        z = jnp.concatenate(zs, axis=1) + bias
        o_ref[rows, :] = (u * z).astype(BF16)


def _gmlp(of, w_t, bias_full):
    m = of.shape[0]
    tr = GMLP_ROWS
    return pl.pallas_call(
        _gmlp_kernel,
        grid=(m // tr,),
        in_specs=[
            pl.BlockSpec((tr, GROUP_WIDTH), lambda i: (i, 0)),
            pl.BlockSpec((tr, GROUP_WIDTH), lambda i: (i, 1)),
            pl.BlockSpec((GMLP_CHUNK, GMLP_GROUPS * GMLP_CHUNK), lambda i: (0, 0)),
            pl.BlockSpec((GMLP_CHUNK, GROUP_WIDTH), lambda i: (0, 0)),
        ],
        out_specs=pl.BlockSpec((tr, GROUP_WIDTH), lambda i: (i, 0)),
        out_shape=jax.ShapeDtypeStruct((m, GROUP_WIDTH), BF16),
        compiler_params=_params(("parallel",)),
        name="gmlp",
    )(of, of, w_t, bias_full)


def _stack_queries(q):
    lane = lax.broadcasted_iota(jnp.int32, q.shape, 1)
    low = lane < HEAD_DIM
    zero = jnp.zeros_like(q)
    q2 = jnp.concatenate([jnp.where(low, q, zero), jnp.where(low, zero, q)], axis=0)
    return (q2 * ATTN_SCALE).astype(BF16)


def _qk(q2, k):
    return lax.dot_general(q2, k, (((1,), (1,)), ((), ())), preferred_element_type=F32)


def _diff_kernel(slope_ref, q_ref, k_ref, v_ref, bias_ref, lam_ref, g_ref, o_ref,
                 m_s, l_s, acc_s, *, lam_init):
    h = pl.program_id(1)
    qi = pl.program_id(2)
    tq, tk = DIFF_TQ, DIFF_TK
    slope = slope_ref[h]
    q2 = _stack_queries(q_ref[...])
    m_s[...] = jnp.full(m_s.shape, MASKED, F32)
    l_s[...] = jnp.zeros(l_s.shape, F32)
    acc_s[...] = jnp.zeros(acc_s.shape, F32)

    def step(k, v, bias, shift):
        s = _qk(q2, k) + bias
        m_prev = m_s[...]
        m_cur = jnp.max(s, axis=1, keepdims=True) + shift
        m_next = jnp.maximum(m_prev, m_cur)
        p = jnp.exp(s - (m_next - shift))
        alpha = jnp.exp(m_prev - m_next)
        l_s[...] = alpha * l_s[...] + jnp.sum(p, axis=1, keepdims=True)
        acc_s[...] = alpha * acc_s[...] + jnp.dot(p.astype(BF16), v, preferred_element_type=F32)
        m_s[...] = m_next

    def body(j, carry):
        start = pl.multiple_of(j * tk, tk)
        shift = -slope * (qi * tq - j * tk).astype(F32)
        step(k_ref[pl.ds(start, tk), :], v_ref[pl.ds(start, tk), :], bias_ref[0], shift)
        return carry

    nfull = qi * (tq // tk)
    lax.fori_loop(0, nfull, body, 0)
    for d in range(tq // tk):
        start = pl.multiple_of(qi * tq + d * tk, tk)
        step(k_ref[pl.ds(start, tk), :], v_ref[pl.ds(start, tk), :], bias_ref[1 + d], 0.0)

    lp = lam_ref[...]
    lam = (jnp.exp(jnp.sum(lp[0:1] * lp[1:2], axis=1, keepdims=True))
           - jnp.exp(jnp.sum(lp[2:3] * lp[3:4], axis=1, keepdims=True)) + lam_init)
    o = acc_s[...] / l_s[...]
    o = o[:tq] - lam * o[tq:]
    o_ref[...] = (_rms(o, g_ref[...]) * (1.0 - lam_init)).astype(BF16)


def _diff_bias_tiles(slopes):
    tq, tk = DIFF_TQ, DIFF_TK
    rq = (np.arange(2 * tq) % tq)[:, None].astype(np.float64)
    c = np.arange(tk)[None, :].astype(np.float64)
    tiles = np.zeros((DIFF_HEADS, 1 + tq // tk, 2 * tq, tk), np.float64)
    for h, sl in enumerate(slopes):
        tiles[h, 0] = -sl * (rq - c)
        for d in range(tq // tk):
            dist = rq - d * tk - c
            tiles[h, 1 + d] = np.where(dist >= 0, -sl * dist, MASKED)
    return tiles.astype(np.float32)


def _diff_attention(ob, lam_p, subln_g, lam_init, batch, seq):
    m = ob.shape[0]
    tq, tk = DIFF_TQ, DIFF_TK
    nq = seq // tq
    slopes, _ = _alibi_slopes()
    bias = jnp.asarray(_diff_bias_tiles(slopes))
    kern = functools.partial(_diff_kernel, lam_init=lam_init)
    return pl.pallas_call(
        kern,
        grid=(batch, DIFF_HEADS, nq),
        in_specs=[
            pl.BlockSpec(memory_space=pltpu.SMEM),
            pl.BlockSpec((tq, LANES), lambda b, h, i: (b * nq + i, h)),
            pl.BlockSpec((seq, LANES), lambda b, h, i: (b, DIFF_HEADS + h)),
            pl.BlockSpec((seq, LANES), lambda b, h, i: (b, 2 * DIFF_HEADS + h)),
            pl.BlockSpec((None, 1 + tq // tk, 2 * tq, tk), lambda b, h, i: (h, 0, 0, 0)),
            pl.BlockSpec((4, HEAD_DIM), lambda b, h, i: (0, 0)),
            pl.BlockSpec((1, 2 * HEAD_DIM), lambda b, h, i: (0, 0)),
        ],
        out_specs=pl.BlockSpec((tq, LANES), lambda b, h, i: (b * nq + i, h)),
        out_shape=jax.ShapeDtypeStruct((m, GROUP_WIDTH), BF16),
        scratch_shapes=[
            pltpu.VMEM((2 * tq, LANES), F32),
            pltpu.VMEM((2 * tq, LANES), F32),
            pltpu.VMEM((2 * tq, LANES), F32),
        ],
        compiler_params=_params(("parallel", "parallel", "arbitrary")),
        name="diff_attn",
    )(jnp.asarray(slopes, F32), ob, ob, ob, bias, lam_p, subln_g)


def _dil_kernel(q_ref, k_ref, v_ref, bias_ref, o_ref, qf, kf, vf, m_s, l_s, o_s, *, seq):
    n = DIL_BLOCK
    qf[...] = q_ref[...].astype(F32)
    kf[...] = k_ref[...].astype(F32)
    vf[...] = v_ref[...].astype(F32)
    lane = lax.broadcasted_iota(jnp.int32, (n, LANES), 1)
    low = lane < HEAD_DIM

    for pi, (window, dil) in enumerate(DIL_PATTERNS):
        nblk = seq // (n * dil)
        for r in range(dil):
            for c in range(nblk):
                cur = pl.ds(r + dil * n * c, n, stride=dil)
                q2 = _stack_queries(qf[cur, :])
                if c > 0:
                    prev = pl.ds(r + dil * n * (c - 1), n, stride=dil)
                    kb = jnp.concatenate([kf[prev, :], kf[cur, :]], axis=0).astype(BF16)
                    vb = jnp.concatenate([vf[prev, :], vf[cur, :]], axis=0).astype(BF16)
                    bias = bias_ref[pi]
                else:
                    kb = kf[cur, :].astype(BF16)
                    vb = vf[cur, :].astype(BF16)
                    bias = bias_ref[pi, :, n:]
                s = _qk(q2, kb) + bias
                mx = jnp.max(s, axis=1, keepdims=True)
                p = jnp.exp(s - mx)
                l = jnp.sum(p, axis=1, keepdims=True)
                o = jnp.dot(p.astype(BF16), vb, preferred_element_type=F32)
                o_s[pi, cur, :] = jnp.where(low, o[:n], o[n:])
                m_s[pi, cur, :] = jnp.where(low, mx[:n], mx[n:])
                l_s[pi, cur, :] = jnp.where(low, l[:n], l[n:])

    m_all = jnp.maximum(jnp.maximum(m_s[0], m_s[1]), m_s[2])
    num = jnp.zeros((seq, LANES), F32)
    den = jnp.zeros((seq, LANES), F32)
    for pi in range(len(DIL_PATTERNS)):
        w = jnp.exp(m_s[pi] - m_all)
        num = num + w * o_s[pi]
        den = den + w * l_s[pi]
    o_ref[...] = (num / den).astype(BF16)


def _dil_bias_tiles(slopes):
    n = DIL_BLOCK
    qi = (np.arange(2 * n) % n)[:, None]
    ki = np.arange(2 * n)[None, :]
    step = n + qi - ki
    valid = (step >= 0) & (step <= n)
    tiles = np.zeros((len(DIL_PATTERNS), DIL_HEADS // 2, 2 * n, 2 * n), np.float64)
    for pi, (_, dil) in enumerate(DIL_PATTERNS):
        for p in range(DIL_HEADS // 2):
            sl = np.where(np.arange(2 * n) < n, slopes[2 * p], slopes[2 * p + 1])[:, None]
            tiles[pi, p] = np.where(valid, -sl * (step * dil), MASKED)
    return tiles.astype(np.float32)


def _dil_attention(ob, batch, seq):
    m = ob.shape[0]
    _, slopes = _alibi_slopes()
    bias = jnp.asarray(_dil_bias_tiles(slopes))
    npairs = DIL_HEADS // 2
    npat = len(DIL_PATTERNS)
    col0 = 3 * GROUP_WIDTH // LANES
    kern = functools.partial(_dil_kernel, seq=seq)
    return pl.pallas_call(
        kern,
        grid=(batch, npairs),
        in_specs=[
            pl.BlockSpec((seq, LANES), lambda b, p: (b, col0 + p)),
            pl.BlockSpec((seq, LANES), lambda b, p: (b, col0 + npairs + p)),
            pl.BlockSpec((seq, LANES), lambda b, p: (b, col0 + 2 * npairs + p)),
            pl.BlockSpec((npat, None, 2 * DIL_BLOCK, 2 * DIL_BLOCK), lambda b, p: (0, p, 0, 0)),
        ],
        out_specs=pl.BlockSpec((seq, LANES), lambda b, p: (b, p)),
        out_shape=jax.ShapeDtypeStruct((m, GROUP_WIDTH), BF16),
        scratch_shapes=[
            pltpu.VMEM((seq, LANES), F32),
            pltpu.VMEM((seq, LANES), F32),
            pltpu.VMEM((seq, LANES), F32),
            pltpu.VMEM((npat, seq, LANES), F32),
            pltpu.VMEM((npat, seq, LANES), F32),
            pltpu.VMEM((npat, seq, LANES), F32),
        ],
        compiler_params=_params(("parallel", "parallel")),
        name="dil_attn",
    )(ob, ob, ob, bias)


def _conv_kernel(a_ref, g_ref, ap_ref, gp_ref, w_ref, b_ref, n_ref, o_ref, h_s):
    i = pl.program_id(1)
    tr, halo = CONV_ROWS, CONV_HALO
    hp = ap_ref[...] * jax.nn.sigmoid(gp_ref[...])
    h_s[0:halo, :] = jnp.where(i > 0, hp, 0.0)
    h_s[halo:, :] = a_ref[...] * jax.nn.sigmoid(g_ref[...])
    acc = jnp.zeros((tr, GROUP_WIDTH), F32)
    for j in range(CONV_WIDTH):
        off = halo - (CONV_WIDTH - 1) + j
        acc = acc + w_ref[j:j + 1, :] * h_s[off:off + tr, :]
    y = _rms(acc + b_ref[...], n_ref[...])
    o_ref[...] = (y * jax.nn.sigmoid(y)).astype(BF16)


def _conv(of, w, b, g, batch, seq):
    m = of.shape[0]
    tr, halo = CONV_ROWS, CONV_HALO
    nt = seq // tr
    per = tr // halo
    cur = lambda col: (lambda bb, i: (bb * nt + i, col))
    prev = lambda col: (lambda bb, i: (jnp.maximum((bb * nt + i) * per - 1, 0), col))
    return pl.pallas_call(
        _conv_kernel,
        grid=(batch, nt),
        in_specs=[
            pl.BlockSpec((tr, GROUP_WIDTH), cur(2)),
            pl.BlockSpec((tr, GROUP_WIDTH), cur(3)),
            pl.BlockSpec((halo, GROUP_WIDTH), prev(2)),
            pl.BlockSpec((halo, GROUP_WIDTH), prev(3)),
            pl.BlockSpec((CONV_WIDTH, GROUP_WIDTH), lambda bb, i: (0, 0)),
            pl.BlockSpec((1, GROUP_WIDTH), lambda bb, i: (0, 0)),
            pl.BlockSpec((1, GROUP_WIDTH), lambda bb, i: (0, 0)),
        ],
        out_specs=pl.BlockSpec((tr, GROUP_WIDTH), lambda bb, i: (bb * nt + i, 0)),
        out_shape=jax.ShapeDtypeStruct((m, GROUP_WIDTH), BF16),
        scratch_shapes=[pltpu.VMEM((halo + tr, GROUP_WIDTH), F32)],
        compiler_params=_params(("parallel", "parallel")),
        name="conformer_conv",
    )(of, of, of, of, w, b, g)


def _out_kernel(a_ref, b_ref, c_ref, d_ref, w_ref, x_ref, g_ref, o_ref):
    gw = GROUP_WIDTH
    y = jnp.dot(a_ref[...], w_ref[0:gw, :], preferred_element_type=F32)
    y = y + jnp.dot(b_ref[...], w_ref[gw:2 * gw, :], preferred_element_type=F32)
    y = y + jnp.dot(c_ref[...], w_ref[2 * gw:3 * gw, :], preferred_element_type=F32)
    y = y + jnp.dot(d_ref[...], w_ref[3 * gw:4 * gw, :], preferred_element_type=F32)
    o_ref[...] = x_ref[...] + _rms(y, g_ref[...])


def _out_proj(mix, w, x, g):
    m = x.shape[0]
    tm = TM_OUT
    mix_spec = pl.BlockSpec((tm, GROUP_WIDTH), lambda i: (i, 0))
    return pl.pallas_call(
        _out_kernel,
        grid=(m // tm,),
        in_specs=[mix_spec, mix_spec, mix_spec, mix_spec,
                  pl.BlockSpec((D_MODEL, D_MODEL), lambda i: (0, 0)),
                  pl.BlockSpec((tm, D_MODEL), lambda i: (i, 0)),
                  pl.BlockSpec((1, D_MODEL), lambda i: (0, 0))],
        out_specs=pl.BlockSpec((tm, D_MODEL), lambda i: (i, 0)),
        out_shape=jax.ShapeDtypeStruct((m, D_MODEL), F32),
        compiler_params=_params(("parallel",)),
        name="out_proj",
    )(*mix, w, x, g)


def _ffn_kernel(x_ref, gpre_ref, w1_ref, w2_ref, gpost_ref, o_ref, h_s, acc_s):
    j = pl.program_id(1)

    @pl.when(j == 0)
    def _():
        h_s[...] = _rms(x_ref[...], gpre_ref[...]).astype(BF16)
        acc_s[...] = jnp.zeros(acc_s.shape, F32)

    f = jnp.dot(h_s[...], w1_ref[...], preferred_element_type=F32)
    f = jnp.square(jnp.maximum(f, 0.0)).astype(BF16)
    acc_s[...] += jnp.dot(f, w2_ref[...], preferred_element_type=F32)

    @pl.when(j == pl.num_programs(1) - 1)
    def _():
        o_ref[...] = x_ref[...] + _rms(acc_s[...], gpost_ref[...])


def _ffn(x, gpre, w1, w2, gpost):
    m = x.shape[0]
    tm, tf = TM_FFN, TF_FFN
    return pl.pallas_call(
        _ffn_kernel,
        grid=(m // tm, D_FF // tf),
        in_specs=[
            pl.BlockSpec((tm, D_MODEL), lambda i, j: (i, 0)),
            pl.BlockSpec((1, D_MODEL), lambda i, j: (0, 0)),
            pl.BlockSpec((D_MODEL, tf), lambda i, j: (0, j)),
            pl.BlockSpec((tf, D_MODEL), lambda i, j: (j, 0)),
            pl.BlockSpec((1, D_MODEL), lambda i, j: (0, 0)),
        ],
        out_specs=pl.BlockSpec((tm, D_MODEL), lambda i, j: (i, 0)),
        out_shape=jax.ShapeDtypeStruct((m, D_MODEL), F32),
        scratch_shapes=[pltpu.VMEM((tm, D_MODEL), BF16), pltpu.VMEM((tm, D_MODEL), F32)],
        compiler_params=_params(("parallel", "arbitrary")),
        name="ffn",
    )(x, gpre, w1, w2, gpost)


def kernel(x, g_mix_pre, g_mix_post, w_in, gmlp_w, gmlp_b, diff_lam, diff_subln, conv_w, conv_b,
           conv_norm, w_out, g_ffn_pre, g_ffn_post, w_ff1, w_ff2):
    batch, seq, _ = x.shape
    depth = w_in.shape[0]
    xf = x.reshape(batch * seq, D_MODEL)
    row = lambda v: v.reshape(1, -1)
    for li in range(depth):
        of, ob = _in_proj(xf, row(g_mix_pre[li]), w_in[li].astype(BF16))

        w_t = gmlp_w[li].transpose(1, 0, 2).reshape(GMLP_CHUNK, GMLP_GROUPS * GMLP_CHUNK)
        bias_full = jnp.repeat(gmlp_b[li].T, GROUP_WIDTH // GMLP_GROUPS, axis=1)
        out_a = _gmlp(of, w_t, bias_full)

        lam_init = 0.8 - 0.6 * float(np.exp(-0.3 * li))
        out_b = _diff_attention(ob, diff_lam[li], row(diff_subln[li]), lam_init, batch, seq)
        out_c = _dil_attention(ob, batch, seq)
        out_d = _conv(of, conv_w[li], row(conv_b[li]), row(conv_norm[li]), batch, seq)

        xf = _out_proj((out_a, out_b, out_c, out_d), w_out[li].astype(BF16), xf,
                       row(g_mix_post[li]))
        xf = _ffn(xf, row(g_ffn_pre[li]), w_ff1[li].astype(BF16), w_ff2[li].astype(BF16),
                  row(g_ffn_post[li]))
    return xf.reshape(batch, seq, D_MODEL)
```

```python
import functools

import numpy as np
import jax
import jax.numpy as jnp
from jax import lax
from jax.experimental import pallas as pl
from jax.experimental.pallas import tpu as pltpu

F32 = jnp.float32
BF16 = jnp.bfloat16

D_MODEL = 2048
GROUP_WIDTH = 512
N_IN_SPLITS = 10
D_FF = 4 * D_MODEL
NORM_EPS = 1e-6
GMLP_CHUNK = 128
GMLP_GROUPS = 8
DIFF_HEADS = 4
DIL_HEADS = 8
HEAD_DIM = 64
DIL_PATTERNS = ((128, 1), (512, 4), (2048, 16))
DIL_BLOCK = 128
CONV_WIDTH = 31
N_ALIBI_HEADS = DIFF_HEADS + DIL_HEADS
ATTN_SCALE = HEAD_DIM ** -0.5
MASKED = -1e30

LANES = 128
VMEM_LIMIT = 48 * 1024 * 1024
VMEM_LIMIT_FFN = 58 * 1024 * 1024

TM_PROJ = 1024
TM_OUT = 512
TM_FFN = 1024
TF_FFN = 512
FFN_OUT_CHUNK = 512
GMLP_ROWS = 512
DIFF_TQ = 256
DIFF_TK = 256
CONV_ROWS = 256
CONV_HALO = 32


def _alibi_slopes():
    i = np.arange(1, N_ALIBI_HEADS + 1, dtype=np.float64)
    s = 2.0 ** (-8.0 * i / N_ALIBI_HEADS)
    diff_idx = np.arange(0, N_ALIBI_HEADS, 3)
    dil_idx = np.array([j for j in range(N_ALIBI_HEADS) if j % 3 != 0])
    return s[diff_idx], s[dil_idx]


def _rms(x, g):
    return x * lax.rsqrt(jnp.mean(x * x, axis=-1, keepdims=True) + NORM_EPS) * g


def _params(sem):
    return pltpu.CompilerParams(dimension_semantics=sem, vmem_limit_bytes=VMEM_LIMIT)


def _proj_kernel(x_ref, g_ref, w_ref, of_ref, ob_ref, h_ref):
    j = pl.program_id(1)

    @pl.when(j == 0)
    def _():
        h_ref[...] = _rms(x_ref[...], g_ref[...]).astype(BF16)

    y = jnp.dot(h_ref[...], w_ref[...].astype(BF16), preferred_element_type=F32)

    @pl.when(j < 4)
    def _():
        of_ref[...] = y

    @pl.when(j >= 4)
    def _():
        ob_ref[...] = y.astype(BF16)


def _in_proj(x, g, w, li):
    m = x.shape[0]
    tm, tn = TM_PROJ, GROUP_WIDTH
    wcol = lambda i, j: (li, 0, jnp.where(j < 2, j, jnp.where(j < 4, j + 6, j - 2)))
    return pl.pallas_call(
        _proj_kernel,
        grid=(m // tm, N_IN_SPLITS),
        in_specs=[
            pl.BlockSpec((tm, D_MODEL), lambda i, j: (i, 0)),
            pl.BlockSpec((1, D_MODEL), lambda i, j: (0, 0)),
            pl.BlockSpec((None, D_MODEL, tn), wcol),
        ],
        out_specs=[
            pl.BlockSpec((tm, tn), lambda i, j: (i, jnp.minimum(j, 3))),
            pl.BlockSpec((tm, tn), lambda i, j: (i, jnp.maximum(j - 4, 0))),
        ],
        out_shape=[
            jax.ShapeDtypeStruct((m, 4 * GROUP_WIDTH), F32),
            jax.ShapeDtypeStruct((m, 6 * GROUP_WIDTH), BF16),
        ],
        scratch_shapes=[pltpu.VMEM((tm, D_MODEL), BF16)],
        compiler_params=_params(("parallel", "arbitrary")),
        name="in_proj",
    )(x, g, w)


def _gelu(x):
    return jax.nn.gelu(x, approximate=True)


def _gmlp_kernel(u_ref, v_ref, w_ref, b_ref, o_ref):
    c = GMLP_CHUNK
    t_idx = lax.broadcasted_iota(jnp.int32, w_ref.shape, 0)
    s_idx = lax.broadcasted_iota(jnp.int32, w_ref.shape, 1) % c
    w = jnp.where(s_idx <= t_idx, w_ref[...], 0.0).astype(BF16)
    bias = b_ref[...]
    lane = lax.broadcasted_iota(jnp.int32, (c, LANES), 1)
    low = lane < HEAD_DIM
    for ci in range(GMLP_ROWS // c):
        rows = slice(ci * c, (ci + 1) * c)
        u = _gelu(u_ref[rows, :])
        v = _gelu(v_ref[rows, :])
        mu = jnp.mean(v, axis=-1, keepdims=True)
        vc = v - mu
        v = vc * lax.rsqrt(jnp.mean(vc * vc, axis=-1, keepdims=True) + NORM_EPS)
        zs = []
        for p in range(GMLP_GROUPS // 2):
            vp = v[:, p * LANES:(p + 1) * LANES]
            rhs = jnp.concatenate([jnp.where(low, vp, 0.0), jnp.where(low, 0.0, vp)], axis=0)
            zs.append(jnp.dot(w[:, p * 2 * c:(p + 1) * 2 * c], rhs.astype(BF16),
                              preferred_element_type=F32))
        z = jnp.concatenate(zs, axis=1) + bias
        o_ref[rows, :] = (u * z).astype(BF16)


def _gmlp(of, w_t, bias_full):
    m = of.shape[0]
    tr = GMLP_ROWS
    return pl.pallas_call(
        _gmlp_kernel,
        grid=(m // tr,),
        in_specs=[
            pl.BlockSpec((tr, GROUP_WIDTH), lambda i: (i, 0)),
            pl.BlockSpec((tr, GROUP_WIDTH), lambda i: (i, 1)),
            pl.BlockSpec((GMLP_CHUNK, GMLP_GROUPS * GMLP_CHUNK), lambda i: (0, 0)),
            pl.BlockSpec((GMLP_CHUNK, GROUP_WIDTH), lambda i: (0, 0)),
        ],
        out_specs=pl.BlockSpec((tr, GROUP_WIDTH), lambda i: (i, 0)),
        out_shape=jax.ShapeDtypeStruct((m, GROUP_WIDTH), BF16),
        compiler_params=_params(("parallel",)),
        name="gmlp",
    )(of, of, w_t, bias_full)


def _stack_queries(q):
    lane = lax.broadcasted_iota(jnp.int32, q.shape, 1)
    low = lane < HEAD_DIM
    zero = jnp.zeros_like(q)
    q2 = jnp.concatenate([jnp.where(low, q, zero), jnp.where(low, zero, q)], axis=0)
    return (q2 * ATTN_SCALE).astype(BF16)


def _qk(q2, k):
    return lax.dot_general(q2, k, (((1,), (1,)), ((), ())), preferred_element_type=F32)


def _diff_kernel(q_ref, k_ref, v_ref, bias_ref, lam_ref, g_ref, o_ref, vt_s, acc_s, *,
                 lam_init, slopes):
    qi = pl.program_id(1)
    tq, tk = DIFF_TQ, DIFF_TK
    seq = k_ref.shape[0]
    heads = range(DIFF_HEADS)
    hcols = lambda h: slice(h * LANES, (h + 1) * LANES)

    @pl.when(qi == 0)
    def _():
        for h in heads:
            for c in range(seq // tk):
                rows = slice(c * tk, (c + 1) * tk)
                vt_s[hcols(h), rows] = v_ref[rows, hcols(h)].astype(F32).T.astype(BF16)

    q2 = [_stack_queries(q_ref[:, hcols(h)]) for h in heads]
    acc_s[...] = jnp.zeros(acc_s.shape, F32)

    def step(start, which, dist, stats):
        sts = [_qk(k_ref[pl.ds(start, tk), hcols(h)], q2[h]) + bias_ref[h, which]
               for h in heads]
        shifts = [-slopes[h] * dist for h in heads]
        m_next = [jnp.maximum(stats[h][0], jnp.max(sts[h], axis=0, keepdims=True) + shifts[h])
                  for h in heads]
        ps = [jnp.exp(sts[h] - (m_next[h] - shifts[h])) for h in heads]
        alphas = [jnp.exp(stats[h][0] - m_next[h]) for h in heads]
        l_next = [alphas[h] * stats[h][1] + jnp.sum(ps[h], axis=0, keepdims=True)
                  for h in heads]
        pvs = [jnp.dot(vt_s[hcols(h), pl.ds(start, tk)], ps[h].astype(BF16),
                       preferred_element_type=F32) for h in heads]
        for h in heads:
            acc_s[h] = alphas[h] * acc_s[h] + pvs[h]
        return tuple((m_next[h], l_next[h]) for h in heads)

    def body(j, stats):
        return step(pl.multiple_of(j * tk, tk), 0, ((qi - j) * tq).astype(F32), stats)

    m0 = jnp.full((1, 2 * tq), MASKED, F32)
    l0 = jnp.zeros((1, 2 * tq), F32)
    stats = lax.fori_loop(0, qi, body, tuple((m0, l0) for _ in heads))
    stats = step(pl.multiple_of(qi * tq, tq), 1, 0.0, stats)

    lp = lam_ref[...]
    lam = (jnp.exp(jnp.sum(lp[0:1] * lp[1:2], axis=1, keepdims=True))
           - jnp.exp(jnp.sum(lp[2:3] * lp[3:4], axis=1, keepdims=True)) + lam_init)
    for h in heads:
        o = acc_s[h] / stats[h][1]
        o = o[:, :tq] - lam * o[:, tq:]
        y = o * lax.rsqrt(jnp.mean(o * o, axis=0, keepdims=True) + NORM_EPS) * g_ref[...]
        o_ref[:, hcols(h)] = (y * (1.0 - lam_init)).T.astype(BF16)


def _diff_bias_tiles(slopes):
    tq, tk = DIFF_TQ, DIFF_TK
    rq = (np.arange(2 * tq) % tq)[None, :].astype(np.float64)
    c = np.arange(tk)[:, None].astype(np.float64)
    tiles = np.zeros((DIFF_HEADS, 2, tk, 2 * tq), np.float64)
    for h, sl in enumerate(slopes):
        tiles[h, 0] = -sl * (rq - c)
        tiles[h, 1] = np.where(rq - c >= 0, -sl * (rq - c), MASKED)
    return tiles.astype(np.float32)


def _diff_attention(ob, lam_p, subln_g, lam_init, batch, seq):
    m = ob.shape[0]
    tq, tk = DIFF_TQ, DIFF_TK
    assert tq == tk and seq % tq == 0
    nq = seq // tq
    slopes, _ = _alibi_slopes()
    bias = jnp.asarray(_diff_bias_tiles(slopes))
    kern = functools.partial(_diff_kernel, lam_init=lam_init,
                             slopes=tuple(float(s) for s in slopes))
    return pl.pallas_call(
        kern,
        grid=(batch, nq),
        in_specs=[
            pl.BlockSpec((tq, GROUP_WIDTH), lambda b, i: (b * nq + i, 0)),
            pl.BlockSpec((seq, GROUP_WIDTH), lambda b, i: (b, 1)),
            pl.BlockSpec((seq, GROUP_WIDTH), lambda b, i: (b, 2)),
            pl.BlockSpec((DIFF_HEADS, 2, tk, 2 * tq), lambda b, i: (0, 0, 0, 0)),
            pl.BlockSpec((4, HEAD_DIM), lambda b, i: (0, 0)),
            pl.BlockSpec((2 * HEAD_DIM, 1), lambda b, i: (0, 0)),
        ],
        out_specs=pl.BlockSpec((tq, GROUP_WIDTH), lambda b, i: (b * nq + i, 0)),
        out_shape=jax.ShapeDtypeStruct((m, GROUP_WIDTH), BF16),
        scratch_shapes=[
            pltpu.VMEM((GROUP_WIDTH, seq), BF16),
            pltpu.VMEM((DIFF_HEADS, LANES, 2 * tq), F32),
        ],
        compiler_params=_params(("parallel", "arbitrary")),
        name="diff_attn",
    )(ob, ob, ob, bias, lam_p, subln_g)


def _dil_kernel(q_ref, k_ref, v_ref, bias_ref, o_ref, qf, kf, vf, m_s, l_s, o_s, *, seq):
    n = DIL_BLOCK
    qf[...] = q_ref[...].astype(F32)
    kf[...] = k_ref[...].astype(F32)
    vf[...] = v_ref[...].astype(F32)
    lane = lax.broadcasted_iota(jnp.int32, (n, LANES), 1)
    low = lane < HEAD_DIM

    for pi, (window, dil) in enumerate(DIL_PATTERNS):
        nblk = seq // (n * dil)
        for r in range(dil):
            for c in range(nblk):
                cur = pl.ds(r + dil * n * c, n, stride=dil)
                q2 = _stack_queries(qf[cur, :])
                if c > 0:
                    prev = pl.ds(r + dil * n * (c - 1), n, stride=dil)
                    kb = jnp.concatenate([kf[prev, :], kf[cur, :]], axis=0).astype(BF16)
                    vb = jnp.concatenate([vf[prev, :], vf[cur, :]], axis=0).astype(BF16)
                    bias = bias_ref[pi]
                else:
                    kb = kf[cur, :].astype(BF16)
                    vb = vf[cur, :].astype(BF16)
                    bias = bias_ref[pi, :, n:]
                s = _qk(q2, kb) + bias
                mx = jnp.max(s, axis=1, keepdims=True)
                p = jnp.exp(s - mx)
                l = jnp.sum(p, axis=1, keepdims=True)
                o = jnp.dot(p.astype(BF16), vb, preferred_element_type=F32)
                o_s[pi, cur, :] = jnp.where(low, o[:n], o[n:])
                m_s[pi, cur, :] = jnp.where(low, mx[:n], mx[n:])
                l_s[pi, cur, :] = jnp.where(low, l[:n], l[n:])

    m_all = jnp.maximum(jnp.maximum(m_s[0], m_s[1]), m_s[2])
    num = jnp.zeros((seq, LANES), F32)
    den = jnp.zeros((seq, LANES), F32)
    for pi in range(len(DIL_PATTERNS)):
        w = jnp.exp(m_s[pi] - m_all)
        num = num + w * o_s[pi]
        den = den + w * l_s[pi]
    o_ref[...] = (num / den).astype(BF16)


def _dil_bias_tiles(slopes):
    n = DIL_BLOCK
    qi = (np.arange(2 * n) % n)[:, None]
    ki = np.arange(2 * n)[None, :]
    step = n + qi - ki
    valid = (step >= 0) & (step <= n)
    tiles = np.zeros((len(DIL_PATTERNS), DIL_HEADS // 2, 2 * n, 2 * n), np.float64)
    for pi, (_, dil) in enumerate(DIL_PATTERNS):
        for p in range(DIL_HEADS // 2):
            sl = np.where(np.arange(2 * n) < n, slopes[2 * p], slopes[2 * p + 1])[:, None]
            tiles[pi, p] = np.where(valid, -sl * (step * dil), MASKED)
    return tiles.astype(np.float32)


def _dil_attention(ob, batch, seq):
    m = ob.shape[0]
    _, slopes = _alibi_slopes()
    bias = jnp.asarray(_dil_bias_tiles(slopes))
    npairs = DIL_HEADS // 2
    npat = len(DIL_PATTERNS)
    col0 = 3 * GROUP_WIDTH // LANES
    kern = functools.partial(_dil_kernel, seq=seq)
    return pl.pallas_call(
        kern,
        grid=(batch, npairs),
        in_specs=[
            pl.BlockSpec((seq, LANES), lambda b, p: (b, col0 + p)),
            pl.BlockSpec((seq, LANES), lambda b, p: (b, col0 + npairs + p)),
            pl.BlockSpec((seq, LANES), lambda b, p: (b, col0 + 2 * npairs + p)),
            pl.BlockSpec((npat, None, 2 * DIL_BLOCK, 2 * DIL_BLOCK), lambda b, p: (0, p, 0, 0)),
        ],
        out_specs=pl.BlockSpec((seq, LANES), lambda b, p: (b, p)),
        out_shape=jax.ShapeDtypeStruct((m, GROUP_WIDTH), BF16),
        scratch_shapes=[
            pltpu.VMEM((seq, LANES), F32),
            pltpu.VMEM((seq, LANES), F32),
            pltpu.VMEM((seq, LANES), F32),
            pltpu.VMEM((npat, seq, LANES), F32),
            pltpu.VMEM((npat, seq, LANES), F32),
            pltpu.VMEM((npat, seq, LANES), F32),
        ],
        compiler_params=_params(("parallel", "parallel")),
        name="dil_attn",
    )(ob, ob, ob, bias)


def _conv_kernel(a_ref, g_ref, ap_ref, gp_ref, w_ref, b_ref, n_ref, o_ref, h_s):
    i = pl.program_id(1)
    tr, halo = CONV_ROWS, CONV_HALO
    hp = ap_ref[...] * jax.nn.sigmoid(gp_ref[...])
    h_s[0:halo, :] = jnp.where(i > 0, hp, 0.0)
    h_s[halo:, :] = a_ref[...] * jax.nn.sigmoid(g_ref[...])
    acc = jnp.zeros((tr, GROUP_WIDTH), F32)
    for j in range(CONV_WIDTH):
        off = halo - (CONV_WIDTH - 1) + j
        acc = acc + w_ref[j:j + 1, :] * h_s[off:off + tr, :]
    y = _rms(acc + b_ref[...], n_ref[...])
    o_ref[...] = (y * jax.nn.sigmoid(y)).astype(BF16)


def _conv(of, w, b, g, batch, seq):
    m = of.shape[0]
    tr, halo = CONV_ROWS, CONV_HALO
    nt = seq // tr
    per = tr // halo
    cur = lambda col: (lambda bb, i: (bb * nt + i, col))
    prev = lambda col: (lambda bb, i: (jnp.maximum((bb * nt + i) * per - 1, 0), col))
    return pl.pallas_call(
        _conv_kernel,
        grid=(batch, nt),
        in_specs=[
            pl.BlockSpec((tr, GROUP_WIDTH), cur(2)),
            pl.BlockSpec((tr, GROUP_WIDTH), cur(3)),
            pl.BlockSpec((halo, GROUP_WIDTH), prev(2)),
            pl.BlockSpec((halo, GROUP_WIDTH), prev(3)),
            pl.BlockSpec((CONV_WIDTH, GROUP_WIDTH), lambda bb, i: (0, 0)),
            pl.BlockSpec((1, GROUP_WIDTH), lambda bb, i: (0, 0)),
            pl.BlockSpec((1, GROUP_WIDTH), lambda bb, i: (0, 0)),
        ],
        out_specs=pl.BlockSpec((tr, GROUP_WIDTH), lambda bb, i: (bb * nt + i, 0)),
        out_shape=jax.ShapeDtypeStruct((m, GROUP_WIDTH), BF16),
        scratch_shapes=[pltpu.VMEM((halo + tr, GROUP_WIDTH), F32)],
        compiler_params=_params(("parallel", "parallel")),
        name="conformer_conv",
    )(of, of, of, of, w, b, g)


def _out_kernel(a_ref, b_ref, c_ref, d_ref, w_ref, x_ref, g_ref, o_ref):
    gw = GROUP_WIDTH
    y = jnp.dot(a_ref[...], w_ref[0:gw, :], preferred_element_type=F32)
    y = y + jnp.dot(b_ref[...], w_ref[gw:2 * gw, :], preferred_element_type=F32)
    y = y + jnp.dot(c_ref[...], w_ref[2 * gw:3 * gw, :], preferred_element_type=F32)
    y = y + jnp.dot(d_ref[...], w_ref[3 * gw:4 * gw, :], preferred_element_type=F32)
    o_ref[...] = x_ref[...] + _rms(y, g_ref[...])


def _out_proj(mix, w, x, g, li):
    m = x.shape[0]
    tm = TM_OUT
    mix_spec = pl.BlockSpec((tm, GROUP_WIDTH), lambda i: (i, 0))
    return pl.pallas_call(
        _out_kernel,
        grid=(m // tm,),
        in_specs=[mix_spec, mix_spec, mix_spec, mix_spec,
                  pl.BlockSpec((None, D_MODEL, D_MODEL), lambda i: (li, 0, 0)),
                  pl.BlockSpec((tm, D_MODEL), lambda i: (i, 0)),
                  pl.BlockSpec((1, D_MODEL), lambda i: (0, 0))],
        out_specs=pl.BlockSpec((tm, D_MODEL), lambda i: (i, 0)),
        out_shape=jax.ShapeDtypeStruct((m, D_MODEL), F32),
        compiler_params=_params(("parallel",)),
        name="out_proj",
    )(*mix, w, x, g)


def _ffn_kernel(x_ref, gpre_ref, w1_ref, w2_ref, gpost_ref, o_ref, h_s):
    j = pl.program_id(1)

    @pl.when(j == 0)
    def _():
        h_s[...] = _rms(x_ref[...], gpre_ref[...]).astype(BF16)
        o_ref[...] = jnp.zeros(o_ref.shape, F32)

    f = jnp.dot(h_s[...], w1_ref[...], preferred_element_type=F32)
    f = jnp.square(jnp.maximum(f, 0.0)).astype(BF16)
    for c in range(D_MODEL // FFN_OUT_CHUNK):
        cols = slice(c * FFN_OUT_CHUNK, (c + 1) * FFN_OUT_CHUNK)
        o_ref[:, cols] += jnp.dot(f, w2_ref[:, cols], preferred_element_type=F32)

    @pl.when(j == pl.num_programs(1) - 1)
    def _():
        o_ref[...] = x_ref[...] + _rms(o_ref[...], gpost_ref[...])


def _ffn(x, gpre, w1, w2, gpost, li):
    m = x.shape[0]
    tm, tf = TM_FFN, TF_FFN
    return pl.pallas_call(
        _ffn_kernel,
        grid=(m // tm, D_FF // tf),
        in_specs=[
            pl.BlockSpec((tm, D_MODEL), lambda i, j: (i, 0)),
            pl.BlockSpec((1, D_MODEL), lambda i, j: (0, 0)),
            pl.BlockSpec((None, D_MODEL, tf), lambda i, j: (li, 0, j)),
            pl.BlockSpec((None, tf, D_MODEL), lambda i, j: (li, j, 0)),
            pl.BlockSpec((1, D_MODEL), lambda i, j: (0, 0)),
        ],
        out_specs=pl.BlockSpec((tm, D_MODEL), lambda i, j: (i, 0)),
        out_shape=jax.ShapeDtypeStruct((m, D_MODEL), F32),
        scratch_shapes=[pltpu.VMEM((tm, D_MODEL), BF16)],
        compiler_params=pltpu.CompilerParams(dimension_semantics=("parallel", "arbitrary"),
                                             vmem_limit_bytes=VMEM_LIMIT_FFN),
        name="ffn",
    )(x, gpre, w1, w2, gpost)


def kernel(x, g_mix_pre, g_mix_post, w_in, gmlp_w, gmlp_b, diff_lam, diff_subln, conv_w, conv_b,
           conv_norm, w_out, g_ffn_pre, g_ffn_post, w_ff1, w_ff2):
    batch, seq, _ = x.shape
    depth = w_in.shape[0]
    xf = x.reshape(batch * seq, D_MODEL)
    row = lambda v: v.reshape(1, -1)
    w_out_b, w_ff1_b, w_ff2_b = w_out.astype(BF16), w_ff1.astype(BF16), w_ff2.astype(BF16)
    for li in range(depth):
        of, ob = _in_proj(xf, row(g_mix_pre[li]), w_in, li)

        w_t = gmlp_w[li].transpose(1, 0, 2).reshape(GMLP_CHUNK, GMLP_GROUPS * GMLP_CHUNK)
        bias_full = jnp.repeat(gmlp_b[li].T, GROUP_WIDTH // GMLP_GROUPS, axis=1)
        out_a = _gmlp(of, w_t, bias_full)

        lam_init = 0.8 - 0.6 * float(np.exp(-0.3 * li))
        out_b = _diff_attention(ob, diff_lam[li], diff_subln[li].reshape(-1, 1), lam_init,
                                batch, seq)
        out_c = _dil_attention(ob, batch, seq)
        out_d = _conv(of, conv_w[li], row(conv_b[li]), row(conv_norm[li]), batch, seq)

        xf = _out_proj((out_a, out_b, out_c, out_d), w_out_b, xf, row(g_mix_post[li]), li)
        xf = _ffn(xf, row(g_ffn_pre[li]), w_ff1_b, w_ff2_b, row(g_ffn_post[li]), li)
    return xf.reshape(batch, seq, D_MODEL)
```

```python
import functools

import numpy as np
import jax
import jax.numpy as jnp
from jax import lax
from jax.experimental import pallas as pl
from jax.experimental.pallas import tpu as pltpu

F32 = jnp.float32
BF16 = jnp.bfloat16

D_MODEL = 2048
GROUP_WIDTH = 512
N_IN_SPLITS = 10
D_FF = 4 * D_MODEL
NORM_EPS = 1e-6
GMLP_CHUNK = 128
GMLP_GROUPS = 8
DIFF_HEADS = 4
DIL_HEADS = 8
HEAD_DIM = 64
DIL_PATTERNS = ((128, 1), (512, 4), (2048, 16))
DIL_BLOCK = 128
CONV_WIDTH = 31
N_ALIBI_HEADS = DIFF_HEADS + DIL_HEADS
ATTN_SCALE = HEAD_DIM ** -0.5
MASKED = -1e30
LOG2E = 1.4426950408889634
DIFF_ONES_ROWS = 16
DIFF_HEAD_GROUP = 4

LANES = 128
VMEM_LIMIT = 48 * 1024 * 1024
VMEM_LIMIT_FFN = 58 * 1024 * 1024

TM_PROJ = 1024
TM_OUT = 512
TM_FFN = 1024
TF_FFN = 512
FFN_OUT_CHUNK = 512
GMLP_ROWS = 512
DIFF_TQ = 256
DIFF_TK = 256
CONV_ROWS = 256
CONV_HALO = 32
CONV_CHUNK = 32
SUBLANES = 8


def _alibi_slopes():
    i = np.arange(1, N_ALIBI_HEADS + 1, dtype=np.float64)
    s = 2.0 ** (-8.0 * i / N_ALIBI_HEADS)
    diff_idx = np.arange(0, N_ALIBI_HEADS, 3)
    dil_idx = np.array([j for j in range(N_ALIBI_HEADS) if j % 3 != 0])
    return s[diff_idx], s[dil_idx]


def _rms(x, g):
    return x * lax.rsqrt(jnp.mean(x * x, axis=-1, keepdims=True) + NORM_EPS) * g


def _params(sem):
    return pltpu.CompilerParams(dimension_semantics=sem, vmem_limit_bytes=VMEM_LIMIT)


def _proj_kernel(x_ref, g_ref, w_ref, of_ref, ob_ref, h_ref):
    j = pl.program_id(1)

    @pl.when(j == 0)
    def _():
        h_ref[...] = _rms(x_ref[...], g_ref[...]).astype(BF16)

    y = jnp.dot(h_ref[...], w_ref[...].astype(BF16), preferred_element_type=F32)

    @pl.when(j < 4)
    def _():
        of_ref[...] = y

    @pl.when(j >= 4)
    def _():
        ob_ref[...] = y.astype(BF16)


def _in_proj(x, g, w, li):
    m = x.shape[0]
    tm, tn = TM_PROJ, GROUP_WIDTH
    wcol = lambda i, j: (li, 0, jnp.where(j < 2, j, jnp.where(j < 4, j + 6, j - 2)))
    return pl.pallas_call(
        _proj_kernel,
        grid=(m // tm, N_IN_SPLITS),
        in_specs=[
            pl.BlockSpec((tm, D_MODEL), lambda i, j: (i, 0)),
            pl.BlockSpec((1, D_MODEL), lambda i, j: (0, 0)),
            pl.BlockSpec((None, D_MODEL, tn), wcol),
        ],
        out_specs=[
            pl.BlockSpec((tm, tn), lambda i, j: (i, jnp.minimum(j, 3))),
            pl.BlockSpec((tm, tn), lambda i, j: (i, jnp.maximum(j - 4, 0))),
        ],
        out_shape=[
            jax.ShapeDtypeStruct((m, 4 * GROUP_WIDTH), F32),
            jax.ShapeDtypeStruct((m, 6 * GROUP_WIDTH), BF16),
        ],
        scratch_shapes=[pltpu.VMEM((tm, D_MODEL), BF16)],
        compiler_params=_params(("parallel", "arbitrary")),
        name="in_proj",
    )(x, g, w)


def _gelu(x):
    return jax.nn.gelu(x, approximate=True)


def _gmlp_kernel(u_ref, v_ref, w_ref, b_ref, o_ref):
    c = GMLP_CHUNK
    t_idx = lax.broadcasted_iota(jnp.int32, w_ref.shape, 0)
    s_idx = lax.broadcasted_iota(jnp.int32, w_ref.shape, 1) % c
    w = jnp.where(s_idx <= t_idx, w_ref[...], 0.0).astype(BF16)
    bias = b_ref[...]
    lane = lax.broadcasted_iota(jnp.int32, (c, LANES), 1)
    low = lane < HEAD_DIM
    for ci in range(GMLP_ROWS // c):
        rows = slice(ci * c, (ci + 1) * c)
        u = _gelu(u_ref[rows, :])
        v = _gelu(v_ref[rows, :])
        mu = jnp.mean(v, axis=-1, keepdims=True)
        vc = v - mu
        v = vc * lax.rsqrt(jnp.mean(vc * vc, axis=-1, keepdims=True) + NORM_EPS)
        zs = []
        for p in range(GMLP_GROUPS // 2):
            vp = v[:, p * LANES:(p + 1) * LANES]
            rhs = jnp.concatenate([jnp.where(low, vp, 0.0), jnp.where(low, 0.0, vp)], axis=0)
            zs.append(jnp.dot(w[:, p * 2 * c:(p + 1) * 2 * c], rhs.astype(BF16),
                              preferred_element_type=F32))
        z = jnp.concatenate(zs, axis=1) + bias
        o_ref[rows, :] = (u * z).astype(BF16)


def _gmlp(of, w_t, bias_full):
    m = of.shape[0]
    tr = GMLP_ROWS
    return pl.pallas_call(
        _gmlp_kernel,
        grid=(m // tr,),
        in_specs=[
            pl.BlockSpec((tr, GROUP_WIDTH), lambda i: (i, 0)),
            pl.BlockSpec((tr, GROUP_WIDTH), lambda i: (i, 1)),
            pl.BlockSpec((GMLP_CHUNK, GMLP_GROUPS * GMLP_CHUNK), lambda i: (0, 0)),
            pl.BlockSpec((GMLP_CHUNK, GROUP_WIDTH), lambda i: (0, 0)),
        ],
        out_specs=pl.BlockSpec((tr, GROUP_WIDTH), lambda i: (i, 0)),
        out_shape=jax.ShapeDtypeStruct((m, GROUP_WIDTH), BF16),
        compiler_params=_params(("parallel",)),
        name="gmlp",
    )(of, of, w_t, bias_full)


def _stack_queries(q):
    lane = lax.broadcasted_iota(jnp.int32, q.shape, 1)
    low = lane < HEAD_DIM
    zero = jnp.zeros_like(q)
    q2 = jnp.concatenate([jnp.where(low, q, zero), jnp.where(low, zero, q)], axis=0)
    return (q2.astype(F32) * (ATTN_SCALE * LOG2E)).astype(BF16)


def _qk(q2, k):
    return lax.dot_general(q2, k, (((1,), (1,)), ((), ())), preferred_element_type=F32)


def _diff_kernel(q_ref, k_ref, v_ref, bias_ref, lam_ref, g_ref, o_ref, vt_s, acc_s, *,
                 lam_init, slopes):
    qi = pl.program_id(1)
    tq, tk = DIFF_TQ, DIFF_TK
    seq = k_ref.shape[0]
    heads = range(DIFF_HEADS)
    hcols = lambda h: slice(h * LANES, (h + 1) * LANES)

    @pl.when(qi == 0)
    def _():
        for h in heads:
            for c in range(seq // tk):
                rows = slice(c * tk, (c + 1) * tk)
                vt_s[h, 0:LANES, rows] = v_ref[rows, hcols(h)].astype(F32).T.astype(BF16)
            vt_s[h, LANES:, :] = jnp.ones((DIFF_ONES_ROWS, seq), BF16)

    q2 = [_stack_queries(q_ref[:, hcols(h)]) for h in heads]
    acc_s[...] = jnp.zeros(acc_s.shape, F32)

    def step_group(hs, start, which, dist, m_prev):
        sts = {h: _qk(k_ref[pl.ds(start, tk), hcols(h)], q2[h]) + bias_ref[h, which]
               for h in hs}
        shifts = {h: (-slopes[h] * LOG2E) * dist for h in hs}
        m_next = {h: jnp.maximum(m_prev[h], jnp.max(sts[h], axis=0, keepdims=True) + shifts[h])
                  for h in hs}
        ps = {h: jnp.exp2(sts[h] - (m_next[h] - shifts[h])).astype(BF16) for h in hs}
        alphas = {h: jnp.exp2(m_prev[h] - m_next[h]) for h in hs}
        pvs = {h: jnp.dot(vt_s[h, :, pl.ds(start, tk)], ps[h], preferred_element_type=F32)
               for h in hs}
        for h in hs:
            acc_s[h] = alphas[h] * acc_s[h] + pvs[h]
        return m_next

    def step(start, which, dist, m_prev):
        m_next = {}
        for g in range(0, DIFF_HEADS, DIFF_HEAD_GROUP):
            m_next.update(step_group(range(g, g + DIFF_HEAD_GROUP), start, which, dist, m_prev))
        return tuple(m_next[h] for h in heads)

    def body(j, m_prev):
        return step(pl.multiple_of(j * tk, tk), 0, ((qi - j) * tq).astype(F32), m_prev)

    m0 = jnp.full((1, 2 * tq), MASKED, F32)
    m_prev = lax.fori_loop(0, qi, body, tuple(m0 for _ in heads))
    step(pl.multiple_of(qi * tq, tq), 1, 0.0, m_prev)

    lp = lam_ref[...]
    lam = (jnp.exp(jnp.sum(lp[0:1] * lp[1:2], axis=1, keepdims=True))
           - jnp.exp(jnp.sum(lp[2:3] * lp[3:4], axis=1, keepdims=True)) + lam_init)
    for h in heads:
        o = acc_s[h, 0:LANES, :] / acc_s[h, LANES:LANES + 1, :]
        o = o[:, :tq] - lam * o[:, tq:]
        y = o * lax.rsqrt(jnp.mean(o * o, axis=0, keepdims=True) + NORM_EPS) * g_ref[...]
        o_ref[:, hcols(h)] = (y * (1.0 - lam_init)).T.astype(BF16)


def _diff_bias_tiles(slopes):
    tq, tk = DIFF_TQ, DIFF_TK
    rq = (np.arange(2 * tq) % tq)[None, :].astype(np.float64)
    c = np.arange(tk)[:, None].astype(np.float64)
    tiles = np.zeros((DIFF_HEADS, 2, tk, 2 * tq), np.float64)
    for h, sl in enumerate(slopes):
        tiles[h, 0] = -sl * LOG2E * (rq - c)
        tiles[h, 1] = np.where(rq - c >= 0, -sl * LOG2E * (rq - c), MASKED)
    return tiles.astype(np.float32)


def _diff_attention(ob, lam_p, subln_g, lam_init, batch, seq):
    m = ob.shape[0]
    tq, tk = DIFF_TQ, DIFF_TK
    assert tq == tk and seq % tq == 0
    nq = seq // tq
    slopes, _ = _alibi_slopes()
    bias = jnp.asarray(_diff_bias_tiles(slopes))
    kern = functools.partial(_diff_kernel, lam_init=lam_init,
                             slopes=tuple(float(s) for s in slopes))
    return pl.pallas_call(
        kern,
        grid=(batch, nq),
        in_specs=[
            pl.BlockSpec((tq, GROUP_WIDTH), lambda b, i: (b * nq + i, 0)),
            pl.BlockSpec((seq, GROUP_WIDTH), lambda b, i: (b, 1)),
            pl.BlockSpec((seq, GROUP_WIDTH), lambda b, i: (b, 2)),
            pl.BlockSpec((DIFF_HEADS, 2, tk, 2 * tq), lambda b, i: (0, 0, 0, 0)),
            pl.BlockSpec((4, HEAD_DIM), lambda b, i: (0, 0)),
            pl.BlockSpec((2 * HEAD_DIM, 1), lambda b, i: (0, 0)),
        ],
        out_specs=pl.BlockSpec((tq, GROUP_WIDTH), lambda b, i: (b * nq + i, 0)),
        out_shape=jax.ShapeDtypeStruct((m, GROUP_WIDTH), BF16),
        scratch_shapes=[
            pltpu.VMEM((DIFF_HEADS, LANES + DIFF_ONES_ROWS, seq), BF16),
            pltpu.VMEM((DIFF_HEADS, LANES + DIFF_ONES_ROWS, 2 * tq), F32),
        ],
        compiler_params=_params(("parallel", "arbitrary")),
        name="diff_attn",
    )(ob, ob, ob, bias, lam_p, subln_g)


def _dil_kernel(q_ref, k_ref, v_ref, bias_ref, o_ref, qf, kf, vf, m_s, l_s, o_s, *, seq):
    n = DIL_BLOCK
    qf[...] = q_ref[...].astype(F32)
    kf[...] = k_ref[...].astype(F32)
    vf[...] = v_ref[...].astype(F32)
    lane = lax.broadcasted_iota(jnp.int32, (n, LANES), 1)
    low = lane < HEAD_DIM

    for pi, (window, dil) in enumerate(DIL_PATTERNS):
        nblk = seq // (n * dil)
        for r in range(dil):
            for c in range(nblk):
                cur = pl.ds(r + dil * n * c, n, stride=dil)
                q2 = _stack_queries(qf[cur, :])
                if c > 0:
                    prev = pl.ds(r + dil * n * (c - 1), n, stride=dil)
                    kb = jnp.concatenate([kf[prev, :], kf[cur, :]], axis=0).astype(BF16)
                    vb = jnp.concatenate([vf[prev, :], vf[cur, :]], axis=0).astype(BF16)
                    bias = bias_ref[pi]
                else:
                    kb = kf[cur, :].astype(BF16)
                    vb = vf[cur, :].astype(BF16)
                    bias = bias_ref[pi, :, n:]
                s = _qk(q2, kb) + bias
                mx = jnp.max(s, axis=1, keepdims=True)
                p = jnp.exp2(s - mx).astype(BF16)
                vb1 = jnp.concatenate([vb, jnp.ones(vb.shape, BF16)], axis=1)
                ol = jnp.dot(p, vb1, preferred_element_type=F32)
                o, l = ol[:, :LANES], ol[:, LANES:]
                o_s[pi, cur, :] = jnp.where(low, o[:n], o[n:])
                m_s[pi, cur, :] = jnp.where(low, mx[:n], mx[n:])
                l_s[pi, cur, :] = jnp.where(low, l[:n], l[n:])

    m_all = jnp.maximum(jnp.maximum(m_s[0], m_s[1]), m_s[2])
    num = jnp.zeros((seq, LANES), F32)
    den = jnp.zeros((seq, LANES), F32)
    for pi in range(len(DIL_PATTERNS)):
        w = jnp.exp2(m_s[pi] - m_all)
        num = num + w * o_s[pi]
        den = den + w * l_s[pi]
    o_ref[...] = (num / den).astype(BF16)


def _dil_bias_tiles(slopes):
    n = DIL_BLOCK
    qi = (np.arange(2 * n) % n)[:, None]
    ki = np.arange(2 * n)[None, :]
    step = n + qi - ki
    valid = (step >= 0) & (step <= n)
    tiles = np.zeros((len(DIL_PATTERNS), DIL_HEADS // 2, 2 * n, 2 * n), np.float64)
    for pi, (_, dil) in enumerate(DIL_PATTERNS):
        for p in range(DIL_HEADS // 2):
            sl = np.where(np.arange(2 * n) < n, slopes[2 * p], slopes[2 * p + 1])[:, None]
            tiles[pi, p] = np.where(valid, -sl * LOG2E * (step * dil), MASKED)
    return tiles.astype(np.float32)


def _dil_attention(ob, batch, seq):
    m = ob.shape[0]
    _, slopes = _alibi_slopes()
    bias = jnp.asarray(_dil_bias_tiles(slopes))
    npairs = DIL_HEADS // 2
    npat = len(DIL_PATTERNS)
    col0 = 3 * GROUP_WIDTH // LANES
    kern = functools.partial(_dil_kernel, seq=seq)
    return pl.pallas_call(
        kern,
        grid=(batch, npairs),
        in_specs=[
            pl.BlockSpec((seq, LANES), lambda b, p: (b, col0 + p)),
            pl.BlockSpec((seq, LANES), lambda b, p: (b, col0 + npairs + p)),
            pl.BlockSpec((seq, LANES), lambda b, p: (b, col0 + 2 * npairs + p)),
            pl.BlockSpec((npat, None, 2 * DIL_BLOCK, 2 * DIL_BLOCK), lambda b, p: (0, p, 0, 0)),
        ],
        out_specs=pl.BlockSpec((seq, LANES), lambda b, p: (b, p)),
        out_shape=jax.ShapeDtypeStruct((m, GROUP_WIDTH), BF16),
        scratch_shapes=[
            pltpu.VMEM((seq, LANES), F32),
            pltpu.VMEM((seq, LANES), F32),
            pltpu.VMEM((seq, LANES), F32),
            pltpu.VMEM((npat, seq, LANES), F32),
            pltpu.VMEM((npat, seq, LANES), F32),
            pltpu.VMEM((npat, seq, LANES), F32),
        ],
        compiler_params=_params(("parallel", "parallel")),
        name="dil_attn",
    )(ob, ob, ob, bias)


def _conv_kernel(a_ref, g_ref, ap_ref, gp_ref, w_ref, b_ref, n_ref, o_ref, h_s):
    i = pl.program_id(1)
    tr, halo = CONV_ROWS, CONV_HALO
    hp = ap_ref[...] * jax.nn.sigmoid(gp_ref[...])
    h_s[0:halo, :] = jnp.where(i > 0, hp, 0.0)
    h_s[halo:, :] = a_ref[...] * jax.nn.sigmoid(g_ref[...])
    rc, sub, last = CONV_CHUNK, SUBLANES, CONV_WIDTH - 1

    def chunk(c, carry):
        base = pl.multiple_of(c * rc, rc)
        y = jnp.zeros((rc, GROUP_WIDTH), F32)
        for r in range(sub):
            a_r = None
            for q in range((last - r) // sub + 1):
                d = sub * q + r
                win = h_s[pl.ds(base + (halo - sub - sub * q), rc + sub), :]
                term = w_ref[last - d] * win.reshape(rc // sub + 1, sub, GROUP_WIDTH)
                a_r = term if a_r is None else a_r + term
            y = y + a_r.reshape(rc + sub, GROUP_WIDTH)[sub - r:sub - r + rc, :]
        y = _rms(y + b_ref[...], n_ref[...])
        o_ref[pl.ds(base, rc), :] = (y * jax.nn.sigmoid(y)).astype(BF16)
        return carry

    lax.fori_loop(0, tr // rc, chunk, 0)


def _conv(of, w, b, g, batch, seq):
    m = of.shape[0]
    tr, halo = CONV_ROWS, CONV_HALO
    nt = seq // tr
    per = tr // halo
    cur = lambda col: (lambda bb, i: (bb * nt + i, col))
    prev = lambda col: (lambda bb, i: (jnp.maximum((bb * nt + i) * per - 1, 0), col))
    return pl.pallas_call(
        _conv_kernel,
        grid=(batch, nt),
        in_specs=[
            pl.BlockSpec((tr, GROUP_WIDTH), cur(2)),
            pl.BlockSpec((tr, GROUP_WIDTH), cur(3)),
            pl.BlockSpec((halo, GROUP_WIDTH), prev(2)),
            pl.BlockSpec((halo, GROUP_WIDTH), prev(3)),
            pl.BlockSpec((CONV_WIDTH, SUBLANES, GROUP_WIDTH), lambda bb, i: (0, 0, 0)),
            pl.BlockSpec((1, GROUP_WIDTH), lambda bb, i: (0, 0)),
            pl.BlockSpec((1, GROUP_WIDTH), lambda bb, i: (0, 0)),
        ],
        out_specs=pl.BlockSpec((tr, GROUP_WIDTH), lambda bb, i: (bb * nt + i, 0)),
        out_shape=jax.ShapeDtypeStruct((m, GROUP_WIDTH), BF16),
        scratch_shapes=[pltpu.VMEM((halo + tr, GROUP_WIDTH), F32)],
        compiler_params=_params(("parallel", "parallel")),
        name="conformer_conv",
    )(of, of, of, of, jnp.broadcast_to(w[:, None, :], (CONV_WIDTH, SUBLANES, GROUP_WIDTH)), b, g)


def _out_kernel(a_ref, b_ref, c_ref, d_ref, w_ref, x_ref, g_ref, o_ref):
    gw = GROUP_WIDTH
    y = jnp.dot(a_ref[...], w_ref[0:gw, :], preferred_element_type=F32)
    y = y + jnp.dot(b_ref[...], w_ref[gw:2 * gw, :], preferred_element_type=F32)
    y = y + jnp.dot(c_ref[...], w_ref[2 * gw:3 * gw, :], preferred_element_type=F32)
    y = y + jnp.dot(d_ref[...], w_ref[3 * gw:4 * gw, :], preferred_element_type=F32)
    o_ref[...] = x_ref[...] + _rms(y, g_ref[...])


def _out_proj(mix, w, x, g, li):
    m = x.shape[0]
    tm = TM_OUT
    mix_spec = pl.BlockSpec((tm, GROUP_WIDTH), lambda i: (i, 0))
    return pl.pallas_call(
        _out_kernel,
        grid=(m // tm,),
        in_specs=[mix_spec, mix_spec, mix_spec, mix_spec,
                  pl.BlockSpec((None, D_MODEL, D_MODEL), lambda i: (li, 0, 0)),
                  pl.BlockSpec((tm, D_MODEL), lambda i: (i, 0)),
                  pl.BlockSpec((1, D_MODEL), lambda i: (0, 0))],
        out_specs=pl.BlockSpec((tm, D_MODEL), lambda i: (i, 0)),
        out_shape=jax.ShapeDtypeStruct((m, D_MODEL), F32),
        compiler_params=_params(("parallel",)),
        name="out_proj",
    )(*mix, w, x, g)


def _ffn_kernel(x_ref, gpre_ref, w1_ref, w2_ref, gpost_ref, o_ref, h_s):
    j = pl.program_id(1)

    @pl.when(j == 0)
    def _():
        h_s[...] = _rms(x_ref[...], gpre_ref[...]).astype(BF16)
        o_ref[...] = jnp.zeros(o_ref.shape, F32)

    f = jnp.dot(h_s[...], w1_ref[...], preferred_element_type=F32)
    f = jnp.square(jnp.maximum(f, 0.0)).astype(BF16)
    for c in range(D_MODEL // FFN_OUT_CHUNK):
        cols = slice(c * FFN_OUT_CHUNK, (c + 1) * FFN_OUT_CHUNK)
        o_ref[:, cols] += jnp.dot(f, w2_ref[:, cols], preferred_element_type=F32)

    @pl.when(j == pl.num_programs(1) - 1)
    def _():
        o_ref[...] = x_ref[...] + _rms(o_ref[...], gpost_ref[...])


def _ffn(x, gpre, w1, w2, gpost, li):
    m = x.shape[0]
    tm, tf = TM_FFN, TF_FFN
    return pl.pallas_call(
        _ffn_kernel,
        grid=(m // tm, D_FF // tf),
        in_specs=[
            pl.BlockSpec((tm, D_MODEL), lambda i, j: (i, 0)),
            pl.BlockSpec((1, D_MODEL), lambda i, j: (0, 0)),
            pl.BlockSpec((None, D_MODEL, tf), lambda i, j: (li, 0, j)),
            pl.BlockSpec((None, tf, D_MODEL), lambda i, j: (li, j, 0)),
            pl.BlockSpec((1, D_MODEL), lambda i, j: (0, 0)),
        ],
        out_specs=pl.BlockSpec((tm, D_MODEL), lambda i, j: (i, 0)),
        out_shape=jax.ShapeDtypeStruct((m, D_MODEL), F32),
        scratch_shapes=[pltpu.VMEM((tm, D_MODEL), BF16)],
        compiler_params=pltpu.CompilerParams(dimension_semantics=("parallel", "arbitrary"),
                                             vmem_limit_bytes=VMEM_LIMIT_FFN),
        name="ffn",
    )(x, gpre, w1, w2, gpost)


def kernel(x, g_mix_pre, g_mix_post, w_in, gmlp_w, gmlp_b, diff_lam, diff_subln, conv_w, conv_b,
           conv_norm, w_out, g_ffn_pre, g_ffn_post, w_ff1, w_ff2):
    batch, seq, _ = x.shape
    depth = w_in.shape[0]
    xf = x.reshape(batch * seq, D_MODEL)
    row = lambda v: v.reshape(1, -1)
    w_out_b, w_ff1_b, w_ff2_b = w_out.astype(BF16), w_ff1.astype(BF16), w_ff2.astype(BF16)
    for li in range(depth):
        of, ob = _in_proj(xf, row(g_mix_pre[li]), w_in, li)

        w_t = gmlp_w[li].transpose(1, 0, 2).reshape(GMLP_CHUNK, GMLP_GROUPS * GMLP_CHUNK)
        bias_full = jnp.repeat(gmlp_b[li].T, GROUP_WIDTH // GMLP_GROUPS, axis=1)
        out_a = _gmlp(of, w_t, bias_full)

        lam_init = 0.8 - 0.6 * float(np.exp(-0.3 * li))
        out_b = _diff_attention(ob, diff_lam[li], diff_subln[li].reshape(-1, 1), lam_init,
                                batch, seq)
        out_c = _dil_attention(ob, batch, seq)
        out_d = _conv(of, conv_w[li], row(conv_b[li]), row(conv_norm[li]), batch, seq)

        xf = _out_proj((out_a, out_b, out_c, out_d), w_out_b, xf, row(g_mix_post[li]), li)
        xf = _ffn(xf, row(g_ffn_pre[li]), w_ff1_b, w_ff2_b, row(g_ffn_post[li]), li)
    return xf.reshape(batch, seq, D_MODEL)
```

```python
import functools

import numpy as np
import jax
import jax.numpy as jnp
from jax import lax
from jax.experimental import pallas as pl
from jax.experimental.pallas import tpu as pltpu

F32 = jnp.float32
BF16 = jnp.bfloat16

D_MODEL = 2048
GROUP_WIDTH = 512
N_IN_SPLITS = 10
D_FF = 4 * D_MODEL
NORM_EPS = 1e-6
GMLP_CHUNK = 128
GMLP_GROUPS = 8
DIFF_HEADS = 4
DIL_HEADS = 8
HEAD_DIM = 64
DIL_PATTERNS = ((128, 1), (512, 4), (2048, 16))
DIL_BLOCK = 128
CONV_WIDTH = 31
N_ALIBI_HEADS = DIFF_HEADS + DIL_HEADS
ATTN_SCALE = HEAD_DIM ** -0.5
MASKED = -1e30
LOG2E = 1.4426950408889634
DIFF_ONES_ROWS = 16
DIFF_HEAD_GROUP = 4

LANES = 128
VMEM_LIMIT = 48 * 1024 * 1024
VMEM_LIMIT_FFN = 58 * 1024 * 1024

TM_PROJ = 1024
TM_OUT = 512
TM_FFN = 1024
TF_FFN = 512
FFN_OUT_CHUNK = 512
GMLP_ROWS = 512
DIFF_TQ = 256
DIFF_TK = 256
CONV_ROWS = 256
CONV_HALO = 32
CONV_CHUNK = 32
SUBLANES = 8


def _alibi_slopes():
    i = np.arange(1, N_ALIBI_HEADS + 1, dtype=np.float64)
    s = 2.0 ** (-8.0 * i / N_ALIBI_HEADS)
    diff_idx = np.arange(0, N_ALIBI_HEADS, 3)
    dil_idx = np.array([j for j in range(N_ALIBI_HEADS) if j % 3 != 0])
    return s[diff_idx], s[dil_idx]


def _rms(x, g):
    return x * lax.rsqrt(jnp.mean(x * x, axis=-1, keepdims=True) + NORM_EPS) * g


def _params(sem):
    return pltpu.CompilerParams(dimension_semantics=sem, vmem_limit_bytes=VMEM_LIMIT)


def _proj_kernel(x_ref, g_ref, w_ref, of_ref, ob_ref, h_ref):
    j = pl.program_id(1)

    @pl.when(j == 0)
    def _():
        h_ref[...] = _rms(x_ref[...], g_ref[...]).astype(BF16)

    y = jnp.dot(h_ref[...], w_ref[...].astype(BF16), preferred_element_type=F32)

    @pl.when(j < 4)
    def _():
        of_ref[...] = y

    @pl.when(j >= 4)
    def _():
        ob_ref[...] = y.astype(BF16)


def _in_proj(x, g, w, li):
    m = x.shape[0]
    tm, tn = TM_PROJ, GROUP_WIDTH
    wcol = lambda i, j: (li, 0, jnp.where(j < 2, j, jnp.where(j < 4, j + 6, j - 2)))
    return pl.pallas_call(
        _proj_kernel,
        grid=(m // tm, N_IN_SPLITS),
        in_specs=[
            pl.BlockSpec((tm, D_MODEL), lambda i, j: (i, 0)),
            pl.BlockSpec((1, D_MODEL), lambda i, j: (0, 0)),
            pl.BlockSpec((None, D_MODEL, tn), wcol),
        ],
        out_specs=[
            pl.BlockSpec((tm, tn), lambda i, j: (i, jnp.minimum(j, 3))),
            pl.BlockSpec((tm, tn), lambda i, j: (i, jnp.maximum(j - 4, 0))),
        ],
        out_shape=[
            jax.ShapeDtypeStruct((m, 4 * GROUP_WIDTH), F32),
            jax.ShapeDtypeStruct((m, 6 * GROUP_WIDTH), BF16),
        ],
        scratch_shapes=[pltpu.VMEM((tm, D_MODEL), BF16)],
        compiler_params=_params(("parallel", "arbitrary")),
        name="in_proj",
    )(x, g, w)


def _gelu(x):
    return jax.nn.gelu(x, approximate=True)


def _gmlp_kernel(u_ref, v_ref, w_ref, b_ref, o_ref):
    c = GMLP_CHUNK
    t_idx = lax.broadcasted_iota(jnp.int32, w_ref.shape, 0)
    s_idx = lax.broadcasted_iota(jnp.int32, w_ref.shape, 1) % c
    w = jnp.where(s_idx <= t_idx, w_ref[...], 0.0).astype(BF16)
    bias = b_ref[...]
    lane = lax.broadcasted_iota(jnp.int32, (c, LANES), 1)
    low = lane < HEAD_DIM
    for ci in range(GMLP_ROWS // c):
        rows = slice(ci * c, (ci + 1) * c)
        u = _gelu(u_ref[rows, :])
        v = _gelu(v_ref[rows, :])
        mu = jnp.mean(v, axis=-1, keepdims=True)
        vc = v - mu
        v = vc * lax.rsqrt(jnp.mean(vc * vc, axis=-1, keepdims=True) + NORM_EPS)
        zs = []
        for p in range(GMLP_GROUPS // 2):
            vp = v[:, p * LANES:(p + 1) * LANES]
            rhs = jnp.concatenate([jnp.where(low, vp, 0.0), jnp.where(low, 0.0, vp)], axis=0)
            zs.append(jnp.dot(w[:, p * 2 * c:(p + 1) * 2 * c], rhs.astype(BF16),
                              preferred_element_type=F32))
        z = jnp.concatenate(zs, axis=1) + bias
        o_ref[rows, :] = (u * z).astype(BF16)


def _gmlp(of, w_t, bias_full):
    m = of.shape[0]
    tr = GMLP_ROWS
    return pl.pallas_call(
        _gmlp_kernel,
        grid=(m // tr,),
        in_specs=[
            pl.BlockSpec((tr, GROUP_WIDTH), lambda i: (i, 0)),
            pl.BlockSpec((tr, GROUP_WIDTH), lambda i: (i, 1)),
            pl.BlockSpec((GMLP_CHUNK, GMLP_GROUPS * GMLP_CHUNK), lambda i: (0, 0)),
            pl.BlockSpec((GMLP_CHUNK, GROUP_WIDTH), lambda i: (0, 0)),
        ],
        out_specs=pl.BlockSpec((tr, GROUP_WIDTH), lambda i: (i, 0)),
        out_shape=jax.ShapeDtypeStruct((m, GROUP_WIDTH), BF16),
        compiler_params=_params(("parallel",)),
        name="gmlp",
    )(of, of, w_t, bias_full)


def _stack_queries(q):
    lane = lax.broadcasted_iota(jnp.int32, q.shape, 1)
    low = lane < HEAD_DIM
    zero = jnp.zeros_like(q)
    q2 = jnp.concatenate([jnp.where(low, q, zero), jnp.where(low, zero, q)], axis=0)
    return (q2.astype(F32) * (ATTN_SCALE * LOG2E)).astype(BF16)


def _qk(q2, k):
    return lax.dot_general(q2, k, (((1,), (1,)), ((), ())), preferred_element_type=F32)


def _cast_specs(w, li, nsteps, step_of):
    _, rows, cols = w.shape
    slab = rows // nsteps
    assert slab * nsteps == rows and slab % 16 == 0
    in_spec = pl.BlockSpec((None, slab, cols), lambda *ids: (li, step_of(*ids), 0))
    out_spec = pl.BlockSpec((slab, cols), lambda *ids: (step_of(*ids), 0))
    return in_spec, out_spec, jax.ShapeDtypeStruct((rows, cols), BF16)


def _diff_kernel(q_ref, k_ref, v_ref, bias_ref, lam_ref, g_ref, wf_ref, o_ref, wb_ref,
                 vt_s, acc_s, *, lam_init, slopes):
    wb_ref[...] = wf_ref[...].astype(BF16)
    qi = pl.program_id(1)
    tq, tk = DIFF_TQ, DIFF_TK
    seq = k_ref.shape[0]
    heads = range(DIFF_HEADS)
    hcols = lambda h: slice(h * LANES, (h + 1) * LANES)

    @pl.when(qi == 0)
    def _():
        for h in heads:
            for c in range(seq // tk):
                rows = slice(c * tk, (c + 1) * tk)
                vt_s[h, 0:LANES, rows] = v_ref[rows, hcols(h)].astype(F32).T.astype(BF16)
            vt_s[h, LANES:, :] = jnp.ones((DIFF_ONES_ROWS, seq), BF16)

    q2 = [_stack_queries(q_ref[:, hcols(h)]) for h in heads]
    acc_s[...] = jnp.zeros(acc_s.shape, F32)

    def step_group(hs, start, which, dist, m_prev):
        sts = {h: _qk(k_ref[pl.ds(start, tk), hcols(h)], q2[h]) + bias_ref[h, which]
               for h in hs}
        shifts = {h: (-slopes[h] * LOG2E) * dist for h in hs}
        m_next = {h: jnp.maximum(m_prev[h], jnp.max(sts[h], axis=0, keepdims=True) + shifts[h])
                  for h in hs}
        ps = {h: jnp.exp2(sts[h] - (m_next[h] - shifts[h])).astype(BF16) for h in hs}
        alphas = {h: jnp.exp2(m_prev[h] - m_next[h]) for h in hs}
        pvs = {h: jnp.dot(vt_s[h, :, pl.ds(start, tk)], ps[h], preferred_element_type=F32)
               for h in hs}
        for h in hs:
            acc_s[h] = alphas[h] * acc_s[h] + pvs[h]
        return m_next

    def step(start, which, dist, m_prev):
        m_next = {}
        for g in range(0, DIFF_HEADS, DIFF_HEAD_GROUP):
            m_next.update(step_group(range(g, g + DIFF_HEAD_GROUP), start, which, dist, m_prev))
        return tuple(m_next[h] for h in heads)

    def body(j, m_prev):
        return step(pl.multiple_of(j * tk, tk), 0, ((qi - j) * tq).astype(F32), m_prev)

    m0 = jnp.full((1, 2 * tq), MASKED, F32)
    m_prev = lax.fori_loop(0, qi, body, tuple(m0 for _ in heads))
    step(pl.multiple_of(qi * tq, tq), 1, 0.0, m_prev)

    lp = lam_ref[...]
    lam = (jnp.exp(jnp.sum(lp[0:1] * lp[1:2], axis=1, keepdims=True))
           - jnp.exp(jnp.sum(lp[2:3] * lp[3:4], axis=1, keepdims=True)) + lam_init)
    for h in heads:
        o = acc_s[h, 0:LANES, :] / acc_s[h, LANES:LANES + 1, :]
        o = o[:, :tq] - lam * o[:, tq:]
        y = o * lax.rsqrt(jnp.mean(o * o, axis=0, keepdims=True) + NORM_EPS) * g_ref[...]
        o_ref[:, hcols(h)] = (y * (1.0 - lam_init)).T.astype(BF16)


def _diff_bias_tiles(slopes):
    tq, tk = DIFF_TQ, DIFF_TK
    rq = (np.arange(2 * tq) % tq)[None, :].astype(np.float64)
    c = np.arange(tk)[:, None].astype(np.float64)
    tiles = np.zeros((DIFF_HEADS, 2, tk, 2 * tq), np.float64)
    for h, sl in enumerate(slopes):
        tiles[h, 0] = -sl * LOG2E * (rq - c)
        tiles[h, 1] = np.where(rq - c >= 0, -sl * LOG2E * (rq - c), MASKED)
    return tiles.astype(np.float32)


def _diff_attention(ob, lam_p, subln_g, lam_init, batch, seq, w_cast, li):
    m = ob.shape[0]
    tq, tk = DIFF_TQ, DIFF_TK
    assert tq == tk and seq % tq == 0
    nq = seq // tq
    slopes, _ = _alibi_slopes()
    bias = jnp.asarray(_diff_bias_tiles(slopes))
    kern = functools.partial(_diff_kernel, lam_init=lam_init,
                             slopes=tuple(float(s) for s in slopes))
    w_in_spec, w_out_spec, w_shape = _cast_specs(w_cast, li, batch * nq, lambda b, i: b * nq + i)
    return pl.pallas_call(
        kern,
        grid=(batch, nq),
        in_specs=[
            pl.BlockSpec((tq, GROUP_WIDTH), lambda b, i: (b * nq + i, 0)),
            pl.BlockSpec((seq, GROUP_WIDTH), lambda b, i: (b, 1)),
            pl.BlockSpec((seq, GROUP_WIDTH), lambda b, i: (b, 2)),
            pl.BlockSpec((DIFF_HEADS, 2, tk, 2 * tq), lambda b, i: (0, 0, 0, 0)),
            pl.BlockSpec((4, HEAD_DIM), lambda b, i: (0, 0)),
            pl.BlockSpec((2 * HEAD_DIM, 1), lambda b, i: (0, 0)),
            w_in_spec,
        ],
        out_specs=[pl.BlockSpec((tq, GROUP_WIDTH), lambda b, i: (b * nq + i, 0)), w_out_spec],
        out_shape=[jax.ShapeDtypeStruct((m, GROUP_WIDTH), BF16), w_shape],
        scratch_shapes=[
            pltpu.VMEM((DIFF_HEADS, LANES + DIFF_ONES_ROWS, seq), BF16),
            pltpu.VMEM((DIFF_HEADS, LANES + DIFF_ONES_ROWS, 2 * tq), F32),
        ],
        compiler_params=_params(("parallel", "arbitrary")),
        name="diff_attn",
    )(ob, ob, ob, bias, lam_p, subln_g, w_cast)


def _dil_kernel(q_ref, k_ref, v_ref, bias_ref, wf_ref, o_ref, wb_ref, qf, kf, vf, m_s, l_s, o_s,
                *, seq):
    wb_ref[...] = wf_ref[...].astype(BF16)
    n = DIL_BLOCK
    qf[...] = q_ref[...].astype(F32)
    kf[...] = k_ref[...].astype(F32)
    vf[...] = v_ref[...].astype(F32)
    lane = lax.broadcasted_iota(jnp.int32, (n, LANES), 1)
    low = lane < HEAD_DIM

    for pi, (window, dil) in enumerate(DIL_PATTERNS):
        nblk = seq // (n * dil)
        for r in range(dil):
            for c in range(nblk):
                cur = pl.ds(r + dil * n * c, n, stride=dil)
                q2 = _stack_queries(qf[cur, :])
                if c > 0:
                    prev = pl.ds(r + dil * n * (c - 1), n, stride=dil)
                    kb = jnp.concatenate([kf[prev, :], kf[cur, :]], axis=0).astype(BF16)
                    vb = jnp.concatenate([vf[prev, :], vf[cur, :]], axis=0).astype(BF16)
                    bias = bias_ref[pi]
                else:
                    kb = kf[cur, :].astype(BF16)
                    vb = vf[cur, :].astype(BF16)
                    bias = bias_ref[pi, :, n:]
                s = _qk(q2, kb) + bias
                mx = jnp.max(s, axis=1, keepdims=True)
                p = jnp.exp2(s - mx).astype(BF16)
                vb1 = jnp.concatenate([vb, jnp.ones(vb.shape, BF16)], axis=1)
                ol = jnp.dot(p, vb1, preferred_element_type=F32)
                o, l = ol[:, :LANES], ol[:, LANES:]
                o_s[pi, cur, :] = jnp.where(low, o[:n], o[n:])
                m_s[pi, cur, :] = jnp.where(low, mx[:n], mx[n:])
                l_s[pi, cur, :] = jnp.where(low, l[:n], l[n:])

    m_all = jnp.maximum(jnp.maximum(m_s[0], m_s[1]), m_s[2])
    num = jnp.zeros((seq, LANES), F32)
    den = jnp.zeros((seq, LANES), F32)
    for pi in range(len(DIL_PATTERNS)):
        w = jnp.exp2(m_s[pi] - m_all)
        num = num + w * o_s[pi]
        den = den + w * l_s[pi]
    o_ref[...] = (num / den).astype(BF16)


def _dil_bias_tiles(slopes):
    n = DIL_BLOCK
    qi = (np.arange(2 * n) % n)[:, None]
    ki = np.arange(2 * n)[None, :]
    step = n + qi - ki
    valid = (step >= 0) & (step <= n)
    tiles = np.zeros((len(DIL_PATTERNS), DIL_HEADS // 2, 2 * n, 2 * n), np.float64)
    for pi, (_, dil) in enumerate(DIL_PATTERNS):
        for p in range(DIL_HEADS // 2):
            sl = np.where(np.arange(2 * n) < n, slopes[2 * p], slopes[2 * p + 1])[:, None]
            tiles[pi, p] = np.where(valid, -sl * LOG2E * (step * dil), MASKED)
    return tiles.astype(np.float32)


def _dil_attention(ob, batch, seq, w_cast, li):
    m = ob.shape[0]
    _, slopes = _alibi_slopes()
    bias = jnp.asarray(_dil_bias_tiles(slopes))
    npairs = DIL_HEADS // 2
    npat = len(DIL_PATTERNS)
    col0 = 3 * GROUP_WIDTH // LANES
    kern = functools.partial(_dil_kernel, seq=seq)
    w_in_spec, w_out_spec, w_shape = _cast_specs(w_cast, li, batch * npairs,
                                                 lambda b, p: b * npairs + p)
    return pl.pallas_call(
        kern,
        grid=(batch, npairs),
        in_specs=[
            pl.BlockSpec((seq, LANES), lambda b, p: (b, col0 + p)),
            pl.BlockSpec((seq, LANES), lambda b, p: (b, col0 + npairs + p)),
            pl.BlockSpec((seq, LANES), lambda b, p: (b, col0 + 2 * npairs + p)),
            pl.BlockSpec((npat, None, 2 * DIL_BLOCK, 2 * DIL_BLOCK), lambda b, p: (0, p, 0, 0)),
            w_in_spec,
        ],
        out_specs=[pl.BlockSpec((seq, LANES), lambda b, p: (b, p)), w_out_spec],
        out_shape=[jax.ShapeDtypeStruct((m, GROUP_WIDTH), BF16), w_shape],
        scratch_shapes=[
            pltpu.VMEM((seq, LANES), F32),
            pltpu.VMEM((seq, LANES), F32),
            pltpu.VMEM((seq, LANES), F32),
            pltpu.VMEM((npat, seq, LANES), F32),
            pltpu.VMEM((npat, seq, LANES), F32),
            pltpu.VMEM((npat, seq, LANES), F32),
        ],
        compiler_params=_params(("parallel", "parallel")),
        name="dil_attn",
    )(ob, ob, ob, bias, w_cast)


def _conv_kernel(a_ref, g_ref, ap_ref, gp_ref, w_ref, b_ref, n_ref, wf_ref, o_ref, wb_ref, h_s):
    wb_ref[...] = wf_ref[...].astype(BF16)
    i = pl.program_id(1)
    tr, halo = CONV_ROWS, CONV_HALO
    hp = ap_ref[...] * jax.nn.sigmoid(gp_ref[...])
    h_s[0:halo, :] = jnp.where(i > 0, hp, 0.0)
    h_s[halo:, :] = a_ref[...] * jax.nn.sigmoid(g_ref[...])
    rc, sub, last = CONV_CHUNK, SUBLANES, CONV_WIDTH - 1

    def chunk(c, carry):
        base = pl.multiple_of(c * rc, rc)
        y = jnp.zeros((rc, GROUP_WIDTH), F32)
        for r in range(sub):
            a_r = None
            for q in range((last - r) // sub + 1):
                d = sub * q + r
                win = h_s[pl.ds(base + (halo - sub - sub * q), rc + sub), :]
                term = w_ref[last - d] * win.reshape(rc // sub + 1, sub, GROUP_WIDTH)
                a_r = term if a_r is None else a_r + term
            y = y + a_r.reshape(rc + sub, GROUP_WIDTH)[sub - r:sub - r + rc, :]
        y = _rms(y + b_ref[...], n_ref[...])
        o_ref[pl.ds(base, rc), :] = (y * jax.nn.sigmoid(y)).astype(BF16)
        return carry

    lax.fori_loop(0, tr // rc, chunk, 0)


def _conv(of, w, b, g, batch, seq, w_cast, li):
    m = of.shape[0]
    tr, halo = CONV_ROWS, CONV_HALO
    nt = seq // tr
    per = tr // halo
    cur = lambda col: (lambda bb, i: (bb * nt + i, col))
    prev = lambda col: (lambda bb, i: (jnp.maximum((bb * nt + i) * per - 1, 0), col))
    w_in_spec, w_out_spec, w_shape = _cast_specs(w_cast, li, batch * nt,
                                                 lambda bb, i: bb * nt + i)
    return pl.pallas_call(
        _conv_kernel,
        grid=(batch, nt),
        in_specs=[
            pl.BlockSpec((tr, GROUP_WIDTH), cur(2)),
            pl.BlockSpec((tr, GROUP_WIDTH), cur(3)),
            pl.BlockSpec((halo, GROUP_WIDTH), prev(2)),
            pl.BlockSpec((halo, GROUP_WIDTH), prev(3)),
            pl.BlockSpec((CONV_WIDTH, SUBLANES, GROUP_WIDTH), lambda bb, i: (0, 0, 0)),
            pl.BlockSpec((1, GROUP_WIDTH), lambda bb, i: (0, 0)),
            pl.BlockSpec((1, GROUP_WIDTH), lambda bb, i: (0, 0)),
            w_in_spec,
        ],
        out_specs=[pl.BlockSpec((tr, GROUP_WIDTH), lambda bb, i: (bb * nt + i, 0)), w_out_spec],
        out_shape=[jax.ShapeDtypeStruct((m, GROUP_WIDTH), BF16), w_shape],
        scratch_shapes=[pltpu.VMEM((halo + tr, GROUP_WIDTH), F32)],
        compiler_params=_params(("parallel", "parallel")),
        name="conformer_conv",
    )(of, of, of, of, jnp.broadcast_to(w[:, None, :], (CONV_WIDTH, SUBLANES, GROUP_WIDTH)), b, g,
      w_cast)


def _out_kernel(a_ref, b_ref, c_ref, d_ref, w_ref, x_ref, g_ref, o_ref):
    gw = GROUP_WIDTH
    y = jnp.dot(a_ref[...], w_ref[0:gw, :], preferred_element_type=F32)
    y = y + jnp.dot(b_ref[...], w_ref[gw:2 * gw, :], preferred_element_type=F32)
    y = y + jnp.dot(c_ref[...], w_ref[2 * gw:3 * gw, :], preferred_element_type=F32)
    y = y + jnp.dot(d_ref[...], w_ref[3 * gw:4 * gw, :], preferred_element_type=F32)
    o_ref[...] = x_ref[...] + _rms(y, g_ref[...])


def _out_proj(mix, w, x, g):
    m = x.shape[0]
    tm = TM_OUT
    mix_spec = pl.BlockSpec((tm, GROUP_WIDTH), lambda i: (i, 0))
    return pl.pallas_call(
        _out_kernel,
        grid=(m // tm,),
        in_specs=[mix_spec, mix_spec, mix_spec, mix_spec,
                  pl.BlockSpec((D_MODEL, D_MODEL), lambda i: (0, 0)),
                  pl.BlockSpec((tm, D_MODEL), lambda i: (i, 0)),
                  pl.BlockSpec((1, D_MODEL), lambda i: (0, 0))],
        out_specs=pl.BlockSpec((tm, D_MODEL), lambda i: (i, 0)),
        out_shape=jax.ShapeDtypeStruct((m, D_MODEL), F32),
        compiler_params=_params(("parallel",)),
        name="out_proj",
    )(*mix, w, x, g)


def _ffn_kernel(x_ref, gpre_ref, w1_ref, w2_ref, gpost_ref, o_ref, h_s):
    j = pl.program_id(1)

    @pl.when(j == 0)
    def _():
        h_s[...] = _rms(x_ref[...], gpre_ref[...]).astype(BF16)
        o_ref[...] = jnp.zeros(o_ref.shape, F32)

    f = jnp.dot(h_s[...], w1_ref[...], preferred_element_type=F32)
    f = jnp.square(jnp.maximum(f, 0.0)).astype(BF16)
    for c in range(D_MODEL // FFN_OUT_CHUNK):
        cols = slice(c * FFN_OUT_CHUNK, (c + 1) * FFN_OUT_CHUNK)
        o_ref[:, cols] += jnp.dot(f, w2_ref[:, cols], preferred_element_type=F32)

    @pl.when(j == pl.num_programs(1) - 1)
    def _():
        o_ref[...] = x_ref[...] + _rms(o_ref[...], gpost_ref[...])


def _ffn(x, gpre, w1, w2, gpost):
    m = x.shape[0]
    tm, tf = TM_FFN, TF_FFN
    return pl.pallas_call(
        _ffn_kernel,
        grid=(m // tm, D_FF // tf),
        in_specs=[
            pl.BlockSpec((tm, D_MODEL), lambda i, j: (i, 0)),
            pl.BlockSpec((1, D_MODEL), lambda i, j: (0, 0)),
            pl.BlockSpec((D_MODEL, tf), lambda i, j: (0, j)),
            pl.BlockSpec((tf, D_MODEL), lambda i, j: (j, 0)),
            pl.BlockSpec((1, D_MODEL), lambda i, j: (0, 0)),
        ],
        out_specs=pl.BlockSpec((tm, D_MODEL), lambda i, j: (i, 0)),
        out_shape=jax.ShapeDtypeStruct((m, D_MODEL), F32),
        scratch_shapes=[pltpu.VMEM((tm, D_MODEL), BF16)],
        compiler_params=pltpu.CompilerParams(dimension_semantics=("parallel", "arbitrary"),
                                             vmem_limit_bytes=VMEM_LIMIT_FFN),
        name="ffn",
    )(x, gpre, w1, w2, gpost)


def kernel(x, g_mix_pre, g_mix_post, w_in, gmlp_w, gmlp_b, diff_lam, diff_subln, conv_w, conv_b,
           conv_norm, w_out, g_ffn_pre, g_ffn_post, w_ff1, w_ff2):
    batch, seq, _ = x.shape
    depth = w_in.shape[0]
    xf = x.reshape(batch * seq, D_MODEL)
    row = lambda v: v.reshape(1, -1)
    w_in_b = w_in.astype(BF16)
    for li in range(depth):
        of, ob = _in_proj(xf, row(g_mix_pre[li]), w_in_b, li)

        w_t = gmlp_w[li].transpose(1, 0, 2).reshape(GMLP_CHUNK, GMLP_GROUPS * GMLP_CHUNK)
        bias_full = jnp.repeat(gmlp_b[li].T, GROUP_WIDTH // GMLP_GROUPS, axis=1)
        out_a = _gmlp(of, w_t, bias_full)

        lam_init = 0.8 - 0.6 * float(np.exp(-0.3 * li))
        out_b, w_ff1_b = _diff_attention(ob, diff_lam[li], diff_subln[li].reshape(-1, 1),
                                         lam_init, batch, seq, w_ff1, li)
        out_c, w_ff2_b = _dil_attention(ob, batch, seq, w_ff2, li)
        out_d, w_out_b = _conv(of, conv_w[li], row(conv_b[li]), row(conv_norm[li]), batch, seq,
                               w_out, li)

        xf = _out_proj((out_a, out_b, out_c, out_d), w_out_b, xf, row(g_mix_post[li]))
        xf = _ffn(xf, row(g_ffn_pre[li]), w_ff1_b, w_ff2_b, row(g_ffn_post[li]))
    return xf.reshape(batch, seq, D_MODEL)
```

```python
import functools

import numpy as np
import jax
import jax.numpy as jnp
from jax import lax
from jax.experimental import pallas as pl
from jax.experimental.pallas import tpu as pltpu

F32 = jnp.float32
BF16 = jnp.bfloat16

D_MODEL = 2048
GROUP_WIDTH = 512
N_IN_SPLITS = 10
D_FF = 4 * D_MODEL
NORM_EPS = 1e-6
GMLP_CHUNK = 128
GMLP_GROUPS = 8
DIFF_HEADS = 4
DIL_HEADS = 8
HEAD_DIM = 64
DIL_PATTERNS = ((128, 1), (512, 4), (2048, 16))
DIL_BLOCK = 128
CONV_WIDTH = 31
N_ALIBI_HEADS = DIFF_HEADS + DIL_HEADS
ATTN_SCALE = HEAD_DIM ** -0.5
MASKED = -1e30
LOG2E = 1.4426950408889634
DIFF_ONES_ROWS = 16
DIFF_HEAD_GROUP = 4

LANES = 128
VMEM_LIMIT = 48 * 1024 * 1024
VMEM_LIMIT_FFN = 58 * 1024 * 1024

TM_PROJ = 1024
TM_OUT = 512
TM_FFN = 1024
TF_FFN = 512
FFN_OUT_CHUNK = 512
GMLP_ROWS = 512
DIFF_TQ = 256
DIFF_TK = 256
CONV_ROWS = 256
CONV_HALO = 32
CONV_CHUNK = 128
SUBLANES = 8


def _alibi_slopes():
    i = np.arange(1, N_ALIBI_HEADS + 1, dtype=np.float64)
    s = 2.0 ** (-8.0 * i / N_ALIBI_HEADS)
    diff_idx = np.arange(0, N_ALIBI_HEADS, 3)
    dil_idx = np.array([j for j in range(N_ALIBI_HEADS) if j % 3 != 0])
    return s[diff_idx], s[dil_idx]


def _rms(x, g):
    return x * lax.rsqrt(jnp.mean(x * x, axis=-1, keepdims=True) + NORM_EPS) * g


def _params(sem):
    return pltpu.CompilerParams(dimension_semantics=sem, vmem_limit_bytes=VMEM_LIMIT)


def _proj_kernel(x_ref, g_ref, w_ref, of_ref, ob_ref, h_ref):
    j = pl.program_id(1)

    @pl.when(j == 0)
    def _():
        h_ref[...] = _rms(x_ref[...], g_ref[...]).astype(BF16)

    y = jnp.dot(h_ref[...], w_ref[...].astype(BF16), preferred_element_type=F32)

    @pl.when(j < 4)
    def _():
        of_ref[...] = y

    @pl.when(j >= 4)
    def _():
        ob_ref[...] = y.astype(BF16)


def _in_proj(x, g, w, li):
    m = x.shape[0]
    tm, tn = TM_PROJ, GROUP_WIDTH
    col = lambda j: jnp.where(j < 2, j, jnp.where(j < 4, j + 6, j - 2))
    if w.ndim == 3:
        w_spec = pl.BlockSpec((None, D_MODEL, tn), lambda i, j: (li, 0, col(j)))
    else:
        w_spec = pl.BlockSpec((D_MODEL, tn), lambda i, j: (0, col(j)))
    return pl.pallas_call(
        _proj_kernel,
        grid=(m // tm, N_IN_SPLITS),
        in_specs=[
            pl.BlockSpec((tm, D_MODEL), lambda i, j: (i, 0)),
            pl.BlockSpec((1, D_MODEL), lambda i, j: (0, 0)),
            w_spec,
        ],
        out_specs=[
            pl.BlockSpec((tm, tn), lambda i, j: (i, jnp.minimum(j, 3))),
            pl.BlockSpec((tm, tn), lambda i, j: (i, jnp.maximum(j - 4, 0))),
        ],
        out_shape=[
            jax.ShapeDtypeStruct((m, 4 * GROUP_WIDTH), F32),
            jax.ShapeDtypeStruct((m, 6 * GROUP_WIDTH), BF16),
        ],
        scratch_shapes=[pltpu.VMEM((tm, D_MODEL), BF16)],
        compiler_params=_params(("parallel", "arbitrary")),
        name="in_proj",
    )(x, g, w)


def _gelu(x):
    return jax.nn.gelu(x, approximate=True)


def _gmlp_kernel(u_ref, v_ref, w_ref, b_ref, o_ref):
    c = GMLP_CHUNK
    t_idx = lax.broadcasted_iota(jnp.int32, w_ref.shape, 0)
    s_idx = lax.broadcasted_iota(jnp.int32, w_ref.shape, 1) % c
    w = jnp.where(s_idx <= t_idx, w_ref[...], 0.0).astype(BF16)
    bias = b_ref[...]
    lane = lax.broadcasted_iota(jnp.int32, (c, LANES), 1)
    low = lane < HEAD_DIM
    for ci in range(GMLP_ROWS // c):
        rows = slice(ci * c, (ci + 1) * c)
        u = _gelu(u_ref[rows, :])
        v = _gelu(v_ref[rows, :])
        mu = jnp.mean(v, axis=-1, keepdims=True)
        vc = v - mu
        v = vc * lax.rsqrt(jnp.mean(vc * vc, axis=-1, keepdims=True) + NORM_EPS)
        zs = []
        for p in range(GMLP_GROUPS // 2):
            vp = v[:, p * LANES:(p + 1) * LANES]
            rhs = jnp.concatenate([jnp.where(low, vp, 0.0), jnp.where(low, 0.0, vp)], axis=0)
            zs.append(jnp.dot(w[:, p * 2 * c:(p + 1) * 2 * c], rhs.astype(BF16),
                              preferred_element_type=F32))
        z = jnp.concatenate(zs, axis=1) + bias
        o_ref[rows, :] = (u * z).astype(BF16)


def _gmlp(of, w_t, bias_full):
    m = of.shape[0]
    tr = GMLP_ROWS
    return pl.pallas_call(
        _gmlp_kernel,
        grid=(m // tr,),
        in_specs=[
            pl.BlockSpec((tr, GROUP_WIDTH), lambda i: (i, 0)),
            pl.BlockSpec((tr, GROUP_WIDTH), lambda i: (i, 1)),
            pl.BlockSpec((GMLP_CHUNK, GMLP_GROUPS * GMLP_CHUNK), lambda i: (0, 0)),
            pl.BlockSpec((GMLP_CHUNK, GROUP_WIDTH), lambda i: (0, 0)),
        ],
        out_specs=pl.BlockSpec((tr, GROUP_WIDTH), lambda i: (i, 0)),
        out_shape=jax.ShapeDtypeStruct((m, GROUP_WIDTH), BF16),
        compiler_params=_params(("parallel",)),
        name="gmlp",
    )(of, of, w_t, bias_full)


def _stack_queries(q):
    lane = lax.broadcasted_iota(jnp.int32, q.shape, 1)
    low = lane < HEAD_DIM
    zero = jnp.zeros_like(q)
    q2 = jnp.concatenate([jnp.where(low, q, zero), jnp.where(low, zero, q)], axis=0)
    return (q2.astype(F32) * (ATTN_SCALE * LOG2E)).astype(BF16)


def _qk(q2, k):
    return lax.dot_general(q2, k, (((1,), (1,)), ((), ())), preferred_element_type=F32)


def _cast_specs(w, li, nsteps, step_of):
    _, rows, cols = w.shape
    slab = rows // nsteps
    assert slab * nsteps == rows and slab % 16 == 0
    in_spec = pl.BlockSpec((None, slab, cols), lambda *ids: (li, step_of(*ids), 0))
    out_spec = pl.BlockSpec((slab, cols), lambda *ids: (step_of(*ids), 0))
    return in_spec, out_spec, jax.ShapeDtypeStruct((rows, cols), BF16)


def _diff_kernel(q_ref, k_ref, v_ref, bias_ref, lam_ref, g_ref, wf_ref, o_ref, wb_ref,
                 vt_s, acc_s, *, lam_init, slopes):
    wb_ref[...] = wf_ref[...].astype(BF16)
    qi = pl.program_id(1)
    tq, tk = DIFF_TQ, DIFF_TK
    seq = k_ref.shape[0]
    heads = range(DIFF_HEADS)
    hcols = lambda h: slice(h * LANES, (h + 1) * LANES)

    @pl.when(qi == 0)
    def _():
        for h in heads:
            for c in range(seq // tk):
                rows = slice(c * tk, (c + 1) * tk)
                vt_s[h, 0:LANES, rows] = v_ref[rows, hcols(h)].astype(F32).T.astype(BF16)
            vt_s[h, LANES:, :] = jnp.ones((DIFF_ONES_ROWS, seq), BF16)

    q2 = [_stack_queries(q_ref[:, hcols(h)]) for h in heads]
    acc_s[...] = jnp.zeros(acc_s.shape, F32)

    def step_group(hs, start, which, dist, m_prev):
        if which == 0:
            bias = {h: jnp.concatenate([bias_ref[h, 0, :, 0:LANES]] * (2 * tq // LANES), axis=1)
                    for h in hs}
        else:
            bias = {h: bias_ref[h, 1] for h in hs}
        sts = {h: _qk(k_ref[pl.ds(start, tk), hcols(h)], q2[h]) + bias[h]
               for h in hs}
        shifts = {h: (-slopes[h] * LOG2E) * dist for h in hs}
        m_next = {h: jnp.maximum(m_prev[h], jnp.max(sts[h], axis=0, keepdims=True) + shifts[h])
                  for h in hs}
        ps = {h: jnp.exp2(sts[h] - (m_next[h] - shifts[h])).astype(BF16) for h in hs}
        alphas = {h: jnp.exp2(m_prev[h] - m_next[h]) for h in hs}
        pvs = {h: jnp.dot(vt_s[h, :, pl.ds(start, tk)], ps[h], preferred_element_type=F32)
               for h in hs}
        for h in hs:
            acc_s[h] = alphas[h] * acc_s[h] + pvs[h]
        return m_next

    def step(start, which, dist, m_prev):
        m_next = {}
        for g in range(0, DIFF_HEADS, DIFF_HEAD_GROUP):
            m_next.update(step_group(range(g, g + DIFF_HEAD_GROUP), start, which, dist, m_prev))
        return tuple(m_next[h] for h in heads)

    def body(j, m_prev):
        return step(pl.multiple_of(j * tk, tk), 0, ((qi - j) * tq).astype(F32), m_prev)

    m0 = jnp.full((1, 2 * tq), MASKED, F32)
    m_prev = lax.fori_loop(0, qi, body, tuple(m0 for _ in heads))
    step(pl.multiple_of(qi * tq, tq), 1, 0.0, m_prev)

    lp = lam_ref[...]
    lam = (jnp.exp(jnp.sum(lp[0:1] * lp[1:2], axis=1, keepdims=True))
           - jnp.exp(jnp.sum(lp[2:3] * lp[3:4], axis=1, keepdims=True)) + lam_init)
    for h in heads:
        o = acc_s[h, 0:LANES, :] / acc_s[h, LANES:LANES + 1, :]
        o = o[:, :tq] - lam * o[:, tq:]
        y = o * lax.rsqrt(jnp.mean(o * o, axis=0, keepdims=True) + NORM_EPS) * g_ref[...]
        o_ref[:, hcols(h)] = (y * (1.0 - lam_init)).T.astype(BF16)


def _diff_bias_tiles(slopes):
    tq, tk = DIFF_TQ, DIFF_TK
    rq = (np.arange(2 * tq) % tq)[None, :].astype(np.float64)
    c = np.arange(tk)[:, None].astype(np.float64)
    tiles = np.zeros((DIFF_HEADS, 2, tk, 2 * tq), np.float64)
    for h, sl in enumerate(slopes):
        tiles[h, 0] = sl * LOG2E * c + 0.0 * rq
        tiles[h, 1] = np.where(rq - c >= 0, sl * LOG2E * c, MASKED)
    return tiles.astype(np.float32)


def _diff_attention(ob, lam_p, subln_g, lam_init, batch, seq, w_cast, li):
    m = ob.shape[0]
    tq, tk = DIFF_TQ, DIFF_TK
    assert tq == tk and seq % tq == 0
    nq = seq // tq
    slopes, _ = _alibi_slopes()
    bias = jnp.asarray(_diff_bias_tiles(slopes))
    kern = functools.partial(_diff_kernel, lam_init=lam_init,
                             slopes=tuple(float(s) for s in slopes))
    w_in_spec, w_out_spec, w_shape = _cast_specs(w_cast, li, batch * nq, lambda b, i: b * nq + i)
    return pl.pallas_call(
        kern,
        grid=(batch, nq),
        in_specs=[
            pl.BlockSpec((tq, GROUP_WIDTH), lambda b, i: (b * nq + i, 0)),
            pl.BlockSpec((seq, GROUP_WIDTH), lambda b, i: (b, 1)),
            pl.BlockSpec((seq, GROUP_WIDTH), lambda b, i: (b, 2)),
            pl.BlockSpec((DIFF_HEADS, 2, tk, 2 * tq), lambda b, i: (0, 0, 0, 0)),
            pl.BlockSpec((4, HEAD_DIM), lambda b, i: (0, 0)),
            pl.BlockSpec((2 * HEAD_DIM, 1), lambda b, i: (0, 0)),
            w_in_spec,
        ],
        out_specs=[pl.BlockSpec((tq, GROUP_WIDTH), lambda b, i: (b * nq + i, 0)), w_out_spec],
        out_shape=[jax.ShapeDtypeStruct((m, GROUP_WIDTH), BF16), w_shape],
        scratch_shapes=[
            pltpu.VMEM((DIFF_HEADS, LANES + DIFF_ONES_ROWS, seq), BF16),
            pltpu.VMEM((DIFF_HEADS, LANES + DIFF_ONES_ROWS, 2 * tq), F32),
        ],
        compiler_params=_params(("parallel", "arbitrary")),
        name="diff_attn",
    )(ob, ob, ob, bias, lam_p, subln_g, w_cast)


def _dil_kernel(q_ref, k_ref, v_ref, bias_ref, wf_ref, o_ref, wb_ref, qf, kf, vf, m_s, l_s, o_s,
                *, seq):
    wb_ref[...] = wf_ref[...].astype(BF16)
    n = DIL_BLOCK
    qf[...] = q_ref[...].astype(F32)
    kf[...] = k_ref[...].astype(F32)
    vf[...] = v_ref[...].astype(F32)
    lane = lax.broadcasted_iota(jnp.int32, (n, LANES), 1)
    low = lane < HEAD_DIM

    for pi, (window, dil) in enumerate(DIL_PATTERNS):
        nblk = seq // (n * dil)
        for r in range(dil):
            for c in range(nblk):
                cur = pl.ds(r + dil * n * c, n, stride=dil)
                q2 = _stack_queries(qf[cur, :])
                if c > 0:
                    prev = pl.ds(r + dil * n * (c - 1), n, stride=dil)
                    kb = jnp.concatenate([kf[prev, :], kf[cur, :]], axis=0).astype(BF16)
                    vb = jnp.concatenate([vf[prev, :], vf[cur, :]], axis=0).astype(BF16)
                    bias = bias_ref[pi]
                else:
                    kb = kf[cur, :].astype(BF16)
                    vb = vf[cur, :].astype(BF16)
                    bias = bias_ref[pi, :, n:]
                s = _qk(q2, kb) + bias
                mx = jnp.max(s, axis=1, keepdims=True)
                p = jnp.exp2(s - mx).astype(BF16)
                vb1 = jnp.concatenate([vb, jnp.ones(vb.shape, BF16)], axis=1)
                ol = jnp.dot(p, vb1, preferred_element_type=F32)
                o, l = ol[:, :LANES], ol[:, LANES:]
                o_s[pi, cur, :] = jnp.where(low, o[:n], o[n:])
                m_s[pi, cur, :] = jnp.where(low, mx[:n], mx[n:])
                l_s[pi, cur, :] = jnp.where(low, l[:n], l[n:])

    m_all = jnp.maximum(jnp.maximum(m_s[0], m_s[1]), m_s[2])
    num = jnp.zeros((seq, LANES), F32)
    den = jnp.zeros((seq, LANES), F32)
    for pi in range(len(DIL_PATTERNS)):
        w = jnp.exp2(m_s[pi] - m_all)
        num = num + w * o_s[pi]
        den = den + w * l_s[pi]
    o_ref[...] = (num / den).astype(BF16)


def _dil_bias_tiles(slopes):
    n = DIL_BLOCK
    qi = (np.arange(2 * n) % n)[:, None]
    ki = np.arange(2 * n)[None, :]
    step = n + qi - ki
    valid = (step >= 0) & (step <= n)
    tiles = np.zeros((len(DIL_PATTERNS), DIL_HEADS // 2, 2 * n, 2 * n), np.float64)
    for pi, (_, dil) in enumerate(DIL_PATTERNS):
        for p in range(DIL_HEADS // 2):
            sl = np.where(np.arange(2 * n) < n, slopes[2 * p], slopes[2 * p + 1])[:, None]
            tiles[pi, p] = np.where(valid, -sl * LOG2E * (step * dil), MASKED)
    return tiles.astype(np.float32)


def _dil_attention(ob, batch, seq, w_cast, li):
    m = ob.shape[0]
    _, slopes = _alibi_slopes()
    bias = jnp.asarray(_dil_bias_tiles(slopes))
    npairs = DIL_HEADS // 2
    npat = len(DIL_PATTERNS)
    col0 = 3 * GROUP_WIDTH // LANES
    kern = functools.partial(_dil_kernel, seq=seq)
    w_in_spec, w_out_spec, w_shape = _cast_specs(w_cast, li, batch * npairs,
                                                 lambda b, p: b * npairs + p)
    return pl.pallas_call(
        kern,
        grid=(batch, npairs),
        in_specs=[
            pl.BlockSpec((seq, LANES), lambda b, p: (b, col0 + p)),
            pl.BlockSpec((seq, LANES), lambda b, p: (b, col0 + npairs + p)),
            pl.BlockSpec((seq, LANES), lambda b, p: (b, col0 + 2 * npairs + p)),
            pl.BlockSpec((npat, None, 2 * DIL_BLOCK, 2 * DIL_BLOCK), lambda b, p: (0, p, 0, 0)),
            w_in_spec,
        ],
        out_specs=[pl.BlockSpec((seq, LANES), lambda b, p: (b, p)), w_out_spec],
        out_shape=[jax.ShapeDtypeStruct((m, GROUP_WIDTH), BF16), w_shape],
        scratch_shapes=[
            pltpu.VMEM((seq, LANES), F32),
            pltpu.VMEM((seq, LANES), F32),
            pltpu.VMEM((seq, LANES), F32),
            pltpu.VMEM((npat, seq, LANES), F32),
            pltpu.VMEM((npat, seq, LANES), F32),
            pltpu.VMEM((npat, seq, LANES), F32),
        ],
        compiler_params=_params(("parallel", "parallel")),
        name="dil_attn",
    )(ob, ob, ob, bias, w_cast)


def _conv_kernel(a_ref, g_ref, ap_ref, gp_ref, w_ref, b_ref, n_ref, *rest, ncast):
    wf_refs, o_ref, wb_refs = rest[:ncast], rest[ncast], rest[ncast + 1:2 * ncast + 1]
    h_s, y_s = rest[2 * ncast + 1:]
    for wf_ref, wb_ref in zip(wf_refs, wb_refs):
        wb_ref[...] = wf_ref[...].astype(BF16)
    i = pl.program_id(1)
    tr, halo = CONV_ROWS, CONV_HALO
    hp = ap_ref[...] * jax.nn.sigmoid(gp_ref[...])
    h_s[0:halo, :] = jnp.where(i > 0, hp, 0.0)
    h_s[halo:, :] = a_ref[...] * jax.nn.sigmoid(g_ref[...])
    rc, sub, last = CONV_CHUNK, SUBLANES, CONV_WIDTH - 1

    for lt in range(GROUP_WIDTH // LANES):
        lanes = slice(lt * LANES, (lt + 1) * LANES)
        for c in range(tr // rc):
            base = c * rc
            y = jnp.zeros((rc, LANES), F32)
            for r in range(sub):
                a_r = None
                for q in range((last - r) // sub + 1):
                    d = sub * q + r
                    start = base + halo - sub - sub * q
                    win = h_s[start:start + rc + sub, lanes]
                    term = w_ref[last - d, :, lanes] * win.reshape(rc // sub + 1, sub, LANES)
                    a_r = term if a_r is None else a_r + term
                y = y + a_r.reshape(rc + sub, LANES)[sub - r:sub - r + rc, :]
            y_s[base:base + rc, lanes] = y
    y = _rms(y_s[...] + b_ref[...], n_ref[...])
    o_ref[...] = (y * jax.nn.sigmoid(y)).astype(BF16)


def _conv(of, w, b, g, batch, seq, casts):
    m = of.shape[0]
    tr, halo = CONV_ROWS, CONV_HALO
    nt = seq // tr
    per = tr // halo
    cur = lambda col: (lambda bb, i: (bb * nt + i, col))
    prev = lambda col: (lambda bb, i: (jnp.maximum((bb * nt + i) * per - 1, 0), col))
    cast_specs = [_cast_specs(wc, lc, batch * nt, lambda bb, i: bb * nt + i) for wc, lc in casts]
    return pl.pallas_call(
        functools.partial(_conv_kernel, ncast=len(casts)),
        grid=(batch, nt),
        in_specs=[
            pl.BlockSpec((tr, GROUP_WIDTH), cur(2)),
            pl.BlockSpec((tr, GROUP_WIDTH), cur(3)),
            pl.BlockSpec((halo, GROUP_WIDTH), prev(2)),
            pl.BlockSpec((halo, GROUP_WIDTH), prev(3)),
            pl.BlockSpec((CONV_WIDTH, SUBLANES, GROUP_WIDTH), lambda bb, i: (0, 0, 0)),
            pl.BlockSpec((1, GROUP_WIDTH), lambda bb, i: (0, 0)),
            pl.BlockSpec((1, GROUP_WIDTH), lambda bb, i: (0, 0)),
            *[cs[0] for cs in cast_specs],
        ],
        out_specs=[pl.BlockSpec((tr, GROUP_WIDTH), lambda bb, i: (bb * nt + i, 0)),
                   *[cs[1] for cs in cast_specs]],
        out_shape=[jax.ShapeDtypeStruct((m, GROUP_WIDTH), BF16), *[cs[2] for cs in cast_specs]],
        scratch_shapes=[pltpu.VMEM((halo + tr, GROUP_WIDTH), F32),
                        pltpu.VMEM((tr, GROUP_WIDTH), F32)],
        compiler_params=_params(("parallel", "parallel")),
        name="conformer_conv",
    )(of, of, of, of, jnp.broadcast_to(w[:, None, :], (CONV_WIDTH, SUBLANES, GROUP_WIDTH)), b, g,
      *[wc for wc, _ in casts])


def _out_kernel(a_ref, b_ref, c_ref, d_ref, w_ref, x_ref, g_ref, o_ref):
    gw = GROUP_WIDTH
    y = jnp.dot(a_ref[...], w_ref[0:gw, :], preferred_element_type=F32)
    y = y + jnp.dot(b_ref[...], w_ref[gw:2 * gw, :], preferred_element_type=F32)
    y = y + jnp.dot(c_ref[...], w_ref[2 * gw:3 * gw, :], preferred_element_type=F32)
    y = y + jnp.dot(d_ref[...], w_ref[3 * gw:4 * gw, :], preferred_element_type=F32)
    o_ref[...] = x_ref[...] + _rms(y, g_ref[...])


def _out_proj(mix, w, x, g):
    m = x.shape[0]
    tm = TM_OUT
    mix_spec = pl.BlockSpec((tm, GROUP_WIDTH), lambda i: (i, 0))
    return pl.pallas_call(
        _out_kernel,
        grid=(m // tm,),
        in_specs=[mix_spec, mix_spec, mix_spec, mix_spec,
                  pl.BlockSpec((D_MODEL, D_MODEL), lambda i: (0, 0)),
                  pl.BlockSpec((tm, D_MODEL), lambda i: (i, 0)),
                  pl.BlockSpec((1, D_MODEL), lambda i: (0, 0))],
        out_specs=pl.BlockSpec((tm, D_MODEL), lambda i: (i, 0)),
        out_shape=jax.ShapeDtypeStruct((m, D_MODEL), F32),
        compiler_params=_params(("parallel",)),
        name="out_proj",
    )(*mix, w, x, g)


def _ffn_kernel(x_ref, gpre_ref, w1_ref, w2_ref, gpost_ref, o_ref, h_s):
    j = pl.program_id(1)

    @pl.when(j == 0)
    def _():
        h_s[...] = _rms(x_ref[...], gpre_ref[...]).astype(BF16)
        o_ref[...] = jnp.zeros(o_ref.shape, F32)

    f = jnp.dot(h_s[...], w1_ref[...], preferred_element_type=F32)
    f = jnp.square(jnp.maximum(f, 0.0)).astype(BF16)
    for c in range(D_MODEL // FFN_OUT_CHUNK):
        cols = slice(c * FFN_OUT_CHUNK, (c + 1) * FFN_OUT_CHUNK)
        o_ref[:, cols] += jnp.dot(f, w2_ref[:, cols], preferred_element_type=F32)

    @pl.when(j == pl.num_programs(1) - 1)
    def _():
        o_ref[...] = x_ref[...] + _rms(o_ref[...], gpost_ref[...])


def _ffn(x, gpre, w1, w2, gpost):
    m = x.shape[0]
    tm, tf = TM_FFN, TF_FFN
    return pl.pallas_call(
        _ffn_kernel,
        grid=(m // tm, D_FF // tf),
        in_specs=[
            pl.BlockSpec((tm, D_MODEL), lambda i, j: (i, 0)),
            pl.BlockSpec((1, D_MODEL), lambda i, j: (0, 0)),
            pl.BlockSpec((D_MODEL, tf), lambda i, j: (0, j)),
            pl.BlockSpec((tf, D_MODEL), lambda i, j: (j, 0)),
            pl.BlockSpec((1, D_MODEL), lambda i, j: (0, 0)),
        ],
        out_specs=pl.BlockSpec((tm, D_MODEL), lambda i, j: (i, 0)),
        out_shape=jax.ShapeDtypeStruct((m, D_MODEL), F32),
        scratch_shapes=[pltpu.VMEM((tm, D_MODEL), BF16)],
        compiler_params=pltpu.CompilerParams(dimension_semantics=("parallel", "arbitrary"),
                                             vmem_limit_bytes=VMEM_LIMIT_FFN),
        name="ffn",
    )(x, gpre, w1, w2, gpost)


def kernel(x, g_mix_pre, g_mix_post, w_in, gmlp_w, gmlp_b, diff_lam, diff_subln, conv_w, conv_b,
           conv_norm, w_out, g_ffn_pre, g_ffn_post, w_ff1, w_ff2):
    batch, seq, _ = x.shape
    depth = w_in.shape[0]
    xf = x.reshape(batch * seq, D_MODEL)
    row = lambda v: v.reshape(1, -1)
    w_in_li = w_in
    for li in range(depth):
        of, ob = _in_proj(xf, row(g_mix_pre[li]), w_in_li, li)

        w_t = gmlp_w[li].transpose(1, 0, 2).reshape(GMLP_CHUNK, GMLP_GROUPS * GMLP_CHUNK)
        bias_full = jnp.repeat(gmlp_b[li].T, GROUP_WIDTH // GMLP_GROUPS, axis=1)
        out_a = _gmlp(of, w_t, bias_full)

        lam_init = 0.8 - 0.6 * float(np.exp(-0.3 * li))
        out_b, w_ff1_b = _diff_attention(ob, diff_lam[li], diff_subln[li].reshape(-1, 1),
                                         lam_init, batch, seq, w_ff1, li)
        out_c, w_ff2_b = _dil_attention(ob, batch, seq, w_ff2, li)
        casts = [(w_out, li)] + ([(w_in, li + 1)] if li + 1 < depth else [])
        out_d, w_out_b, *w_next = _conv(of, conv_w[li], row(conv_b[li]), row(conv_norm[li]),
                                        batch, seq, casts)
        if w_next:
            w_in_li = w_next[0]

        xf = _out_proj((out_a, out_b, out_c, out_d), w_out_b, xf, row(g_mix_post[li]))
        xf = _ffn(xf, row(g_ffn_pre[li]), w_ff1_b, w_ff2_b, row(g_ffn_post[li]))
    return xf.reshape(batch, seq, D_MODEL)
```

```python
import functools

import numpy as np
import jax
import jax.numpy as jnp
from jax import lax
from jax.experimental import pallas as pl
from jax.experimental.pallas import tpu as pltpu

F32 = jnp.float32
BF16 = jnp.bfloat16

D_MODEL = 2048
GROUP_WIDTH = 512
N_IN_SPLITS = 10
D_FF = 4 * D_MODEL
NORM_EPS = 1e-6
GMLP_CHUNK = 128
GMLP_GROUPS = 8
DIFF_HEADS = 4
DIL_HEADS = 8
HEAD_DIM = 64
DIL_PATTERNS = ((128, 1), (512, 4), (2048, 16))
DIL_BLOCK = 128
CONV_WIDTH = 31
N_ALIBI_HEADS = DIFF_HEADS + DIL_HEADS
ATTN_SCALE = HEAD_DIM ** -0.5
MASKED = -1e30
LOG2E = 1.4426950408889634
DIFF_ONES_ROWS = 16
DIFF_HEAD_GROUP = 4

LANES = 128
VMEM_LIMIT = 48 * 1024 * 1024
VMEM_LIMIT_FFN = 58 * 1024 * 1024

TM_PROJ = 1024
TN_PROJ_F32 = 512
TN_PROJ_BF16 = 1024
TM_OUT = 512
OUT_ROW_CHUNK = 256
TM_FFN = 1024
TF_FFN = 512
FFN_OUT_CHUNK = 512
GMLP_ROWS = 512
DIFF_TQ = 256
DIFF_TK = 256
CONV_ROWS = 256
CONV_HALO = 32
CONV_CHUNK = 128
SUBLANES = 8


def _alibi_slopes():
    i = np.arange(1, N_ALIBI_HEADS + 1, dtype=np.float64)
    s = 2.0 ** (-8.0 * i / N_ALIBI_HEADS)
    diff_idx = np.arange(0, N_ALIBI_HEADS, 3)
    dil_idx = np.array([j for j in range(N_ALIBI_HEADS) if j % 3 != 0])
    return s[diff_idx], s[dil_idx]


def _rms(x, g):
    return x * lax.rsqrt(jnp.mean(x * x, axis=-1, keepdims=True) + NORM_EPS) * g


def _params(sem):
    return pltpu.CompilerParams(dimension_semantics=sem, vmem_limit_bytes=VMEM_LIMIT)


def _proj_kernel(x_ref, g_ref, w_ref, of_ref, ob_ref, h_ref, *, n_f32_tiles):
    j = pl.program_id(1)

    @pl.when(j == 0)
    def _():
        h_ref[...] = _rms(x_ref[...], g_ref[...]).astype(BF16)

    y = jnp.dot(h_ref[...], w_ref[...].astype(BF16), preferred_element_type=F32)

    @pl.when(j < n_f32_tiles)
    def _():
        of_ref[...] = y

    @pl.when(j >= n_f32_tiles)
    def _():
        ob_ref[...] = y.astype(BF16)


def _in_proj(x, g, w, li):
    m = x.shape[0]
    tm = TM_PROJ
    tn = TN_PROJ_F32 if w.ndim == 3 else TN_PROJ_BF16
    ntile = N_IN_SPLITS * GROUP_WIDTH // tn
    kf = 2 * GROUP_WIDTH // tn
    col = lambda j: jnp.where(j < kf, j, jnp.where(j < 2 * kf, j + ntile - 2 * kf, j - kf))
    if w.ndim == 3:
        w_spec = pl.BlockSpec((None, D_MODEL, tn), lambda i, j: (li, 0, col(j)))
    else:
        w_spec = pl.BlockSpec((D_MODEL, tn), lambda i, j: (0, col(j)))
    return pl.pallas_call(
        functools.partial(_proj_kernel, n_f32_tiles=2 * kf),
        grid=(m // tm, ntile),
        in_specs=[
            pl.BlockSpec((tm, D_MODEL), lambda i, j: (i, 0)),
            pl.BlockSpec((1, D_MODEL), lambda i, j: (0, 0)),
            w_spec,
        ],
        out_specs=[
            pl.BlockSpec((tm, tn), lambda i, j: (i, jnp.minimum(j, 2 * kf - 1))),
            pl.BlockSpec((tm, tn), lambda i, j: (i, jnp.maximum(j - 2 * kf, 0))),
        ],
        out_shape=[
            jax.ShapeDtypeStruct((m, 4 * GROUP_WIDTH), F32),
            jax.ShapeDtypeStruct((m, 6 * GROUP_WIDTH), BF16),
        ],
        scratch_shapes=[pltpu.VMEM((tm, D_MODEL), BF16)],
        compiler_params=_params(("parallel", "arbitrary")),
        name="in_proj",
    )(x, g, w)


def _gelu(x):
    return jax.nn.gelu(x, approximate=True)


def _gmlp_kernel(u_ref, v_ref, w_ref, b_ref, o_ref):
    c = GMLP_CHUNK
    t_idx = lax.broadcasted_iota(jnp.int32, w_ref.shape, 0)
    s_idx = lax.broadcasted_iota(jnp.int32, w_ref.shape, 1) % c
    w = jnp.where(s_idx <= t_idx, w_ref[...], 0.0).astype(BF16)
    bias = b_ref[...]
    lane = lax.broadcasted_iota(jnp.int32, (c, LANES), 1)
    low = lane < HEAD_DIM
    for ci in range(GMLP_ROWS // c):
        rows = slice(ci * c, (ci + 1) * c)
        u = _gelu(u_ref[rows, :])
        v = _gelu(v_ref[rows, :])
        mu = jnp.mean(v, axis=-1, keepdims=True)
        vc = v - mu
        v = vc * lax.rsqrt(jnp.mean(vc * vc, axis=-1, keepdims=True) + NORM_EPS)
        zs = []
        for p in range(GMLP_GROUPS // 2):
            vp = v[:, p * LANES:(p + 1) * LANES]
            rhs = jnp.concatenate([jnp.where(low, vp, 0.0), jnp.where(low, 0.0, vp)], axis=0)
            zs.append(jnp.dot(w[:, p * 2 * c:(p + 1) * 2 * c], rhs.astype(BF16),
                              preferred_element_type=F32))
        z = jnp.concatenate(zs, axis=1) + bias
        o_ref[rows, :] = (u * z).astype(BF16)


def _gmlp(of, w_t, bias_full):
    m = of.shape[0]
    tr = GMLP_ROWS
    return pl.pallas_call(
        _gmlp_kernel,
        grid=(m // tr,),
        in_specs=[
            pl.BlockSpec((tr, GROUP_WIDTH), lambda i: (i, 0)),
            pl.BlockSpec((tr, GROUP_WIDTH), lambda i: (i, 1)),
            pl.BlockSpec((GMLP_CHUNK, GMLP_GROUPS * GMLP_CHUNK), lambda i: (0, 0)),
            pl.BlockSpec((GMLP_CHUNK, GROUP_WIDTH), lambda i: (0, 0)),
        ],
        out_specs=pl.BlockSpec((tr, GROUP_WIDTH), lambda i: (i, 0)),
        out_shape=jax.ShapeDtypeStruct((m, GROUP_WIDTH), BF16),
        compiler_params=_params(("parallel",)),
        name="gmlp",
    )(of, of, w_t, bias_full)


def _stack_queries(q):
    lane = lax.broadcasted_iota(jnp.int32, q.shape, 1)
    low = lane < HEAD_DIM
    zero = jnp.zeros_like(q)
    q2 = jnp.concatenate([jnp.where(low, q, zero), jnp.where(low, zero, q)], axis=0)
    return (q2.astype(F32) * (ATTN_SCALE * LOG2E)).astype(BF16)


def _qk(q2, k):
    return lax.dot_general(q2, k, (((1,), (1,)), ((), ())), preferred_element_type=F32)


def _cast_specs(w, li, nsteps, step_of):
    _, rows, cols = w.shape
    slab = rows // nsteps
    assert slab * nsteps == rows and slab % 16 == 0
    in_spec = pl.BlockSpec((None, slab, cols), lambda *ids: (li, step_of(*ids), 0))
    out_spec = pl.BlockSpec((slab, cols), lambda *ids: (step_of(*ids), 0))
    return in_spec, out_spec, jax.ShapeDtypeStruct((rows, cols), BF16)


def _diff_kernel(q_ref, k_ref, v_ref, bias_ref, lam_ref, g_ref, wf_ref, o_ref, wb_ref,
                 vt_s, acc_s, *, lam_init, slopes):
    wb_ref[...] = wf_ref[...].astype(BF16)
    qi = pl.program_id(1)
    tq, tk = DIFF_TQ, DIFF_TK
    seq = k_ref.shape[0]
    heads = range(DIFF_HEADS)
    hcols = lambda h: slice(h * LANES, (h + 1) * LANES)

    @pl.when(qi == 0)
    def _():
        for h in heads:
            for c in range(seq // tk):
                rows = slice(c * tk, (c + 1) * tk)
                vt_s[h, 0:LANES, rows] = v_ref[rows, hcols(h)].astype(F32).T.astype(BF16)
            vt_s[h, LANES:, :] = jnp.ones((DIFF_ONES_ROWS, seq), BF16)

    q2 = [_stack_queries(q_ref[:, hcols(h)]) for h in heads]
    acc_s[...] = jnp.zeros(acc_s.shape, F32)

    def step_group(hs, start, which, dist, m_prev):
        if which == 0:
            bias = {h: jnp.concatenate([bias_ref[h, 0, :, 0:LANES]] * (2 * tq // LANES), axis=1)
                    for h in hs}
        else:
            bias = {h: bias_ref[h, 1] for h in hs}
        sts = {h: _qk(k_ref[pl.ds(start, tk), hcols(h)], q2[h]) + bias[h]
               for h in hs}
        shifts = {h: (-slopes[h] * LOG2E) * dist for h in hs}
        m_next = {h: jnp.maximum(m_prev[h], jnp.max(sts[h], axis=0, keepdims=True) + shifts[h])
                  for h in hs}
        ps = {h: jnp.exp2(sts[h] - (m_next[h] - shifts[h])).astype(BF16) for h in hs}
        alphas = {h: jnp.exp2(m_prev[h] - m_next[h]) for h in hs}
        pvs = {h: jnp.dot(vt_s[h, :, pl.ds(start, tk)], ps[h], preferred_element_type=F32)
               for h in hs}
        for h in hs:
            acc_s[h] = alphas[h] * acc_s[h] + pvs[h]
        return m_next

    def step(start, which, dist, m_prev):
        m_next = {}
        for g in range(0, DIFF_HEADS, DIFF_HEAD_GROUP):
            m_next.update(step_group(range(g, g + DIFF_HEAD_GROUP), start, which, dist, m_prev))
        return tuple(m_next[h] for h in heads)

    def body(j, m_prev):
        return step(pl.multiple_of(j * tk, tk), 0, ((qi - j) * tq).astype(F32), m_prev)

    m0 = jnp.full((1, 2 * tq), MASKED, F32)
    m_prev = lax.fori_loop(0, qi, body, tuple(m0 for _ in heads))
    step(pl.multiple_of(qi * tq, tq), 1, 0.0, m_prev)

    lp = lam_ref[...]
    lam = (jnp.exp(jnp.sum(lp[0:1] * lp[1:2], axis=1, keepdims=True))
           - jnp.exp(jnp.sum(lp[2:3] * lp[3:4], axis=1, keepdims=True)) + lam_init)
    for h in heads:
        o = acc_s[h, 0:LANES, :] / acc_s[h, LANES:LANES + 1, :]
        o = o[:, :tq] - lam * o[:, tq:]
        y = o * lax.rsqrt(jnp.mean(o * o, axis=0, keepdims=True) + NORM_EPS) * g_ref[...]
        o_ref[:, hcols(h)] = (y * (1.0 - lam_init)).T.astype(BF16)


def _diff_bias_tiles(slopes):
    tq, tk = DIFF_TQ, DIFF_TK
    rq = (np.arange(2 * tq) % tq)[None, :].astype(np.float64)
    c = np.arange(tk)[:, None].astype(np.float64)
    tiles = np.zeros((DIFF_HEADS, 2, tk, 2 * tq), np.float64)
    for h, sl in enumerate(slopes):
        tiles[h, 0] = sl * LOG2E * c + 0.0 * rq
        tiles[h, 1] = np.where(rq - c >= 0, sl * LOG2E * c, MASKED)
    return tiles.astype(np.float32)


def _diff_attention(ob, lam_p, subln_g, lam_init, batch, seq, w_cast, li):
    m = ob.shape[0]
    tq, tk = DIFF_TQ, DIFF_TK
    assert tq == tk and seq % tq == 0
    nq = seq // tq
    slopes, _ = _alibi_slopes()
    bias = jnp.asarray(_diff_bias_tiles(slopes))
    kern = functools.partial(_diff_kernel, lam_init=lam_init,
                             slopes=tuple(float(s) for s in slopes))
    w_in_spec, w_out_spec, w_shape = _cast_specs(w_cast, li, batch * nq, lambda b, i: b * nq + i)
    return pl.pallas_call(
        kern,
        grid=(batch, nq),
        in_specs=[
            pl.BlockSpec((tq, GROUP_WIDTH), lambda b, i: (b * nq + i, 0)),
            pl.BlockSpec((seq, GROUP_WIDTH), lambda b, i: (b, 1)),
            pl.BlockSpec((seq, GROUP_WIDTH), lambda b, i: (b, 2)),
            pl.BlockSpec((DIFF_HEADS, 2, tk, 2 * tq), lambda b, i: (0, 0, 0, 0)),
            pl.BlockSpec((4, HEAD_DIM), lambda b, i: (0, 0)),
            pl.BlockSpec((2 * HEAD_DIM, 1), lambda b, i: (0, 0)),
            w_in_spec,
        ],
        out_specs=[pl.BlockSpec((tq, GROUP_WIDTH), lambda b, i: (b * nq + i, 0)), w_out_spec],
        out_shape=[jax.ShapeDtypeStruct((m, GROUP_WIDTH), BF16), w_shape],
        scratch_shapes=[
            pltpu.VMEM((DIFF_HEADS, LANES + DIFF_ONES_ROWS, seq), BF16),
            pltpu.VMEM((DIFF_HEADS, LANES + DIFF_ONES_ROWS, 2 * tq), F32),
        ],
        compiler_params=_params(("parallel", "arbitrary")),
        name="diff_attn",
    )(ob, ob, ob, bias, lam_p, subln_g, w_cast)


def _dil_kernel(q_ref, k_ref, v_ref, bias_ref, wf_ref, o_ref, wb_ref, qf, kf, vf, m_s, l_s, o_s,
                *, seq):
    wb_ref[...] = wf_ref[...].astype(BF16)
    n = DIL_BLOCK
    qf[...] = q_ref[...].astype(F32)
    kf[...] = k_ref[...].astype(F32)
    vf[...] = v_ref[...].astype(F32)
    lane = lax.broadcasted_iota(jnp.int32, (n, LANES), 1)
    low = lane < HEAD_DIM

    for pi, (window, dil) in enumerate(DIL_PATTERNS):
        nblk = seq // (n * dil)
        for r in range(dil):
            for c in range(nblk):
                cur = pl.ds(r + dil * n * c, n, stride=dil)
                q2 = _stack_queries(qf[cur, :])
                if c > 0:
                    prev = pl.ds(r + dil * n * (c - 1), n, stride=dil)
                    kb = jnp.concatenate([kf[prev, :], kf[cur, :]], axis=0).astype(BF16)
                    vb = jnp.concatenate([vf[prev, :], vf[cur, :]], axis=0).astype(BF16)
                    bias = bias_ref[pi]
                else:
                    kb = kf[cur, :].astype(BF16)
                    vb = vf[cur, :].astype(BF16)
                    bias = bias_ref[pi, :, n:]
                s = _qk(q2, kb) + bias
                mx = jnp.max(s, axis=1, keepdims=True)
                p = jnp.exp2(s - mx).astype(BF16)
                vb1 = jnp.concatenate([vb, jnp.ones(vb.shape, BF16)], axis=1)
                ol = jnp.dot(p, vb1, preferred_element_type=F32)
                o, l = ol[:, :LANES], ol[:, LANES:]
                o_s[pi, cur, :] = jnp.where(low, o[:n], o[n:])
                m_s[pi, cur, :] = jnp.where(low, mx[:n], mx[n:])
                l_s[pi, cur, :] = jnp.where(low, l[:n], l[n:])

    m_all = jnp.maximum(jnp.maximum(m_s[0], m_s[1]), m_s[2])
    num = jnp.zeros((seq, LANES), F32)
    den = jnp.zeros((seq, LANES), F32)
    for pi in range(len(DIL_PATTERNS)):
        w = jnp.exp2(m_s[pi] - m_all)
        num = num + w * o_s[pi]
        den = den + w * l_s[pi]
    o_ref[...] = (num / den).astype(BF16)


def _dil_bias_tiles(slopes):
    n = DIL_BLOCK
    qi = (np.arange(2 * n) % n)[:, None]
    ki = np.arange(2 * n)[None, :]
    step = n + qi - ki
    valid = (step >= 0) & (step <= n)
    tiles = np.zeros((len(DIL_PATTERNS), DIL_HEADS // 2, 2 * n, 2 * n), np.float64)
    for pi, (_, dil) in enumerate(DIL_PATTERNS):
        for p in range(DIL_HEADS // 2):
            sl = np.where(np.arange(2 * n) < n, slopes[2 * p], slopes[2 * p + 1])[:, None]
            tiles[pi, p] = np.where(valid, -sl * LOG2E * (step * dil), MASKED)
    return tiles.astype(np.float32)


def _dil_attention(ob, batch, seq, w_cast, li):
    m = ob.shape[0]
    _, slopes = _alibi_slopes()
    bias = jnp.asarray(_dil_bias_tiles(slopes))
    npairs = DIL_HEADS // 2
    npat = len(DIL_PATTERNS)
    col0 = 3 * GROUP_WIDTH // LANES
    kern = functools.partial(_dil_kernel, seq=seq)
    w_in_spec, w_out_spec, w_shape = _cast_specs(w_cast, li, batch * npairs,
                                                 lambda b, p: b * npairs + p)
    return pl.pallas_call(
        kern,
        grid=(batch, npairs),
        in_specs=[
            pl.BlockSpec((seq, LANES), lambda b, p: (b, col0 + p)),
            pl.BlockSpec((seq, LANES), lambda b, p: (b, col0 + npairs + p)),
            pl.BlockSpec((seq, LANES), lambda b, p: (b, col0 + 2 * npairs + p)),
            pl.BlockSpec((npat, None, 2 * DIL_BLOCK, 2 * DIL_BLOCK), lambda b, p: (0, p, 0, 0)),
            w_in_spec,
        ],
        out_specs=[pl.BlockSpec((seq, LANES), lambda b, p: (b, p)), w_out_spec],
        out_shape=[jax.ShapeDtypeStruct((m, GROUP_WIDTH), BF16), w_shape],
        scratch_shapes=[
            pltpu.VMEM((seq, LANES), F32),
            pltpu.VMEM((seq, LANES), F32),
            pltpu.VMEM((seq, LANES), F32),
            pltpu.VMEM((npat, seq, LANES), F32),
            pltpu.VMEM((npat, seq, LANES), F32),
            pltpu.VMEM((npat, seq, LANES), F32),
        ],
        compiler_params=_params(("parallel", "parallel")),
        name="dil_attn",
    )(ob, ob, ob, bias, w_cast)


def _conv_kernel(a_ref, g_ref, ap_ref, gp_ref, w_ref, b_ref, n_ref, *rest, ncast):
    wf_refs, o_ref, wb_refs = rest[:ncast], rest[ncast], rest[ncast + 1:2 * ncast + 1]
    h_s, y_s = rest[2 * ncast + 1:]
    for wf_ref, wb_ref in zip(wf_refs, wb_refs):
        wb_ref[...] = wf_ref[...].astype(BF16)
    i = pl.program_id(1)
    tr, halo = CONV_ROWS, CONV_HALO
    hp = ap_ref[...] * jax.nn.sigmoid(gp_ref[...])
    h_s[0:halo, :] = jnp.where(i > 0, hp, 0.0)
    h_s[halo:, :] = a_ref[...] * jax.nn.sigmoid(g_ref[...])
    rc, sub, last = CONV_CHUNK, SUBLANES, CONV_WIDTH - 1

    for lt in range(GROUP_WIDTH // LANES):
        lanes = slice(lt * LANES, (lt + 1) * LANES)
        for c in range(tr // rc):
            base = c * rc
            y = jnp.zeros((rc, LANES), F32)
            for r in range(sub):
                a_r = None
                for q in range((last - r) // sub + 1):
                    d = sub * q + r
                    start = base + halo - sub - sub * q
                    win = h_s[start:start + rc + sub, lanes]
                    term = w_ref[last - d, :, lanes] * win.reshape(rc // sub + 1, sub, LANES)
                    a_r = term if a_r is None else a_r + term
                y = y + a_r.reshape(rc + sub, LANES)[sub - r:sub - r + rc, :]
            y_s[base:base + rc, lanes] = y
    y = _rms(y_s[...] + b_ref[...], n_ref[...])
    o_ref[...] = (y * jax.nn.sigmoid(y)).astype(BF16)


def _conv(of, w, b, g, batch, seq, casts):
    m = of.shape[0]
    tr, halo = CONV_ROWS, CONV_HALO
    nt = seq // tr
    per = tr // halo
    cur = lambda col: (lambda bb, i: (bb * nt + i, col))
    prev = lambda col: (lambda bb, i: (jnp.maximum((bb * nt + i) * per - 1, 0), col))
    cast_specs = [_cast_specs(wc, lc, batch * nt, lambda bb, i: bb * nt + i) for wc, lc in casts]
    return pl.pallas_call(
        functools.partial(_conv_kernel, ncast=len(casts)),
        grid=(batch, nt),
        in_specs=[
            pl.BlockSpec((tr, GROUP_WIDTH), cur(2)),
            pl.BlockSpec((tr, GROUP_WIDTH), cur(3)),
            pl.BlockSpec((halo, GROUP_WIDTH), prev(2)),
            pl.BlockSpec((halo, GROUP_WIDTH), prev(3)),
            pl.BlockSpec((CONV_WIDTH, SUBLANES, GROUP_WIDTH), lambda bb, i: (0, 0, 0)),
            pl.BlockSpec((1, GROUP_WIDTH), lambda bb, i: (0, 0)),
            pl.BlockSpec((1, GROUP_WIDTH), lambda bb, i: (0, 0)),
            *[cs[0] for cs in cast_specs],
        ],
        out_specs=[pl.BlockSpec((tr, GROUP_WIDTH), lambda bb, i: (bb * nt + i, 0)),
                   *[cs[1] for cs in cast_specs]],
        out_shape=[jax.ShapeDtypeStruct((m, GROUP_WIDTH), BF16), *[cs[2] for cs in cast_specs]],
        scratch_shapes=[pltpu.VMEM((halo + tr, GROUP_WIDTH), F32),
                        pltpu.VMEM((tr, GROUP_WIDTH), F32)],
        compiler_params=_params(("parallel", "parallel")),
        name="conformer_conv",
    )(of, of, of, of, jnp.broadcast_to(w[:, None, :], (CONV_WIDTH, SUBLANES, GROUP_WIDTH)), b, g,
      *[wc for wc, _ in casts])


def _out_kernel(a_ref, b_ref, c_ref, d_ref, w_ref, x_ref, g_ref, o_ref):
    gw = GROUP_WIDTH
    for c in range(TM_OUT // OUT_ROW_CHUNK):
        rows = slice(c * OUT_ROW_CHUNK, (c + 1) * OUT_ROW_CHUNK)
        y = jnp.dot(a_ref[rows, :], w_ref[0:gw, :], preferred_element_type=F32)
        y = y + jnp.dot(b_ref[rows, :], w_ref[gw:2 * gw, :], preferred_element_type=F32)
        y = y + jnp.dot(c_ref[rows, :], w_ref[2 * gw:3 * gw, :], preferred_element_type=F32)
        y = y + jnp.dot(d_ref[rows, :], w_ref[3 * gw:4 * gw, :], preferred_element_type=F32)
        o_ref[rows, :] = x_ref[rows, :] + _rms(y, g_ref[...])


def _out_proj(mix, w, x, g):
    m = x.shape[0]
    tm = TM_OUT
    mix_spec = pl.BlockSpec((tm, GROUP_WIDTH), lambda i: (i, 0))
    return pl.pallas_call(
        _out_kernel,
        grid=(m // tm,),
        in_specs=[mix_spec, mix_spec, mix_spec, mix_spec,
                  pl.BlockSpec((D_MODEL, D_MODEL), lambda i: (0, 0)),
                  pl.BlockSpec((tm, D_MODEL), lambda i: (i, 0)),
                  pl.BlockSpec((1, D_MODEL), lambda i: (0, 0))],
        out_specs=pl.BlockSpec((tm, D_MODEL), lambda i: (i, 0)),
        out_shape=jax.ShapeDtypeStruct((m, D_MODEL), F32),
        compiler_params=_params(("parallel",)),
        name="out_proj",
    )(*mix, w, x, g)


def _ffn_kernel(x_ref, gpre_ref, w1_ref, w2_ref, gpost_ref, o_ref, h_s):
    j = pl.program_id(1)

    @pl.when(j == 0)
    def _():
        h_s[...] = _rms(x_ref[...], gpre_ref[...]).astype(BF16)
        o_ref[...] = jnp.zeros(o_ref.shape, F32)

    f = jnp.dot(h_s[...], w1_ref[...], preferred_element_type=F32)
    f = jnp.square(jnp.maximum(f, 0.0)).astype(BF16)
    for c in range(D_MODEL // FFN_OUT_CHUNK):
        cols = slice(c * FFN_OUT_CHUNK, (c + 1) * FFN_OUT_CHUNK)
        o_ref[:, cols] += jnp.dot(f, w2_ref[:, cols], preferred_element_type=F32)

    @pl.when(j == pl.num_programs(1) - 1)
    def _():
        o_ref[...] = x_ref[...] + _rms(o_ref[...], gpost_ref[...])


def _ffn(x, gpre, w1, w2, gpost):
    m = x.shape[0]
    tm, tf = TM_FFN, TF_FFN
    return pl.pallas_call(
        _ffn_kernel,
        grid=(m // tm, D_FF // tf),
        in_specs=[
            pl.BlockSpec((tm, D_MODEL), lambda i, j: (i, 0)),
            pl.BlockSpec((1, D_MODEL), lambda i, j: (0, 0)),
            pl.BlockSpec((D_MODEL, tf), lambda i, j: (0, j)),
            pl.BlockSpec((tf, D_MODEL), lambda i, j: (j, 0)),
            pl.BlockSpec((1, D_MODEL), lambda i, j: (0, 0)),
        ],
        out_specs=pl.BlockSpec((tm, D_MODEL), lambda i, j: (i, 0)),
        out_shape=jax.ShapeDtypeStruct((m, D_MODEL), F32),
        scratch_shapes=[pltpu.VMEM((tm, D_MODEL), BF16)],
        compiler_params=pltpu.CompilerParams(dimension_semantics=("parallel", "arbitrary"),
                                             vmem_limit_bytes=VMEM_LIMIT_FFN),
        name="ffn",
    )(x, gpre, w1, w2, gpost)


def kernel(x, g_mix_pre, g_mix_post, w_in, gmlp_w, gmlp_b, diff_lam, diff_subln, conv_w, conv_b,
           conv_norm, w_out, g_ffn_pre, g_ffn_post, w_ff1, w_ff2):
    batch, seq, _ = x.shape
    depth = w_in.shape[0]
    xf = x.reshape(batch * seq, D_MODEL)
    row = lambda v: v.reshape(1, -1)
    w_in_li = w_in
    for li in range(depth):
        of, ob = _in_proj(xf, row(g_mix_pre[li]), w_in_li, li)

        w_t = gmlp_w[li].transpose(1, 0, 2).reshape(GMLP_CHUNK, GMLP_GROUPS * GMLP_CHUNK)
        bias_full = jnp.repeat(gmlp_b[li].T, GROUP_WIDTH // GMLP_GROUPS, axis=1)
        out_a = _gmlp(of, w_t, bias_full)

        lam_init = 0.8 - 0.6 * float(np.exp(-0.3 * li))
        out_b, w_ff1_b = _diff_attention(ob, diff_lam[li], diff_subln[li].reshape(-1, 1),
                                         lam_init, batch, seq, w_ff1, li)
        out_c, w_ff2_b = _dil_attention(ob, batch, seq, w_ff2, li)
        casts = [(w_out, li)] + ([(w_in, li + 1)] if li + 1 < depth else [])
        out_d, w_out_b, *w_next = _conv(of, conv_w[li], row(conv_b[li]), row(conv_norm[li]),
                                        batch, seq, casts)
        if w_next:
            w_in_li = w_next[0]

        xf = _out_proj((out_a, out_b, out_c, out_d), w_out_b, xf, row(g_mix_post[li]))
        xf = _ffn(xf, row(g_ffn_pre[li]), w_ff1_b, w_ff2_b, row(g_ffn_post[li]))
    return xf.reshape(batch, seq, D_MODEL)
```

```python
import functools

import numpy as np
import jax
import jax.numpy as jnp
from jax import lax
from jax.experimental import pallas as pl
from jax.experimental.pallas import tpu as pltpu

F32 = jnp.float32
BF16 = jnp.bfloat16

D_MODEL = 2048
GROUP_WIDTH = 512
N_IN_SPLITS = 10
D_FF = 4 * D_MODEL
NORM_EPS = 1e-6
GMLP_CHUNK = 128
GMLP_GROUPS = 8
DIFF_HEADS = 4
DIL_HEADS = 8
HEAD_DIM = 64
DIL_PATTERNS = ((128, 1), (512, 4), (2048, 16))
DIL_BLOCK = 128
CONV_WIDTH = 31
N_ALIBI_HEADS = DIFF_HEADS + DIL_HEADS
ATTN_SCALE = HEAD_DIM ** -0.5
MASKED = -1e30
LOG2E = 1.4426950408889634
DIFF_ONES_ROWS = 16
DIFF_HEAD_GROUP = 4

LANES = 128
VMEM_LIMIT = 48 * 1024 * 1024
VMEM_LIMIT_BIG = 58 * 1024 * 1024

TM_PROJ = 1024
TN_PROJ = 1024
TM_OUT = 1024
OUT_ROW_CHUNK = 256
TM_FFN = 1024
TF_FFN = 512
FFN_OUT_CHUNK = 512
GMLP_ROWS = 1024
DIFF_TQ = 256
DIFF_TK = 256
CONV_ROWS = 512
CONV_HALO = 32
CONV_CHUNK = 128
SUBLANES = 8


def _alibi_slopes():
    i = np.arange(1, N_ALIBI_HEADS + 1, dtype=np.float64)
    s = 2.0 ** (-8.0 * i / N_ALIBI_HEADS)
    diff_idx = np.arange(0, N_ALIBI_HEADS, 3)
    dil_idx = np.array([j for j in range(N_ALIBI_HEADS) if j % 3 != 0])
    return s[diff_idx], s[dil_idx]


def _rms(x, g):
    return x * lax.rsqrt(jnp.mean(x * x, axis=-1, keepdims=True) + NORM_EPS) * g


def _params(sem, limit=VMEM_LIMIT):
    return pltpu.CompilerParams(dimension_semantics=sem, vmem_limit_bytes=limit)


def _proj_kernel(x_ref, g_ref, w_ref, of_ref, ob_ref, h_ref, *, n_f32_tiles):
    j = pl.program_id(1)

    @pl.when(j == 0)
    def _():
        h_ref[...] = _rms(x_ref[...], g_ref[...]).astype(BF16)

    def tile(write):
        for c in range(w_ref.shape[1] // GROUP_WIDTH):
            cols = slice(c * GROUP_WIDTH, (c + 1) * GROUP_WIDTH)
            write(cols, jnp.dot(h_ref[...], w_ref[:, cols].astype(BF16),
                                preferred_element_type=F32))

    @pl.when(j < n_f32_tiles)
    def _():
        def write(cols, y):
            of_ref[:, cols] = y
        tile(write)

    @pl.when(j >= n_f32_tiles)
    def _():
        def write(cols, y):
            ob_ref[:, cols] = y.astype(BF16)
        tile(write)


def _in_proj(x, g, w, li):
    m = x.shape[0]
    tm = TM_PROJ
    tn = TN_PROJ
    ntile = N_IN_SPLITS * GROUP_WIDTH // tn
    kf = 2 * GROUP_WIDTH // tn
    col = lambda j: jnp.where(j < kf, j, jnp.where(j < 2 * kf, j + ntile - 2 * kf, j - kf))
    if w.ndim == 3:
        w_spec = pl.BlockSpec((None, D_MODEL, tn), lambda i, j: (li, 0, col(j)))
    else:
        w_spec = pl.BlockSpec((D_MODEL, tn), lambda i, j: (0, col(j)))
    return pl.pallas_call(
        functools.partial(_proj_kernel, n_f32_tiles=2 * kf),
        grid=(m // tm, ntile),
        in_specs=[
            pl.BlockSpec((tm, D_MODEL), lambda i, j: (i, 0)),
            pl.BlockSpec((1, D_MODEL), lambda i, j: (0, 0)),
            w_spec,
        ],
        out_specs=[
            pl.BlockSpec((tm, tn), lambda i, j: (i, jnp.minimum(j, 2 * kf - 1))),
            pl.BlockSpec((tm, tn), lambda i, j: (i, jnp.maximum(j - 2 * kf, 0))),
        ],
        out_shape=[
            jax.ShapeDtypeStruct((m, 4 * GROUP_WIDTH), F32),
            jax.ShapeDtypeStruct((m, 6 * GROUP_WIDTH), BF16),
        ],
        scratch_shapes=[pltpu.VMEM((tm, D_MODEL), BF16)],
        compiler_params=_params(("parallel", "arbitrary"), VMEM_LIMIT_BIG),
        name="in_proj",
    )(x, g, w)


def _gelu(x):
    return jax.nn.gelu(x, approximate=True)


def _gmlp_kernel(u_ref, v_ref, w_ref, b_ref, o_ref):
    c = GMLP_CHUNK
    t_idx = lax.broadcasted_iota(jnp.int32, w_ref.shape, 0)
    s_idx = lax.broadcasted_iota(jnp.int32, w_ref.shape, 1) % c
    w = jnp.where(s_idx <= t_idx, w_ref[...], 0.0).astype(BF16)
    bias = b_ref[...]
    lane = lax.broadcasted_iota(jnp.int32, (c, LANES), 1)
    low = lane < HEAD_DIM
    for ci in range(GMLP_ROWS // c):
        rows = slice(ci * c, (ci + 1) * c)
        u = _gelu(u_ref[rows, :])
        v = _gelu(v_ref[rows, :])
        mu = jnp.mean(v, axis=-1, keepdims=True)
        vc = v - mu
        v = vc * lax.rsqrt(jnp.mean(vc * vc, axis=-1, keepdims=True) + NORM_EPS)
        zs = []
        for p in range(GMLP_GROUPS // 2):
            vp = v[:, p * LANES:(p + 1) * LANES]
            rhs = jnp.concatenate([jnp.where(low, vp, 0.0), jnp.where(low, 0.0, vp)], axis=0)
            zs.append(jnp.dot(w[:, p * 2 * c:(p + 1) * 2 * c], rhs.astype(BF16),
                              preferred_element_type=F32))
        z = jnp.concatenate(zs, axis=1) + bias
        o_ref[rows, :] = (u * z).astype(BF16)


def _gmlp(of, w_t, bias_full):
    m = of.shape[0]
    tr = GMLP_ROWS
    return pl.pallas_call(
        _gmlp_kernel,
        grid=(m // tr,),
        in_specs=[
            pl.BlockSpec((tr, GROUP_WIDTH), lambda i: (i, 0)),
            pl.BlockSpec((tr, GROUP_WIDTH), lambda i: (i, 1)),
            pl.BlockSpec((GMLP_CHUNK, GMLP_GROUPS * GMLP_CHUNK), lambda i: (0, 0)),
            pl.BlockSpec((GMLP_CHUNK, GROUP_WIDTH), lambda i: (0, 0)),
        ],
        out_specs=pl.BlockSpec((tr, GROUP_WIDTH), lambda i: (i, 0)),
        out_shape=jax.ShapeDtypeStruct((m, GROUP_WIDTH), BF16),
        compiler_params=_params(("parallel",)),
        name="gmlp",
    )(of, of, w_t, bias_full)


def _stack_queries(q):
    lane = lax.broadcasted_iota(jnp.int32, q.shape, 1)
    low = lane < HEAD_DIM
    zero = jnp.zeros_like(q)
    q2 = jnp.concatenate([jnp.where(low, q, zero), jnp.where(low, zero, q)], axis=0)
    return (q2.astype(F32) * (ATTN_SCALE * LOG2E)).astype(BF16)


def _qk(q2, k):
    return lax.dot_general(q2, k, (((1,), (1,)), ((), ())), preferred_element_type=F32)


def _cast_specs(w, li, nsteps, step_of):
    _, rows, cols = w.shape
    slab = rows // nsteps
    assert slab * nsteps == rows and slab % 16 == 0
    in_spec = pl.BlockSpec((None, slab, cols), lambda *ids: (li, step_of(*ids), 0))
    out_spec = pl.BlockSpec((slab, cols), lambda *ids: (step_of(*ids), 0))
    return in_spec, out_spec, jax.ShapeDtypeStruct((rows, cols), BF16)


def _diff_kernel(q_ref, k_ref, v_ref, bias_ref, lam_ref, g_ref, wf_ref, o_ref, wb_ref,
                 vt_s, acc_s, *, lam_init, slopes):
    wb_ref[...] = wf_ref[...].astype(BF16)
    qi = pl.program_id(1)
    tq, tk = DIFF_TQ, DIFF_TK
    seq = k_ref.shape[0]
    heads = range(DIFF_HEADS)
    hcols = lambda h: slice(h * LANES, (h + 1) * LANES)

    @pl.when(qi == 0)
    def _():
        for h in heads:
            for c in range(seq // tk):
                rows = slice(c * tk, (c + 1) * tk)
                vt_s[h, 0:LANES, rows] = v_ref[rows, hcols(h)].astype(F32).T.astype(BF16)
            vt_s[h, LANES:, :] = jnp.ones((DIFF_ONES_ROWS, seq), BF16)

    q2 = [_stack_queries(q_ref[:, hcols(h)]) for h in heads]
    acc_s[...] = jnp.zeros(acc_s.shape, F32)

    def step_group(hs, start, which, dist, m_prev):
        if which == 0:
            bias = {h: jnp.concatenate([bias_ref[h, 0, :, 0:LANES]] * (2 * tq // LANES), axis=1)
                    for h in hs}
        else:
            bias = {h: bias_ref[h, 1] for h in hs}
        sts = {h: _qk(k_ref[pl.ds(start, tk), hcols(h)], q2[h]) + bias[h]
               for h in hs}
        shifts = {h: (-slopes[h] * LOG2E) * dist for h in hs}
        m_next = {h: jnp.maximum(m_prev[h], jnp.max(sts[h], axis=0, keepdims=True) + shifts[h])
                  for h in hs}
        ps = {h: jnp.exp2(sts[h] - (m_next[h] - shifts[h])).astype(BF16) for h in hs}
        alphas = {h: jnp.exp2(m_prev[h] - m_next[h]) for h in hs}
        pvs = {h: jnp.dot(vt_s[h, :, pl.ds(start, tk)], ps[h], preferred_element_type=F32)
               for h in hs}
        for h in hs:
            acc_s[h] = alphas[h] * acc_s[h] + pvs[h]
        return m_next

    def step(start, which, dist, m_prev):
        m_next = {}
        for g in range(0, DIFF_HEADS, DIFF_HEAD_GROUP):
            m_next.update(step_group(range(g, g + DIFF_HEAD_GROUP), start, which, dist, m_prev))
        return tuple(m_next[h] for h in heads)

    def body(j, m_prev):
        return step(pl.multiple_of(j * tk, tk), 0, ((qi - j) * tq).astype(F32), m_prev)

    m0 = jnp.full((1, 2 * tq), MASKED, F32)
    m_prev = lax.fori_loop(0, qi, body, tuple(m0 for _ in heads))
    step(pl.multiple_of(qi * tq, tq), 1, 0.0, m_prev)

    lp = lam_ref[...]
    lam = (jnp.exp(jnp.sum(lp[0:1] * lp[1:2], axis=1, keepdims=True))
           - jnp.exp(jnp.sum(lp[2:3] * lp[3:4], axis=1, keepdims=True)) + lam_init)
    for h in heads:
        o = acc_s[h, 0:LANES, :] / acc_s[h, LANES:LANES + 1, :]
        o = o[:, :tq] - lam * o[:, tq:]
        y = o * lax.rsqrt(jnp.mean(o * o, axis=0, keepdims=True) + NORM_EPS) * g_ref[...]
        o_ref[:, hcols(h)] = (y * (1.0 - lam_init)).T.astype(BF16)


def _diff_bias_tiles(slopes):
    tq, tk = DIFF_TQ, DIFF_TK
    rq = (np.arange(2 * tq) % tq)[None, :].astype(np.float64)
    c = np.arange(tk)[:, None].astype(np.float64)
    tiles = np.zeros((DIFF_HEADS, 2, tk, 2 * tq), np.float64)
    for h, sl in enumerate(slopes):
        tiles[h, 0] = sl * LOG2E * c + 0.0 * rq
        tiles[h, 1] = np.where(rq - c >= 0, sl * LOG2E * c, MASKED)
    return tiles.astype(np.float32)


def _diff_attention(ob, lam_p, subln_g, lam_init, batch, seq, w_cast, li):
    m = ob.shape[0]
    tq, tk = DIFF_TQ, DIFF_TK
    assert tq == tk and seq % tq == 0
    nq = seq // tq
    slopes, _ = _alibi_slopes()
    bias = jnp.asarray(_diff_bias_tiles(slopes))
    kern = functools.partial(_diff_kernel, lam_init=lam_init,
                             slopes=tuple(float(s) for s in slopes))
    w_in_spec, w_out_spec, w_shape = _cast_specs(w_cast, li, batch * nq, lambda b, i: b * nq + i)
    return pl.pallas_call(
        kern,
        grid=(batch, nq),
        in_specs=[
            pl.BlockSpec((tq, GROUP_WIDTH), lambda b, i: (b * nq + i, 0)),
            pl.BlockSpec((seq, GROUP_WIDTH), lambda b, i: (b, 1)),
            pl.BlockSpec((seq, GROUP_WIDTH), lambda b, i: (b, 2)),
            pl.BlockSpec((DIFF_HEADS, 2, tk, 2 * tq), lambda b, i: (0, 0, 0, 0)),
            pl.BlockSpec((4, HEAD_DIM), lambda b, i: (0, 0)),
            pl.BlockSpec((2 * HEAD_DIM, 1), lambda b, i: (0, 0)),
            w_in_spec,
        ],
        out_specs=[pl.BlockSpec((tq, GROUP_WIDTH), lambda b, i: (b * nq + i, 0)), w_out_spec],
        out_shape=[jax.ShapeDtypeStruct((m, GROUP_WIDTH), BF16), w_shape],
        scratch_shapes=[
            pltpu.VMEM((DIFF_HEADS, LANES + DIFF_ONES_ROWS, seq), BF16),
            pltpu.VMEM((DIFF_HEADS, LANES + DIFF_ONES_ROWS, 2 * tq), F32),
        ],
        compiler_params=_params(("parallel", "arbitrary")),
        name="diff_attn",
    )(ob, ob, ob, bias, lam_p, subln_g, w_cast)


def _dil_kernel(q_ref, k_ref, v_ref, bias_ref, wf_ref, o_ref, wb_ref, qf, kf, vf, m_s, l_s, o_s,
                *, seq):
    wb_ref[...] = wf_ref[...].astype(BF16)
    n = DIL_BLOCK
    qf[...] = q_ref[...].astype(F32)
    kf[...] = k_ref[...].astype(F32)
    vf[...] = v_ref[...].astype(F32)
    lane = lax.broadcasted_iota(jnp.int32, (n, LANES), 1)
    low = lane < HEAD_DIM

    for pi, (window, dil) in enumerate(DIL_PATTERNS):
        nblk = seq // (n * dil)
        for r in range(dil):
            for c in range(nblk):
                cur = pl.ds(r + dil * n * c, n, stride=dil)
                q2 = _stack_queries(qf[cur, :])
                if c > 0:
                    prev = pl.ds(r + dil * n * (c - 1), n, stride=dil)
                    kb = jnp.concatenate([kf[prev, :], kf[cur, :]], axis=0).astype(BF16)
                    vb = jnp.concatenate([vf[prev, :], vf[cur, :]], axis=0).astype(BF16)
                    bias = bias_ref[pi]
                else:
                    kb = kf[cur, :].astype(BF16)
                    vb = vf[cur, :].astype(BF16)
                    bias = bias_ref[pi, :, n:]
                s = _qk(q2, kb) + bias
                mx = jnp.max(s, axis=1, keepdims=True)
                p = jnp.exp2(s - mx).astype(BF16)
                vb1 = jnp.concatenate([vb, jnp.ones(vb.shape, BF16)], axis=1)
                ol = jnp.dot(p, vb1, preferred_element_type=F32)
                o, l = ol[:, :LANES], ol[:, LANES:]
                o_s[pi, cur, :] = jnp.where(low, o[:n], o[n:])
                m_s[pi, cur, :] = jnp.where(low, mx[:n], mx[n:])
                l_s[pi, cur, :] = jnp.where(low, l[:n], l[n:])

    m_all = jnp.maximum(jnp.maximum(m_s[0], m_s[1]), m_s[2])
    num = jnp.zeros((seq, LANES), F32)
    den = jnp.zeros((seq, LANES), F32)
    for pi in range(len(DIL_PATTERNS)):
        w = jnp.exp2(m_s[pi] - m_all)
        num = num + w * o_s[pi]
        den = den + w * l_s[pi]
    o_ref[...] = (num / den).astype(BF16)


def _dil_bias_tiles(slopes):
    n = DIL_BLOCK
    qi = (np.arange(2 * n) % n)[:, None]
    ki = np.arange(2 * n)[None, :]
    step = n + qi - ki
    valid = (step >= 0) & (step <= n)
    tiles = np.zeros((len(DIL_PATTERNS), DIL_HEADS // 2, 2 * n, 2 * n), np.float64)
    for pi, (_, dil) in enumerate(DIL_PATTERNS):
        for p in range(DIL_HEADS // 2):
            sl = np.where(np.arange(2 * n) < n, slopes[2 * p], slopes[2 * p + 1])[:, None]
            tiles[pi, p] = np.where(valid, -sl * LOG2E * (step * dil), MASKED)
    return tiles.astype(np.float32)


def _dil_attention(ob, batch, seq, w_cast, li):
    m = ob.shape[0]
    _, slopes = _alibi_slopes()
    bias = jnp.asarray(_dil_bias_tiles(slopes))
    npairs = DIL_HEADS // 2
    npat = len(DIL_PATTERNS)
    col0 = 3 * GROUP_WIDTH // LANES
    kern = functools.partial(_dil_kernel, seq=seq)
    w_in_spec, w_out_spec, w_shape = _cast_specs(w_cast, li, batch * npairs,
                                                 lambda b, p: b * npairs + p)
    return pl.pallas_call(
        kern,
        grid=(batch, npairs),
        in_specs=[
            pl.BlockSpec((seq, LANES), lambda b, p: (b, col0 + p)),
            pl.BlockSpec((seq, LANES), lambda b, p: (b, col0 + npairs + p)),
            pl.BlockSpec((seq, LANES), lambda b, p: (b, col0 + 2 * npairs + p)),
            pl.BlockSpec((npat, None, 2 * DIL_BLOCK, 2 * DIL_BLOCK), lambda b, p: (0, p, 0, 0)),
            w_in_spec,
        ],
        out_specs=[pl.BlockSpec((seq, LANES), lambda b, p: (b, p)), w_out_spec],
        out_shape=[jax.ShapeDtypeStruct((m, GROUP_WIDTH), BF16), w_shape],
        scratch_shapes=[
            pltpu.VMEM((seq, LANES), F32),
            pltpu.VMEM((seq, LANES), F32),
            pltpu.VMEM((seq, LANES), F32),
            pltpu.VMEM((npat, seq, LANES), F32),
            pltpu.VMEM((npat, seq, LANES), F32),
            pltpu.VMEM((npat, seq, LANES), F32),
        ],
        compiler_params=_params(("parallel", "parallel")),
        name="dil_attn",
    )(ob, ob, ob, bias, w_cast)


def _conv_kernel(a_ref, g_ref, ap_ref, gp_ref, w_ref, b_ref, n_ref, *rest, ncast):
    wf_refs, o_ref, wb_refs = rest[:ncast], rest[ncast], rest[ncast + 1:2 * ncast + 1]
    h_s, y_s = rest[2 * ncast + 1:]
    for wf_ref, wb_ref in zip(wf_refs, wb_refs):
        wb_ref[...] = wf_ref[...].astype(BF16)
    i = pl.program_id(1)
    tr, halo = CONV_ROWS, CONV_HALO
    hp = ap_ref[...] * jax.nn.sigmoid(gp_ref[...])
    h_s[0:halo, :] = jnp.where(i > 0, hp, 0.0)
    h_s[halo:, :] = a_ref[...] * jax.nn.sigmoid(g_ref[...])
    rc, sub, last = CONV_CHUNK, SUBLANES, CONV_WIDTH - 1

    for lt in range(GROUP_WIDTH // LANES):
        lanes = slice(lt * LANES, (lt + 1) * LANES)
        for c in range(tr // rc):
            base = c * rc
            y = jnp.zeros((rc, LANES), F32)
            for r in range(sub):
                a_r = None
                for q in range((last - r) // sub + 1):
                    d = sub * q + r
                    start = base + halo - sub - sub * q
                    win = h_s[start:start + rc + sub, lanes]
                    term = w_ref[last - d, :, lanes] * win.reshape(rc // sub + 1, sub, LANES)
                    a_r = term if a_r is None else a_r + term
                y = y + a_r.reshape(rc + sub, LANES)[sub - r:sub - r + rc, :]
            y_s[base:base + rc, lanes] = y
    y = _rms(y_s[...] + b_ref[...], n_ref[...])
    o_ref[...] = (y * jax.nn.sigmoid(y)).astype(BF16)


def _conv(of, w, b, g, batch, seq, casts):
    m = of.shape[0]
    tr, halo = CONV_ROWS, CONV_HALO
    nt = seq // tr
    per = tr // halo
    cur = lambda col: (lambda bb, i: (bb * nt + i, col))
    prev = lambda col: (lambda bb, i: (jnp.maximum((bb * nt + i) * per - 1, 0), col))
    cast_specs = [_cast_specs(wc, lc, batch * nt, lambda bb, i: bb * nt + i) for wc, lc in casts]
    return pl.pallas_call(
        functools.partial(_conv_kernel, ncast=len(casts)),
        grid=(batch, nt),
        in_specs=[
            pl.BlockSpec((tr, GROUP_WIDTH), cur(2)),
            pl.BlockSpec((tr, GROUP_WIDTH), cur(3)),
            pl.BlockSpec((halo, GROUP_WIDTH), prev(2)),
            pl.BlockSpec((halo, GROUP_WIDTH), prev(3)),
            pl.BlockSpec((CONV_WIDTH, SUBLANES, GROUP_WIDTH), lambda bb, i: (0, 0, 0)),
            pl.BlockSpec((1, GROUP_WIDTH), lambda bb, i: (0, 0)),
            pl.BlockSpec((1, GROUP_WIDTH), lambda bb, i: (0, 0)),
            *[cs[0] for cs in cast_specs],
        ],
        out_specs=[pl.BlockSpec((tr, GROUP_WIDTH), lambda bb, i: (bb * nt + i, 0)),
                   *[cs[1] for cs in cast_specs]],
        out_shape=[jax.ShapeDtypeStruct((m, GROUP_WIDTH), BF16), *[cs[2] for cs in cast_specs]],
        scratch_shapes=[pltpu.VMEM((halo + tr, GROUP_WIDTH), F32),
                        pltpu.VMEM((tr, GROUP_WIDTH), F32)],
        compiler_params=_params(("parallel", "parallel")),
        name="conformer_conv",
    )(of, of, of, of, jnp.broadcast_to(w[:, None, :], (CONV_WIDTH, SUBLANES, GROUP_WIDTH)), b, g,
      *[wc for wc, _ in casts])


def _out_kernel(a_ref, b_ref, c_ref, d_ref, w_ref, x_ref, g_ref, o_ref):
    gw = GROUP_WIDTH
    for c in range(TM_OUT // OUT_ROW_CHUNK):
        rows = slice(c * OUT_ROW_CHUNK, (c + 1) * OUT_ROW_CHUNK)
        y = jnp.dot(a_ref[rows, :], w_ref[0:gw, :], preferred_element_type=F32)
        y = y + jnp.dot(b_ref[rows, :], w_ref[gw:2 * gw, :], preferred_element_type=F32)
        y = y + jnp.dot(c_ref[rows, :], w_ref[2 * gw:3 * gw, :], preferred_element_type=F32)
        y = y + jnp.dot(d_ref[rows, :], w_ref[3 * gw:4 * gw, :], preferred_element_type=F32)
        o_ref[rows, :] = x_ref[rows, :] + _rms(y, g_ref[...])


def _out_proj(mix, w, x, g):
    m = x.shape[0]
    tm = TM_OUT
    mix_spec = pl.BlockSpec((tm, GROUP_WIDTH), lambda i: (i, 0))
    return pl.pallas_call(
        _out_kernel,
        grid=(m // tm,),
        in_specs=[mix_spec, mix_spec, mix_spec, mix_spec,
                  pl.BlockSpec((D_MODEL, D_MODEL), lambda i: (0, 0), pipeline_mode=pl.Buffered(1)),
                  pl.BlockSpec((tm, D_MODEL), lambda i: (i, 0)),
                  pl.BlockSpec((1, D_MODEL), lambda i: (0, 0))],
        out_specs=pl.BlockSpec((tm, D_MODEL), lambda i: (i, 0)),
        out_shape=jax.ShapeDtypeStruct((m, D_MODEL), F32),
        compiler_params=_params(("parallel",), VMEM_LIMIT_BIG),
        name="out_proj",
    )(*mix, w, x, g)


def _ffn_kernel(x_ref, gpre_ref, w1_ref, w2_ref, gpost_ref, o_ref, h_s):
    j = pl.program_id(1)

    @pl.when(j == 0)
    def _():
        h_s[...] = _rms(x_ref[...], gpre_ref[...]).astype(BF16)
        o_ref[...] = jnp.zeros(o_ref.shape, F32)

    f = jnp.dot(h_s[...], w1_ref[...], preferred_element_type=F32)
    f = jnp.square(jnp.maximum(f, 0.0)).astype(BF16)
    for c in range(D_MODEL // FFN_OUT_CHUNK):
        cols = slice(c * FFN_OUT_CHUNK, (c + 1) * FFN_OUT_CHUNK)
        o_ref[:, cols] += jnp.dot(f, w2_ref[:, cols], preferred_element_type=F32)

    @pl.when(j == pl.num_programs(1) - 1)
    def _():
        o_ref[...] = x_ref[...] + _rms(o_ref[...], gpost_ref[...])


def _ffn(x, gpre, w1, w2, gpost):
    m = x.shape[0]
    tm, tf = TM_FFN, TF_FFN
    return pl.pallas_call(
        _ffn_kernel,
        grid=(m // tm, D_FF // tf),
        in_specs=[
            pl.BlockSpec((tm, D_MODEL), lambda i, j: (i, 0)),
            pl.BlockSpec((1, D_MODEL), lambda i, j: (0, 0)),
            pl.BlockSpec((D_MODEL, tf), lambda i, j: (0, j)),
            pl.BlockSpec((tf, D_MODEL), lambda i, j: (j, 0)),
            pl.BlockSpec((1, D_MODEL), lambda i, j: (0, 0)),
        ],
        out_specs=pl.BlockSpec((tm, D_MODEL), lambda i, j: (i, 0)),
        out_shape=jax.ShapeDtypeStruct((m, D_MODEL), F32),
        scratch_shapes=[pltpu.VMEM((tm, D_MODEL), BF16)],
        compiler_params=_params(("parallel", "arbitrary"), VMEM_LIMIT_BIG),
        name="ffn",
    )(x, gpre, w1, w2, gpost)


def kernel(x, g_mix_pre, g_mix_post, w_in, gmlp_w, gmlp_b, diff_lam, diff_subln, conv_w, conv_b,
           conv_norm, w_out, g_ffn_pre, g_ffn_post, w_ff1, w_ff2):
    batch, seq, _ = x.shape
    depth = w_in.shape[0]
    xf = x.reshape(batch * seq, D_MODEL)
    row = lambda v: v.reshape(1, -1)
    w_in_li = w_in
    for li in range(depth):
        of, ob = _in_proj(xf, row(g_mix_pre[li]), w_in_li, li)

        w_t = gmlp_w[li].transpose(1, 0, 2).reshape(GMLP_CHUNK, GMLP_GROUPS * GMLP_CHUNK)
        bias_full = jnp.repeat(gmlp_b[li].T, GROUP_WIDTH // GMLP_GROUPS, axis=1)
        out_a = _gmlp(of, w_t, bias_full)

        lam_init = 0.8 - 0.6 * float(np.exp(-0.3 * li))
        out_b, w_ff1_b = _diff_attention(ob, diff_lam[li], diff_subln[li].reshape(-1, 1),
                                         lam_init, batch, seq, w_ff1, li)
        out_c, w_ff2_b = _dil_attention(ob, batch, seq, w_ff2, li)
        casts = [(w_out, li)] + ([(w_in, li + 1)] if li + 1 < depth else [])
        out_d, w_out_b, *w_next = _conv(of, conv_w[li], row(conv_b[li]), row(conv_norm[li]),
                                        batch, seq, casts)
        if w_next:
            w_in_li = w_next[0]

        xf = _out_proj((out_a, out_b, out_c, out_d), w_out_b, xf, row(g_mix_post[li]))
        xf = _ffn(xf, row(g_ffn_pre[li]), w_ff1_b, w_ff2_b, row(g_ffn_post[li]))
    return xf.reshape(batch, seq, D_MODEL)
```

```python
import functools

import numpy as np
import jax
import jax.numpy as jnp
from jax import lax
from jax.experimental import pallas as pl
from jax.experimental.pallas import tpu as pltpu

F32 = jnp.float32
BF16 = jnp.bfloat16

D_MODEL = 2048
GROUP_WIDTH = 512
N_IN_SPLITS = 10
D_FF = 4 * D_MODEL
NORM_EPS = 1e-6
GMLP_CHUNK = 128
GMLP_GROUPS = 8
DIFF_HEADS = 4
DIL_HEADS = 8
HEAD_DIM = 64
DIL_PATTERNS = ((128, 1), (512, 4), (2048, 16))
DIL_BLOCK = 128
CONV_WIDTH = 31
N_ALIBI_HEADS = DIFF_HEADS + DIL_HEADS
ATTN_SCALE = HEAD_DIM ** -0.5
MASKED = -1e30
LOG2E = 1.4426950408889634
DIFF_ONES_ROWS = 16
DIFF_HEAD_GROUP = 4

LANES = 128
VMEM_LIMIT = 48 * 1024 * 1024
VMEM_LIMIT_BIG = 58 * 1024 * 1024

TM_PROJ = 1024
TN_PROJ = 1024
TM_OUT = 512
OUT_ROW_CHUNK = 256
TM_FFN = 1024
TF_FFN = 512
FFN_OUT_CHUNK = 512
GMLP_ROWS = 1024
DIFF_TQ = 256
DIFF_TK = 256
CONV_ROWS = 512
CONV_HALO = 32
CONV_CHUNK = 128
SUBLANES = 8


def _alibi_slopes():
    i = np.arange(1, N_ALIBI_HEADS + 1, dtype=np.float64)
    s = 2.0 ** (-8.0 * i / N_ALIBI_HEADS)
    diff_idx = np.arange(0, N_ALIBI_HEADS, 3)
    dil_idx = np.array([j for j in range(N_ALIBI_HEADS) if j % 3 != 0])
    return s[diff_idx], s[dil_idx]


def _rms(x, g):
    return x * lax.rsqrt(jnp.mean(x * x, axis=-1, keepdims=True) + NORM_EPS) * g


def _params(sem, limit=VMEM_LIMIT):
    return pltpu.CompilerParams(dimension_semantics=sem, vmem_limit_bytes=limit)


def _proj_kernel(x_ref, g_ref, w_ref, of_ref, ob_ref, h_ref, *, n_f32_tiles):
    j = pl.program_id(1)

    @pl.when(j == 0)
    def _():
        h_ref[...] = _rms(x_ref[...], g_ref[...]).astype(BF16)

    def tile(write):
        for c in range(w_ref.shape[1] // GROUP_WIDTH):
            cols = slice(c * GROUP_WIDTH, (c + 1) * GROUP_WIDTH)
            write(cols, jnp.dot(h_ref[...], w_ref[:, cols].astype(BF16),
                                preferred_element_type=F32))

    @pl.when(j < n_f32_tiles)
    def _():
        def write(cols, y):
            of_ref[:, cols] = y
        tile(write)

    @pl.when(j >= n_f32_tiles)
    def _():
        def write(cols, y):
            ob_ref[:, cols] = y.astype(BF16)
        tile(write)


def _in_proj(x, g, w, li):
    m = x.shape[0]
    tm = TM_PROJ
    tn = TN_PROJ
    ntile = N_IN_SPLITS * GROUP_WIDTH // tn
    kf = 2 * GROUP_WIDTH // tn
    col = lambda j: jnp.where(j < kf, j, jnp.where(j < 2 * kf, j + ntile - 2 * kf, j - kf))
    if w.ndim == 3:
        w_spec = pl.BlockSpec((None, D_MODEL, tn), lambda i, j: (li, 0, col(j)))
    else:
        w_spec = pl.BlockSpec((D_MODEL, tn), lambda i, j: (0, col(j)))
    return pl.pallas_call(
        functools.partial(_proj_kernel, n_f32_tiles=2 * kf),
        grid=(m // tm, ntile),
        in_specs=[
            pl.BlockSpec((tm, D_MODEL), lambda i, j: (i, 0)),
            pl.BlockSpec((1, D_MODEL), lambda i, j: (0, 0)),
            w_spec,
        ],
        out_specs=[
            pl.BlockSpec((tm, tn), lambda i, j: (i, jnp.minimum(j, 2 * kf - 1))),
            pl.BlockSpec((tm, tn), lambda i, j: (i, jnp.maximum(j - 2 * kf, 0))),
        ],
        out_shape=[
            jax.ShapeDtypeStruct((m, 4 * GROUP_WIDTH), F32),
            jax.ShapeDtypeStruct((m, 6 * GROUP_WIDTH), BF16),
        ],
        scratch_shapes=[pltpu.VMEM((tm, D_MODEL), BF16)],
        compiler_params=_params(("parallel", "arbitrary"), VMEM_LIMIT_BIG),
        name="in_proj",
    )(x, g, w)


def _gelu(x):
    return jax.nn.gelu(x, approximate=True)


def _gmlp_kernel(u_ref, v_ref, w_ref, b_ref, o_ref):
    c = GMLP_CHUNK
    t_idx = lax.broadcasted_iota(jnp.int32, w_ref.shape, 0)
    s_idx = lax.broadcasted_iota(jnp.int32, w_ref.shape, 1) % c
    w = jnp.where(s_idx <= t_idx, w_ref[...], 0.0).astype(BF16)
    bias = b_ref[...]
    lane = lax.broadcasted_iota(jnp.int32, (c, LANES), 1)
    low = lane < HEAD_DIM
    for ci in range(GMLP_ROWS // c):
        rows = slice(ci * c, (ci + 1) * c)
        u = _gelu(u_ref[rows, :])
        v = _gelu(v_ref[rows, :])
        mu = jnp.mean(v, axis=-1, keepdims=True)
        vc = v - mu
        v = vc * lax.rsqrt(jnp.mean(vc * vc, axis=-1, keepdims=True) + NORM_EPS)
        zs = []
        for p in range(GMLP_GROUPS // 2):
            vp = v[:, p * LANES:(p + 1) * LANES]
            rhs = jnp.concatenate([jnp.where(low, vp, 0.0), jnp.where(low, 0.0, vp)], axis=0)
            zs.append(jnp.dot(w[:, p * 2 * c:(p + 1) * 2 * c], rhs.astype(BF16),
                              preferred_element_type=F32))
        z = jnp.concatenate(zs, axis=1) + bias
        o_ref[rows, :] = (u * z).astype(BF16)


def _gmlp(of, w_t, bias_full):
    m = of.shape[0]
    tr = GMLP_ROWS
    return pl.pallas_call(
        _gmlp_kernel,
        grid=(m // tr,),
        in_specs=[
            pl.BlockSpec((tr, GROUP_WIDTH), lambda i: (i, 0)),
            pl.BlockSpec((tr, GROUP_WIDTH), lambda i: (i, 1)),
            pl.BlockSpec((GMLP_CHUNK, GMLP_GROUPS * GMLP_CHUNK), lambda i: (0, 0)),
            pl.BlockSpec((GMLP_CHUNK, GROUP_WIDTH), lambda i: (0, 0)),
        ],
        out_specs=pl.BlockSpec((tr, GROUP_WIDTH), lambda i: (i, 0)),
        out_shape=jax.ShapeDtypeStruct((m, GROUP_WIDTH), BF16),
        compiler_params=_params(("parallel",)),
        name="gmlp",
    )(of, of, w_t, bias_full)


def _stack_queries(q):
    lane = lax.broadcasted_iota(jnp.int32, q.shape, 1)
    low = lane < HEAD_DIM
    zero = jnp.zeros_like(q)
    q2 = jnp.concatenate([jnp.where(low, q, zero), jnp.where(low, zero, q)], axis=0)
    return (q2.astype(F32) * (ATTN_SCALE * LOG2E)).astype(BF16)


def _qk(q2, k):
    return lax.dot_general(q2, k, (((1,), (1,)), ((), ())), preferred_element_type=F32)


def _cast_specs(w, li, nsteps, step_of):
    _, rows, cols = w.shape
    slab = rows // nsteps
    assert slab * nsteps == rows and slab % 16 == 0
    in_spec = pl.BlockSpec((None, slab, cols), lambda *ids: (li, step_of(*ids), 0))
    out_spec = pl.BlockSpec((slab, cols), lambda *ids: (step_of(*ids), 0))
    return in_spec, out_spec, jax.ShapeDtypeStruct((rows, cols), BF16)


def _diff_kernel(q_ref, k_ref, v_ref, bias_ref, lam_ref, g_ref, wf_ref, o_ref, wb_ref,
                 vt_s, acc_s, *, lam_init, slopes):
    wb_ref[...] = wf_ref[...].astype(BF16)
    qi = pl.program_id(1)
    tq, tk = DIFF_TQ, DIFF_TK
    seq = k_ref.shape[0]
    heads = range(DIFF_HEADS)
    hcols = lambda h: slice(h * LANES, (h + 1) * LANES)

    @pl.when(qi == 0)
    def _():
        for h in heads:
            for c in range(seq // tk):
                rows = slice(c * tk, (c + 1) * tk)
                vt_s[h, 0:LANES, rows] = v_ref[rows, hcols(h)].astype(F32).T.astype(BF16)
            vt_s[h, LANES:, :] = jnp.ones((DIFF_ONES_ROWS, seq), BF16)

    q2 = [_stack_queries(q_ref[:, hcols(h)]) for h in heads]
    acc_s[...] = jnp.zeros(acc_s.shape, F32)

    def step_group(hs, start, which, dist, m_prev):
        if which == 0:
            bias = {h: jnp.concatenate([bias_ref[h, 0, :, 0:LANES]] * (2 * tq // LANES), axis=1)
                    for h in hs}
        else:
            bias = {h: bias_ref[h, 1] for h in hs}
        sts = {h: _qk(k_ref[pl.ds(start, tk), hcols(h)], q2[h]) + bias[h]
               for h in hs}
        shifts = {h: (-slopes[h] * LOG2E) * dist for h in hs}
        m_next = {h: jnp.maximum(m_prev[h], jnp.max(sts[h], axis=0, keepdims=True) + shifts[h])
                  for h in hs}
        ps = {h: jnp.exp2(sts[h] - (m_next[h] - shifts[h])).astype(BF16) for h in hs}
        alphas = {h: jnp.exp2(m_prev[h] - m_next[h]) for h in hs}
        pvs = {h: jnp.dot(vt_s[h, :, pl.ds(start, tk)], ps[h], preferred_element_type=F32)
               for h in hs}
        for h in hs:
            acc_s[h] = alphas[h] * acc_s[h] + pvs[h]
        return m_next

    def step(start, which, dist, m_prev):
        m_next = {}
        for g in range(0, DIFF_HEADS, DIFF_HEAD_GROUP):
            m_next.update(step_group(range(g, g + DIFF_HEAD_GROUP), start, which, dist, m_prev))
        return tuple(m_next[h] for h in heads)

    def body(j, m_prev):
        return step(pl.multiple_of(j * tk, tk), 0, ((qi - j) * tq).astype(F32), m_prev)

    m0 = jnp.full((1, 2 * tq), MASKED, F32)
    m_prev = lax.fori_loop(0, qi, body, tuple(m0 for _ in heads))
    step(pl.multiple_of(qi * tq, tq), 1, 0.0, m_prev)

    lp = lam_ref[...]
    lam = (jnp.exp(jnp.sum(lp[0:1] * lp[1:2], axis=1, keepdims=True))
           - jnp.exp(jnp.sum(lp[2:3] * lp[3:4], axis=1, keepdims=True)) + lam_init)
    for h in heads:
        o = acc_s[h, 0:LANES, :] / acc_s[h, LANES:LANES + 1, :]
        o = o[:, :tq] - lam * o[:, tq:]
        y = o * lax.rsqrt(jnp.mean(o * o, axis=0, keepdims=True) + NORM_EPS) * g_ref[...]
        o_ref[:, hcols(h)] = (y * (1.0 - lam_init)).T.astype(BF16)


def _diff_bias_tiles(slopes):
    tq, tk = DIFF_TQ, DIFF_TK
    rq = (np.arange(2 * tq) % tq)[None, :].astype(np.float64)
    c = np.arange(tk)[:, None].astype(np.float64)
    tiles = np.zeros((DIFF_HEADS, 2, tk, 2 * tq), np.float64)
    for h, sl in enumerate(slopes):
        tiles[h, 0] = sl * LOG2E * c + 0.0 * rq
        tiles[h, 1] = np.where(rq - c >= 0, sl * LOG2E * c, MASKED)
    return tiles.astype(np.float32)


def _diff_attention(ob, lam_p, subln_g, lam_init, batch, seq, w_cast, li):
    m = ob.shape[0]
    tq, tk = DIFF_TQ, DIFF_TK
    assert tq == tk and seq % tq == 0
    nq = seq // tq
    slopes, _ = _alibi_slopes()
    bias = jnp.asarray(_diff_bias_tiles(slopes))
    kern = functools.partial(_diff_kernel, lam_init=lam_init,
                             slopes=tuple(float(s) for s in slopes))
    w_in_spec, w_out_spec, w_shape = _cast_specs(w_cast, li, batch * nq, lambda b, i: b * nq + i)
    return pl.pallas_call(
        kern,
        grid=(batch, nq),
        in_specs=[
            pl.BlockSpec((tq, GROUP_WIDTH), lambda b, i: (b * nq + i, 0)),
            pl.BlockSpec((seq, GROUP_WIDTH), lambda b, i: (b, 1)),
            pl.BlockSpec((seq, GROUP_WIDTH), lambda b, i: (b, 2)),
            pl.BlockSpec((DIFF_HEADS, 2, tk, 2 * tq), lambda b, i: (0, 0, 0, 0)),
            pl.BlockSpec((4, HEAD_DIM), lambda b, i: (0, 0)),
            pl.BlockSpec((2 * HEAD_DIM, 1), lambda b, i: (0, 0)),
            w_in_spec,
        ],
        out_specs=[pl.BlockSpec((tq, GROUP_WIDTH), lambda b, i: (b * nq + i, 0)), w_out_spec],
        out_shape=[jax.ShapeDtypeStruct((m, GROUP_WIDTH), BF16), w_shape],
        scratch_shapes=[
            pltpu.VMEM((DIFF_HEADS, LANES + DIFF_ONES_ROWS, seq), BF16),
            pltpu.VMEM((DIFF_HEADS, LANES + DIFF_ONES_ROWS, 2 * tq), F32),
        ],
        compiler_params=_params(("parallel", "arbitrary")),
        name="diff_attn",
    )(ob, ob, ob, bias, lam_p, subln_g, w_cast)


def _dil_kernel(q_ref, k_ref, v_ref, bias_ref, wf_ref, o_ref, wb_ref, qf, kf, vf, m_s, l_s, o_s,
                *, seq):
    wb_ref[...] = wf_ref[...].astype(BF16)
    n = DIL_BLOCK
    qf[...] = q_ref[...].astype(F32)
    kf[...] = k_ref[...].astype(F32)
    vf[...] = v_ref[...].astype(F32)
    lane = lax.broadcasted_iota(jnp.int32, (n, LANES), 1)
    low = lane < HEAD_DIM

    for pi, (window, dil) in enumerate(DIL_PATTERNS):
        nblk = seq // (n * dil)
        for r in range(dil):
            for c in range(nblk):
                cur = pl.ds(r + dil * n * c, n, stride=dil)
                q2 = _stack_queries(qf[cur, :])
                if c > 0:
                    prev = pl.ds(r + dil * n * (c - 1), n, stride=dil)
                    kb = jnp.concatenate([kf[prev, :], kf[cur, :]], axis=0).astype(BF16)
                    vb = jnp.concatenate([vf[prev, :], vf[cur, :]], axis=0).astype(BF16)
                    bias = bias_ref[pi]
                else:
                    kb = kf[cur, :].astype(BF16)
                    vb = vf[cur, :].astype(BF16)
                    bias = bias_ref[pi, :, n:]
                s = _qk(q2, kb) + bias
                mx = jnp.max(s, axis=1, keepdims=True)
                p = jnp.exp2(s - mx).astype(BF16)
                vb1 = jnp.concatenate([vb, jnp.ones(vb.shape, BF16)], axis=1)
                ol = jnp.dot(p, vb1, preferred_element_type=F32)
                o, l = ol[:, :LANES], ol[:, LANES:]
                o_s[pi, cur, :] = jnp.where(low, o[:n], o[n:])
                m_s[pi, cur, :] = jnp.where(low, mx[:n], mx[n:])
                l_s[pi, cur, :] = jnp.where(low, l[:n], l[n:])

    m_all = jnp.maximum(jnp.maximum(m_s[0], m_s[1]), m_s[2])
    num = jnp.zeros((seq, LANES), F32)
    den = jnp.zeros((seq, LANES), F32)
    for pi in range(len(DIL_PATTERNS)):
        w = jnp.exp2(m_s[pi] - m_all)
        num = num + w * o_s[pi]
        den = den + w * l_s[pi]
    o_ref[...] = (num / den).astype(BF16)


def _dil_bias_tiles(slopes):
    n = DIL_BLOCK
    qi = (np.arange(2 * n) % n)[:, None]
    ki = np.arange(2 * n)[None, :]
    step = n + qi - ki
    valid = (step >= 0) & (step <= n)
    tiles = np.zeros((len(DIL_PATTERNS), DIL_HEADS // 2, 2 * n, 2 * n), np.float64)
    for pi, (_, dil) in enumerate(DIL_PATTERNS):
        for p in range(DIL_HEADS // 2):
            sl = np.where(np.arange(2 * n) < n, slopes[2 * p], slopes[2 * p + 1])[:, None]
            tiles[pi, p] = np.where(valid, -sl * LOG2E * (step * dil), MASKED)
    return tiles.astype(np.float32)


def _dil_attention(ob, batch, seq, w_cast, li):
    m = ob.shape[0]
    _, slopes = _alibi_slopes()
    bias = jnp.asarray(_dil_bias_tiles(slopes))
    npairs = DIL_HEADS // 2
    npat = len(DIL_PATTERNS)
    col0 = 3 * GROUP_WIDTH // LANES
    kern = functools.partial(_dil_kernel, seq=seq)
    w_in_spec, w_out_spec, w_shape = _cast_specs(w_cast, li, batch * npairs,
                                                 lambda b, p: b * npairs + p)
    return pl.pallas_call(
        kern,
        grid=(batch, npairs),
        in_specs=[
            pl.BlockSpec((seq, LANES), lambda b, p: (b, col0 + p)),
            pl.BlockSpec((seq, LANES), lambda b, p: (b, col0 + npairs + p)),
            pl.BlockSpec((seq, LANES), lambda b, p: (b, col0 + 2 * npairs + p)),
            pl.BlockSpec((npat, None, 2 * DIL_BLOCK, 2 * DIL_BLOCK), lambda b, p: (0, p, 0, 0)),
            w_in_spec,
        ],
        out_specs=[pl.BlockSpec((seq, LANES), lambda b, p: (b, p)), w_out_spec],
        out_shape=[jax.ShapeDtypeStruct((m, GROUP_WIDTH), BF16), w_shape],
        scratch_shapes=[
            pltpu.VMEM((seq, LANES), F32),
            pltpu.VMEM((seq, LANES), F32),
            pltpu.VMEM((seq, LANES), F32),
            pltpu.VMEM((npat, seq, LANES), F32),
            pltpu.VMEM((npat, seq, LANES), F32),
            pltpu.VMEM((npat, seq, LANES), F32),
        ],
        compiler_params=_params(("parallel", "parallel")),
        name="dil_attn",
    )(ob, ob, ob, bias, w_cast)


def _conv_kernel(a_ref, g_ref, ap_ref, gp_ref, w_ref, b_ref, n_ref, *rest, ncast):
    wf_refs, o_ref, wb_refs = rest[:ncast], rest[ncast], rest[ncast + 1:2 * ncast + 1]
    h_s, y_s = rest[2 * ncast + 1:]
    for wf_ref, wb_ref in zip(wf_refs, wb_refs):
        wb_ref[...] = wf_ref[...].astype(BF16)
    i = pl.program_id(1)
    tr, halo = CONV_ROWS, CONV_HALO
    hp = ap_ref[...] * jax.nn.sigmoid(gp_ref[...])
    h_s[0:halo, :] = jnp.where(i > 0, hp, 0.0)
    h_s[halo:, :] = a_ref[...] * jax.nn.sigmoid(g_ref[...])
    _conv_rows(h_s, w_ref, b_ref, n_ref, y_s, o_ref, 0, tr)


def _conv_rows(h_s, w_ref, b_ref, n_ref, y_s, o_ref, row0, nrows):
    rc, sub, last, halo = CONV_CHUNK, SUBLANES, CONV_WIDTH - 1, CONV_HALO
    for lt in range(GROUP_WIDTH // LANES):
        lanes = slice(lt * LANES, (lt + 1) * LANES)
        for c in range(nrows // rc):
            base = row0 + c * rc
            y = jnp.zeros((rc, LANES), F32)
            for r in range(sub):
                a_r = None
                for q in range((last - r) // sub + 1):
                    d = sub * q + r
                    start = base + halo - sub - sub * q
                    win = h_s[start:start + rc + sub, lanes]
                    term = w_ref[last - d, :, lanes] * win.reshape(rc // sub + 1, sub, LANES)
                    a_r = term if a_r is None else a_r + term
                y = y + a_r.reshape(rc + sub, LANES)[sub - r:sub - r + rc, :]
            y_s[c * rc:(c + 1) * rc, lanes] = y
    y = _rms(y_s[0:nrows, :] + b_ref[...], n_ref[...])
    o_ref[row0:row0 + nrows, :] = (y * jax.nn.sigmoid(y)).astype(BF16)


def _conv(of, w, b, g, batch, seq, casts):
    m = of.shape[0]
    tr, halo = CONV_ROWS, CONV_HALO
    nt = seq // tr
    per = tr // halo
    cur = lambda col: (lambda bb, i: (bb * nt + i, col))
    prev = lambda col: (lambda bb, i: (jnp.maximum((bb * nt + i) * per - 1, 0), col))
    cast_specs = [_cast_specs(wc, lc, batch * nt, lambda bb, i: bb * nt + i) for wc, lc in casts]
    return pl.pallas_call(
        functools.partial(_conv_kernel, ncast=len(casts)),
        grid=(batch, nt),
        in_specs=[
            pl.BlockSpec((tr, GROUP_WIDTH), cur(2)),
            pl.BlockSpec((tr, GROUP_WIDTH), cur(3)),
            pl.BlockSpec((halo, GROUP_WIDTH), prev(2)),
            pl.BlockSpec((halo, GROUP_WIDTH), prev(3)),
            pl.BlockSpec((CONV_WIDTH, SUBLANES, GROUP_WIDTH), lambda bb, i: (0, 0, 0)),
            pl.BlockSpec((1, GROUP_WIDTH), lambda bb, i: (0, 0)),
            pl.BlockSpec((1, GROUP_WIDTH), lambda bb, i: (0, 0)),
            *[cs[0] for cs in cast_specs],
        ],
        out_specs=[pl.BlockSpec((tr, GROUP_WIDTH), lambda bb, i: (bb * nt + i, 0)),
                   *[cs[1] for cs in cast_specs]],
        out_shape=[jax.ShapeDtypeStruct((m, GROUP_WIDTH), BF16), *[cs[2] for cs in cast_specs]],
        scratch_shapes=[pltpu.VMEM((halo + tr, GROUP_WIDTH), F32),
                        pltpu.VMEM((tr, GROUP_WIDTH), F32)],
        compiler_params=_params(("parallel", "parallel")),
        name="conformer_conv",
    )(of, of, of, of, jnp.broadcast_to(w[:, None, :], (CONV_WIDTH, SUBLANES, GROUP_WIDTH)), b, g,
      *[wc for wc, _ in casts])


def _out_kernel(a_ref, b_ref, c_ref, d_ref, w_ref, x_ref, g_ref, gn_ref, o_ref, h_ref):
    gw = GROUP_WIDTH
    for c in range(TM_OUT // OUT_ROW_CHUNK):
        rows = slice(c * OUT_ROW_CHUNK, (c + 1) * OUT_ROW_CHUNK)
        y = jnp.dot(a_ref[rows, :], w_ref[0:gw, :], preferred_element_type=F32)
        y = y + jnp.dot(b_ref[rows, :], w_ref[gw:2 * gw, :], preferred_element_type=F32)
        y = y + jnp.dot(c_ref[rows, :], w_ref[2 * gw:3 * gw, :], preferred_element_type=F32)
        y = y + jnp.dot(d_ref[rows, :], w_ref[3 * gw:4 * gw, :], preferred_element_type=F32)
        xn = x_ref[rows, :] + _rms(y, g_ref[...])
        o_ref[rows, :] = xn
        h_ref[rows, :] = _rms(xn, gn_ref[...]).astype(BF16)


def _out_proj(mix, w, x, g, g_next):
    m = x.shape[0]
    tm = TM_OUT
    mix_spec = pl.BlockSpec((tm, GROUP_WIDTH), lambda i: (i, 0))
    row_spec = pl.BlockSpec((tm, D_MODEL), lambda i: (i, 0))
    gain_spec = pl.BlockSpec((1, D_MODEL), lambda i: (0, 0))
    return pl.pallas_call(
        _out_kernel,
        grid=(m // tm,),
        in_specs=[mix_spec, mix_spec, mix_spec, mix_spec,
                  pl.BlockSpec((D_MODEL, D_MODEL), lambda i: (0, 0), pipeline_mode=pl.Buffered(1)),
                  row_spec, gain_spec, gain_spec],
        out_specs=[row_spec, row_spec],
        out_shape=[jax.ShapeDtypeStruct((m, D_MODEL), F32),
                   jax.ShapeDtypeStruct((m, D_MODEL), BF16)],
        compiler_params=_params(("parallel",)),
        name="out_proj",
    )(*mix, w, x, g, g_next)


def _ffn_kernel(x_ref, h_ref, w1_ref, w2_ref, gpost_ref, o_ref):
    j = pl.program_id(1)

    @pl.when(j == 0)
    def _():
        o_ref[...] = jnp.zeros(o_ref.shape, F32)

    f = jnp.dot(h_ref[...], w1_ref[...], preferred_element_type=F32)
    f = jnp.square(jnp.maximum(f, 0.0)).astype(BF16)
    for c in range(D_MODEL // FFN_OUT_CHUNK):
        cols = slice(c * FFN_OUT_CHUNK, (c + 1) * FFN_OUT_CHUNK)
        o_ref[:, cols] += jnp.dot(f, w2_ref[:, cols], preferred_element_type=F32)

    @pl.when(j == pl.num_programs(1) - 1)
    def _():
        o_ref[...] = x_ref[...] + _rms(o_ref[...], gpost_ref[...])


def _ffn(x, h, w1, w2, gpost):
    m = x.shape[0]
    tm, tf = TM_FFN, TF_FFN
    return pl.pallas_call(
        _ffn_kernel,
        grid=(m // tm, D_FF // tf),
        in_specs=[
            pl.BlockSpec((tm, D_MODEL), lambda i, j: (i, 0)),
            pl.BlockSpec((tm, D_MODEL), lambda i, j: (i, 0)),
            pl.BlockSpec((D_MODEL, tf), lambda i, j: (0, j)),
            pl.BlockSpec((tf, D_MODEL), lambda i, j: (j, 0)),
            pl.BlockSpec((1, D_MODEL), lambda i, j: (0, 0)),
        ],
        out_specs=pl.BlockSpec((tm, D_MODEL), lambda i, j: (i, 0)),
        out_shape=jax.ShapeDtypeStruct((m, D_MODEL), F32),
        compiler_params=_params(("parallel", "arbitrary"), VMEM_LIMIT_BIG),
        name="ffn",
    )(x, h, w1, w2, gpost)


def kernel(x, g_mix_pre, g_mix_post, w_in, gmlp_w, gmlp_b, diff_lam, diff_subln, conv_w, conv_b,
           conv_norm, w_out, g_ffn_pre, g_ffn_post, w_ff1, w_ff2):
    batch, seq, _ = x.shape
    depth = w_in.shape[0]
    xf = x.reshape(batch * seq, D_MODEL)
    row = lambda v: v.reshape(1, -1)
    w_in_li = w_in
    for li in range(depth):
        of, ob = _in_proj(xf, row(g_mix_pre[li]), w_in_li, li)

        w_t = gmlp_w[li].transpose(1, 0, 2).reshape(GMLP_CHUNK, GMLP_GROUPS * GMLP_CHUNK)
        bias_full = jnp.repeat(gmlp_b[li].T, GROUP_WIDTH // GMLP_GROUPS, axis=1)
        out_a = _gmlp(of, w_t, bias_full)

        lam_init = 0.8 - 0.6 * float(np.exp(-0.3 * li))
        out_b, w_ff1_b = _diff_attention(ob, diff_lam[li], diff_subln[li].reshape(-1, 1),
                                         lam_init, batch, seq, w_ff1, li)
        out_c, w_ff2_b = _dil_attention(ob, batch, seq, w_ff2, li)
        casts = [(w_out, li)] + ([(w_in, li + 1)] if li + 1 < depth else [])
        out_d, w_out_b, *w_next = _conv(of, conv_w[li], row(conv_b[li]), row(conv_norm[li]),
                                        batch, seq, casts)
        if w_next:
            w_in_li = w_next[0]

        xf, h_ffn = _out_proj((out_a, out_b, out_c, out_d), w_out_b, xf, row(g_mix_post[li]),
                              row(g_ffn_pre[li]))
        xf = _ffn(xf, h_ffn, w_ff1_b, w_ff2_b, row(g_ffn_post[li]))
    return xf.reshape(batch, seq, D_MODEL)
```

```python
import functools

import numpy as np
import jax
import jax.numpy as jnp
from jax import lax
from jax.experimental import pallas as pl
from jax.experimental.pallas import tpu as pltpu

F32 = jnp.float32
BF16 = jnp.bfloat16

D_MODEL = 2048
GROUP_WIDTH = 512
N_IN_SPLITS = 10
D_FF = 4 * D_MODEL
NORM_EPS = 1e-6
GMLP_CHUNK = 128
GMLP_GROUPS = 8
DIFF_HEADS = 4
DIL_HEADS = 8
HEAD_DIM = 64
DIL_PATTERNS = ((128, 1), (512, 4), (2048, 16))
DIL_BLOCK = 128
CONV_WIDTH = 31
N_ALIBI_HEADS = DIFF_HEADS + DIL_HEADS
ATTN_SCALE = HEAD_DIM ** -0.5
MASKED = -1e30
LOG2E = 1.4426950408889634
DIFF_ONES_ROWS = 16
DIFF_HEAD_GROUP = 4

LANES = 128
VMEM_LIMIT = 48 * 1024 * 1024
VMEM_LIMIT_BIG = 60 * 1024 * 1024

TM_PROJ = 1024
TN_PROJ = 1024
TM_OUT = 512
OUT_ROW_CHUNK = 256
TM_FFN = 1024
TF_FFN = 1024
FFN_MID_CHUNK = 512
FFN_OUT_CHUNK = 512
GMLP_ROWS = 1024
DIFF_TQ = 256
DIFF_TK = 256
CONV_ROWS = 512
CONV_HALO = 32
CONV_CHUNK = 128
SUBLANES = 8


def _alibi_slopes():
    i = np.arange(1, N_ALIBI_HEADS + 1, dtype=np.float64)
    s = 2.0 ** (-8.0 * i / N_ALIBI_HEADS)
    diff_idx = np.arange(0, N_ALIBI_HEADS, 3)
    dil_idx = np.array([j for j in range(N_ALIBI_HEADS) if j % 3 != 0])
    return s[diff_idx], s[dil_idx]


def _rms(x, g):
    return x * lax.rsqrt(jnp.mean(x * x, axis=-1, keepdims=True) + NORM_EPS) * g


def _params(sem, limit=VMEM_LIMIT):
    return pltpu.CompilerParams(dimension_semantics=sem, vmem_limit_bytes=limit)


def _proj_kernel(x_ref, g_ref, w_ref, of_ref, ob_ref, h_ref, *, n_f32_tiles):
    j = pl.program_id(1)

    @pl.when(j == 0)
    def _():
        h_ref[...] = _rms(x_ref[...], g_ref[...]).astype(BF16)

    def tile(write):
        for c in range(w_ref.shape[1] // GROUP_WIDTH):
            cols = slice(c * GROUP_WIDTH, (c + 1) * GROUP_WIDTH)
            write(cols, jnp.dot(h_ref[...], w_ref[:, cols].astype(BF16),
                                preferred_element_type=F32))

    @pl.when(j < n_f32_tiles)
    def _():
        def write(cols, y):
            of_ref[:, cols] = y
        tile(write)

    @pl.when(j >= n_f32_tiles)
    def _():
        def write(cols, y):
            ob_ref[:, cols] = y.astype(BF16)
        tile(write)


def _in_proj(x, g, w, li):
    m = x.shape[0]
    tm = TM_PROJ
    tn = TN_PROJ
    ntile = N_IN_SPLITS * GROUP_WIDTH // tn
    kf = 2 * GROUP_WIDTH // tn
    col = lambda j: jnp.where(j < kf, j, jnp.where(j < 2 * kf, j + ntile - 2 * kf, j - kf))
    if w.ndim == 3:
        w_spec = pl.BlockSpec((None, D_MODEL, tn), lambda i, j: (li, 0, col(j)))
    else:
        w_spec = pl.BlockSpec((D_MODEL, tn), lambda i, j: (0, col(j)))
    return pl.pallas_call(
        functools.partial(_proj_kernel, n_f32_tiles=2 * kf),
        grid=(m // tm, ntile),
        in_specs=[
            pl.BlockSpec((tm, D_MODEL), lambda i, j: (i, 0)),
            pl.BlockSpec((1, D_MODEL), lambda i, j: (0, 0)),
            w_spec,
        ],
        out_specs=[
            pl.BlockSpec((tm, tn), lambda i, j: (i, jnp.minimum(j, 2 * kf - 1))),
            pl.BlockSpec((tm, tn), lambda i, j: (i, jnp.maximum(j - 2 * kf, 0))),
        ],
        out_shape=[
            jax.ShapeDtypeStruct((m, 4 * GROUP_WIDTH), F32),
            jax.ShapeDtypeStruct((m, 6 * GROUP_WIDTH), BF16),
        ],
        scratch_shapes=[pltpu.VMEM((tm, D_MODEL), BF16)],
        compiler_params=_params(("parallel", "arbitrary"), VMEM_LIMIT_BIG),
        name="in_proj",
    )(x, g, w)


def _gelu(x):
    return jax.nn.gelu(x, approximate=True)


def _gmlp_kernel(u_ref, v_ref, w_ref, b_ref, o_ref):
    c = GMLP_CHUNK
    t_idx = lax.broadcasted_iota(jnp.int32, w_ref.shape, 0)
    s_idx = lax.broadcasted_iota(jnp.int32, w_ref.shape, 1) % c
    w = jnp.where(s_idx <= t_idx, w_ref[...], 0.0).astype(BF16)
    bias = b_ref[...]
    lane = lax.broadcasted_iota(jnp.int32, (c, LANES), 1)
    low = lane < HEAD_DIM
    for ci in range(GMLP_ROWS // c):
        rows = slice(ci * c, (ci + 1) * c)
        u = _gelu(u_ref[rows, :])
        v = _gelu(v_ref[rows, :])
        mu = jnp.mean(v, axis=-1, keepdims=True)
        vc = v - mu
        v = vc * lax.rsqrt(jnp.mean(vc * vc, axis=-1, keepdims=True) + NORM_EPS)
        zs = []
        for p in range(GMLP_GROUPS // 2):
            vp = v[:, p * LANES:(p + 1) * LANES]
            rhs = jnp.concatenate([jnp.where(low, vp, 0.0), jnp.where(low, 0.0, vp)], axis=0)
            zs.append(jnp.dot(w[:, p * 2 * c:(p + 1) * 2 * c], rhs.astype(BF16),
                              preferred_element_type=F32))
        z = jnp.concatenate(zs, axis=1) + bias
        o_ref[rows, :] = (u * z).astype(BF16)


def _gmlp(of, w_t, bias_full):
    m = of.shape[0]
    tr = GMLP_ROWS
    return pl.pallas_call(
        _gmlp_kernel,
        grid=(m // tr,),
        in_specs=[
            pl.BlockSpec((tr, GROUP_WIDTH), lambda i: (i, 0)),
            pl.BlockSpec((tr, GROUP_WIDTH), lambda i: (i, 1)),
            pl.BlockSpec((GMLP_CHUNK, GMLP_GROUPS * GMLP_CHUNK), lambda i: (0, 0)),
            pl.BlockSpec((GMLP_CHUNK, GROUP_WIDTH), lambda i: (0, 0)),
        ],
        out_specs=pl.BlockSpec((tr, GROUP_WIDTH), lambda i: (i, 0)),
        out_shape=jax.ShapeDtypeStruct((m, GROUP_WIDTH), BF16),
        compiler_params=_params(("parallel",)),
        name="gmlp",
    )(of, of, w_t, bias_full)


def _stack_queries(q):
    lane = lax.broadcasted_iota(jnp.int32, q.shape, 1)
    low = lane < HEAD_DIM
    zero = jnp.zeros_like(q)
    q2 = jnp.concatenate([jnp.where(low, q, zero), jnp.where(low, zero, q)], axis=0)
    return (q2.astype(F32) * (ATTN_SCALE * LOG2E)).astype(BF16)


def _qk(q2, k):
    return lax.dot_general(q2, k, (((1,), (1,)), ((), ())), preferred_element_type=F32)


def _cast_specs(w, li, nsteps, step_of):
    _, rows, cols = w.shape
    slab = rows // nsteps
    assert slab * nsteps == rows and slab % 16 == 0
    in_spec = pl.BlockSpec((None, slab, cols), lambda *ids: (li, step_of(*ids), 0))
    out_spec = pl.BlockSpec((slab, cols), lambda *ids: (step_of(*ids), 0))
    return in_spec, out_spec, jax.ShapeDtypeStruct((rows, cols), BF16)


def _diff_kernel(q_ref, k_ref, v_ref, bias_ref, lam_ref, g_ref, wf_ref, o_ref, wb_ref,
                 vt_s, acc_s, *, lam_init, slopes):
    wb_ref[...] = wf_ref[...].astype(BF16)
    qi = pl.program_id(1)
    tq, tk = DIFF_TQ, DIFF_TK
    seq = k_ref.shape[0]
    heads = range(DIFF_HEADS)
    hcols = lambda h: slice(h * LANES, (h + 1) * LANES)

    @pl.when(qi == 0)
    def _():
        for h in heads:
            for c in range(seq // tk):
                rows = slice(c * tk, (c + 1) * tk)
                vt_s[h, 0:LANES, rows] = v_ref[rows, hcols(h)].astype(F32).T.astype(BF16)
            vt_s[h, LANES:, :] = jnp.ones((DIFF_ONES_ROWS, seq), BF16)

    q2 = [_stack_queries(q_ref[:, hcols(h)]) for h in heads]
    acc_s[...] = jnp.zeros(acc_s.shape, F32)

    def step_group(hs, start, which, dist, m_prev):
        if which == 0:
            bias = {h: jnp.concatenate([bias_ref[h, 0, :, 0:LANES]] * (2 * tq // LANES), axis=1)
                    for h in hs}
        else:
            bias = {h: bias_ref[h, 1] for h in hs}
        sts = {h: _qk(k_ref[pl.ds(start, tk), hcols(h)], q2[h]) + bias[h]
               for h in hs}
        shifts = {h: (-slopes[h] * LOG2E) * dist for h in hs}
        m_next = {h: jnp.maximum(m_prev[h], jnp.max(sts[h], axis=0, keepdims=True) + shifts[h])
                  for h in hs}
        ps = {h: jnp.exp2(sts[h] - (m_next[h] - shifts[h])).astype(BF16) for h in hs}
        alphas = {h: jnp.exp2(m_prev[h] - m_next[h]) for h in hs}
        pvs = {h: jnp.dot(vt_s[h, :, pl.ds(start, tk)], ps[h], preferred_element_type=F32)
               for h in hs}
        for h in hs:
            acc_s[h] = alphas[h] * acc_s[h] + pvs[h]
        return m_next

    def step(start, which, dist, m_prev):
        m_next = {}
        for g in range(0, DIFF_HEADS, DIFF_HEAD_GROUP):
            m_next.update(step_group(range(g, g + DIFF_HEAD_GROUP), start, which, dist, m_prev))
        return tuple(m_next[h] for h in heads)

    def body(j, m_prev):
        return step(pl.multiple_of(j * tk, tk), 0, ((qi - j) * tq).astype(F32), m_prev)

    m0 = jnp.full((1, 2 * tq), MASKED, F32)
    m_prev = lax.fori_loop(0, qi, body, tuple(m0 for _ in heads))
    step(pl.multiple_of(qi * tq, tq), 1, 0.0, m_prev)

    lp = lam_ref[...]
    lam = (jnp.exp(jnp.sum(lp[0:1] * lp[1:2], axis=1, keepdims=True))
           - jnp.exp(jnp.sum(lp[2:3] * lp[3:4], axis=1, keepdims=True)) + lam_init)
    for h in heads:
        o = acc_s[h, 0:LANES, :] / acc_s[h, LANES:LANES + 1, :]
        o = o[:, :tq] - lam * o[:, tq:]
        y = o * lax.rsqrt(jnp.mean(o * o, axis=0, keepdims=True) + NORM_EPS) * g_ref[...]
        o_ref[:, hcols(h)] = (y * (1.0 - lam_init)).T.astype(BF16)


def _diff_bias_tiles(slopes):
    tq, tk = DIFF_TQ, DIFF_TK
    rq = (np.arange(2 * tq) % tq)[None, :].astype(np.float64)
    c = np.arange(tk)[:, None].astype(np.float64)
    tiles = np.zeros((DIFF_HEADS, 2, tk, 2 * tq), np.float64)
    for h, sl in enumerate(slopes):
        tiles[h, 0] = sl * LOG2E * c + 0.0 * rq
        tiles[h, 1] = np.where(rq - c >= 0, sl * LOG2E * c, MASKED)
    return tiles.astype(np.float32)


def _diff_attention(ob, lam_p, subln_g, lam_init, batch, seq, w_cast, li):
    m = ob.shape[0]
    tq, tk = DIFF_TQ, DIFF_TK
    assert tq == tk and seq % tq == 0
    nq = seq // tq
    slopes, _ = _alibi_slopes()
    bias = jnp.asarray(_diff_bias_tiles(slopes))
    kern = functools.partial(_diff_kernel, lam_init=lam_init,
                             slopes=tuple(float(s) for s in slopes))
    w_in_spec, w_out_spec, w_shape = _cast_specs(w_cast, li, batch * nq, lambda b, i: b * nq + i)
    return pl.pallas_call(
        kern,
        grid=(batch, nq),
        in_specs=[
            pl.BlockSpec((tq, GROUP_WIDTH), lambda b, i: (b * nq + i, 0)),
            pl.BlockSpec((seq, GROUP_WIDTH), lambda b, i: (b, 1)),
            pl.BlockSpec((seq, GROUP_WIDTH), lambda b, i: (b, 2)),
            pl.BlockSpec((DIFF_HEADS, 2, tk, 2 * tq), lambda b, i: (0, 0, 0, 0)),
            pl.BlockSpec((4, HEAD_DIM), lambda b, i: (0, 0)),
            pl.BlockSpec((2 * HEAD_DIM, 1), lambda b, i: (0, 0)),
            w_in_spec,
        ],
        out_specs=[pl.BlockSpec((tq, GROUP_WIDTH), lambda b, i: (b * nq + i, 0)), w_out_spec],
        out_shape=[jax.ShapeDtypeStruct((m, GROUP_WIDTH), BF16), w_shape],
        scratch_shapes=[
            pltpu.VMEM((DIFF_HEADS, LANES + DIFF_ONES_ROWS, seq), BF16),
            pltpu.VMEM((DIFF_HEADS, LANES + DIFF_ONES_ROWS, 2 * tq), F32),
        ],
        compiler_params=_params(("parallel", "arbitrary")),
        name="diff_attn",
    )(ob, ob, ob, bias, lam_p, subln_g, w_cast)


def _dil_kernel(q_ref, k_ref, v_ref, bias_ref, wf_ref, o_ref, wb_ref, qf, kf, vf, m_s, l_s, o_s,
                *, seq):
    wb_ref[...] = wf_ref[...].astype(BF16)
    n = DIL_BLOCK
    qf[...] = q_ref[...].astype(F32)
    kf[...] = k_ref[...].astype(F32)
    vf[...] = v_ref[...].astype(F32)
    lane = lax.broadcasted_iota(jnp.int32, (n, LANES), 1)
    low = lane < HEAD_DIM

    for pi, (window, dil) in enumerate(DIL_PATTERNS):
        nblk = seq // (n * dil)
        for r in range(dil):
            for c in range(nblk):
                cur = pl.ds(r + dil * n * c, n, stride=dil)
                q2 = _stack_queries(qf[cur, :])
                if c > 0:
                    prev = pl.ds(r + dil * n * (c - 1), n, stride=dil)
                    kb = jnp.concatenate([kf[prev, :], kf[cur, :]], axis=0).astype(BF16)
                    vb = jnp.concatenate([vf[prev, :], vf[cur, :]], axis=0).astype(BF16)
                    bias = bias_ref[pi]
                else:
                    kb = kf[cur, :].astype(BF16)
                    vb = vf[cur, :].astype(BF16)
                    bias = bias_ref[pi, :, n:]
                s = _qk(q2, kb) + bias
                mx = jnp.max(s, axis=1, keepdims=True)
                p = jnp.exp2(s - mx).astype(BF16)
                vb1 = jnp.concatenate([vb, jnp.ones(vb.shape, BF16)], axis=1)
                ol = jnp.dot(p, vb1, preferred_element_type=F32)
                o, l = ol[:, :LANES], ol[:, LANES:]
                o_s[pi, cur, :] = jnp.where(low, o[:n], o[n:])
                m_s[pi, cur, :] = jnp.where(low, mx[:n], mx[n:])
                l_s[pi, cur, :] = jnp.where(low, l[:n], l[n:])

    m_all = jnp.maximum(jnp.maximum(m_s[0], m_s[1]), m_s[2])
    num = jnp.zeros((seq, LANES), F32)
    den = jnp.zeros((seq, LANES), F32)
    for pi in range(len(DIL_PATTERNS)):
        w = jnp.exp2(m_s[pi] - m_all)
        num = num + w * o_s[pi]
        den = den + w * l_s[pi]
    o_ref[...] = (num / den).astype(BF16)


def _dil_bias_tiles(slopes):
    n = DIL_BLOCK
    qi = (np.arange(2 * n) % n)[:, None]
    ki = np.arange(2 * n)[None, :]
    step = n + qi - ki
    valid = (step >= 0) & (step <= n)
    tiles = np.zeros((len(DIL_PATTERNS), DIL_HEADS // 2, 2 * n, 2 * n), np.float64)
    for pi, (_, dil) in enumerate(DIL_PATTERNS):
        for p in range(DIL_HEADS // 2):
            sl = np.where(np.arange(2 * n) < n, slopes[2 * p], slopes[2 * p + 1])[:, None]
            tiles[pi, p] = np.where(valid, -sl * LOG2E * (step * dil), MASKED)
    return tiles.astype(np.float32)


def _dil_attention(ob, batch, seq, w_cast, li):
    m = ob.shape[0]
    _, slopes = _alibi_slopes()
    bias = jnp.asarray(_dil_bias_tiles(slopes))
    npairs = DIL_HEADS // 2
    npat = len(DIL_PATTERNS)
    col0 = 3 * GROUP_WIDTH // LANES
    kern = functools.partial(_dil_kernel, seq=seq)
    w_in_spec, w_out_spec, w_shape = _cast_specs(w_cast, li, batch * npairs,
                                                 lambda b, p: b * npairs + p)
    return pl.pallas_call(
        kern,
        grid=(batch, npairs),
        in_specs=[
            pl.BlockSpec((seq, LANES), lambda b, p: (b, col0 + p)),
            pl.BlockSpec((seq, LANES), lambda b, p: (b, col0 + npairs + p)),
            pl.BlockSpec((seq, LANES), lambda b, p: (b, col0 + 2 * npairs + p)),
            pl.BlockSpec((npat, None, 2 * DIL_BLOCK, 2 * DIL_BLOCK), lambda b, p: (0, p, 0, 0)),
            w_in_spec,
        ],
        out_specs=[pl.BlockSpec((seq, LANES), lambda b, p: (b, p)), w_out_spec],
        out_shape=[jax.ShapeDtypeStruct((m, GROUP_WIDTH), BF16), w_shape],
        scratch_shapes=[
            pltpu.VMEM((seq, LANES), F32),
            pltpu.VMEM((seq, LANES), F32),
            pltpu.VMEM((seq, LANES), F32),
            pltpu.VMEM((npat, seq, LANES), F32),
            pltpu.VMEM((npat, seq, LANES), F32),
            pltpu.VMEM((npat, seq, LANES), F32),
        ],
        compiler_params=_params(("parallel", "parallel")),
        name="dil_attn",
    )(ob, ob, ob, bias, w_cast)


def _conv_kernel(a_ref, g_ref, ap_ref, gp_ref, w_ref, b_ref, n_ref, *rest, ncast):
    wf_refs, o_ref, wb_refs = rest[:ncast], rest[ncast], rest[ncast + 1:2 * ncast + 1]
    h_s, y_s = rest[2 * ncast + 1:]
    for wf_ref, wb_ref in zip(wf_refs, wb_refs):
        wb_ref[...] = wf_ref[...].astype(BF16)
    i = pl.program_id(1)
    tr, halo = CONV_ROWS, CONV_HALO
    hp = ap_ref[...] * jax.nn.sigmoid(gp_ref[...])
    h_s[0:halo, :] = jnp.where(i > 0, hp, 0.0)
    h_s[halo:, :] = a_ref[...] * jax.nn.sigmoid(g_ref[...])
    _conv_rows(h_s, w_ref, b_ref, n_ref, y_s, o_ref, 0, tr)


def _conv_rows(h_s, w_ref, b_ref, n_ref, y_s, o_ref, row0, nrows):
    rc, sub, last, halo = CONV_CHUNK, SUBLANES, CONV_WIDTH - 1, CONV_HALO
    for lt in range(GROUP_WIDTH // LANES):
        lanes = slice(lt * LANES, (lt + 1) * LANES)
        for c in range(nrows // rc):
            base = row0 + c * rc
            y = jnp.zeros((rc, LANES), F32)
            for r in range(sub):
                a_r = None
                for q in range((last - r) // sub + 1):
                    d = sub * q + r
                    start = base + halo - sub - sub * q
                    win = h_s[start:start + rc + sub, lanes]
                    term = w_ref[last - d, :, lanes] * win.reshape(rc // sub + 1, sub, LANES)
                    a_r = term if a_r is None else a_r + term
                y = y + a_r.reshape(rc + sub, LANES)[sub - r:sub - r + rc, :]
            y_s[c * rc:(c + 1) * rc, lanes] = y
    y = _rms(y_s[0:nrows, :] + b_ref[...], n_ref[...])
    o_ref[row0:row0 + nrows, :] = (y * jax.nn.sigmoid(y)).astype(BF16)


def _conv(of, w, b, g, batch, seq, casts):
    m = of.shape[0]
    tr, halo = CONV_ROWS, CONV_HALO
    nt = seq // tr
    per = tr // halo
    cur = lambda col: (lambda bb, i: (bb * nt + i, col))
    prev = lambda col: (lambda bb, i: (jnp.maximum((bb * nt + i) * per - 1, 0), col))
    cast_specs = [_cast_specs(wc, lc, batch * nt, lambda bb, i: bb * nt + i) for wc, lc in casts]
    return pl.pallas_call(
        functools.partial(_conv_kernel, ncast=len(casts)),
        grid=(batch, nt),
        in_specs=[
            pl.BlockSpec((tr, GROUP_WIDTH), cur(2)),
            pl.BlockSpec((tr, GROUP_WIDTH), cur(3)),
            pl.BlockSpec((halo, GROUP_WIDTH), prev(2)),
            pl.BlockSpec((halo, GROUP_WIDTH), prev(3)),
            pl.BlockSpec((CONV_WIDTH, SUBLANES, GROUP_WIDTH), lambda bb, i: (0, 0, 0)),
            pl.BlockSpec((1, GROUP_WIDTH), lambda bb, i: (0, 0)),
            pl.BlockSpec((1, GROUP_WIDTH), lambda bb, i: (0, 0)),
            *[cs[0] for cs in cast_specs],
        ],
        out_specs=[pl.BlockSpec((tr, GROUP_WIDTH), lambda bb, i: (bb * nt + i, 0)),
                   *[cs[1] for cs in cast_specs]],
        out_shape=[jax.ShapeDtypeStruct((m, GROUP_WIDTH), BF16), *[cs[2] for cs in cast_specs]],
        scratch_shapes=[pltpu.VMEM((halo + tr, GROUP_WIDTH), F32),
                        pltpu.VMEM((tr, GROUP_WIDTH), F32)],
        compiler_params=_params(("parallel", "parallel")),
        name="conformer_conv",
    )(of, of, of, of, jnp.broadcast_to(w[:, None, :], (CONV_WIDTH, SUBLANES, GROUP_WIDTH)), b, g,
      *[wc for wc, _ in casts])


def _out_kernel(a_ref, b_ref, c_ref, d_ref, w_ref, x_ref, g_ref, gn_ref, o_ref, h_ref):
    gw = GROUP_WIDTH
    for c in range(TM_OUT // OUT_ROW_CHUNK):
        rows = slice(c * OUT_ROW_CHUNK, (c + 1) * OUT_ROW_CHUNK)
        y = jnp.dot(a_ref[rows, :], w_ref[0:gw, :], preferred_element_type=F32)
        y = y + jnp.dot(b_ref[rows, :], w_ref[gw:2 * gw, :], preferred_element_type=F32)
        y = y + jnp.dot(c_ref[rows, :], w_ref[2 * gw:3 * gw, :], preferred_element_type=F32)
        y = y + jnp.dot(d_ref[rows, :], w_ref[3 * gw:4 * gw, :], preferred_element_type=F32)
        xn = x_ref[rows, :] + _rms(y, g_ref[...])
        o_ref[rows, :] = xn
        h_ref[rows, :] = _rms(xn, gn_ref[...]).astype(BF16)


def _out_proj(mix, w, x, g, g_next):
    m = x.shape[0]
    tm = TM_OUT
    mix_spec = pl.BlockSpec((tm, GROUP_WIDTH), lambda i: (i, 0))
    row_spec = pl.BlockSpec((tm, D_MODEL), lambda i: (i, 0))
    gain_spec = pl.BlockSpec((1, D_MODEL), lambda i: (0, 0))
    return pl.pallas_call(
        _out_kernel,
        grid=(m // tm,),
        in_specs=[mix_spec, mix_spec, mix_spec, mix_spec,
                  pl.BlockSpec((D_MODEL, D_MODEL), lambda i: (0, 0), pipeline_mode=pl.Buffered(1)),
                  row_spec, gain_spec, gain_spec],
        out_specs=[row_spec, row_spec],
        out_shape=[jax.ShapeDtypeStruct((m, D_MODEL), F32),
                   jax.ShapeDtypeStruct((m, D_MODEL), BF16)],
        compiler_params=_params(("parallel",)),
        name="out_proj",
    )(*mix, w, x, g, g_next)


def _ffn_kernel(x_hbm, h_ref, w1_ref, w2_ref, gpost_ref, o_ref, x_s, x_sem):
    i, j = pl.program_id(0), pl.program_id(1)
    x_copy = pltpu.make_async_copy(x_hbm.at[pl.ds(i * TM_FFN, TM_FFN), :], x_s, x_sem)

    @pl.when(j == 0)
    def _():
        x_copy.start()
        o_ref[...] = jnp.zeros(o_ref.shape, F32)

    for cf in range(TF_FFN // FFN_MID_CHUNK):
        mid = slice(cf * FFN_MID_CHUNK, (cf + 1) * FFN_MID_CHUNK)
        f = jnp.dot(h_ref[...], w1_ref[:, mid], preferred_element_type=F32)
        f = jnp.square(jnp.maximum(f, 0.0)).astype(BF16)
        for c in range(D_MODEL // FFN_OUT_CHUNK):
            cols = slice(c * FFN_OUT_CHUNK, (c + 1) * FFN_OUT_CHUNK)
            o_ref[:, cols] += jnp.dot(f, w2_ref[mid, cols], preferred_element_type=F32)

    @pl.when(j == pl.num_programs(1) - 1)
    def _():
        x_copy.wait()
        o_ref[...] = x_s[...] + _rms(o_ref[...], gpost_ref[...])


def _ffn(x, h, w1, w2, gpost):
    m = x.shape[0]
    tm, tf = TM_FFN, TF_FFN
    return pl.pallas_call(
        _ffn_kernel,
        grid=(m // tm, D_FF // tf),
        in_specs=[
            pl.BlockSpec(memory_space=pl.ANY),
            pl.BlockSpec((tm, D_MODEL), lambda i, j: (i, 0)),
            pl.BlockSpec((D_MODEL, tf), lambda i, j: (0, j)),
            pl.BlockSpec((tf, D_MODEL), lambda i, j: (j, 0)),
            pl.BlockSpec((1, D_MODEL), lambda i, j: (0, 0)),
        ],
        out_specs=pl.BlockSpec((tm, D_MODEL), lambda i, j: (i, 0)),
        out_shape=jax.ShapeDtypeStruct((m, D_MODEL), F32),
        scratch_shapes=[pltpu.VMEM((tm, D_MODEL), F32), pltpu.SemaphoreType.DMA(())],
        compiler_params=_params(("arbitrary", "arbitrary"), VMEM_LIMIT_BIG),
        name="ffn",
    )(x, h, w1, w2, gpost)


def kernel(x, g_mix_pre, g_mix_post, w_in, gmlp_w, gmlp_b, diff_lam, diff_subln, conv_w, conv_b,
           conv_norm, w_out, g_ffn_pre, g_ffn_post, w_ff1, w_ff2):
    batch, seq, _ = x.shape
    depth = w_in.shape[0]
    xf = x.reshape(batch * seq, D_MODEL)
    row = lambda v: v.reshape(1, -1)
    w_in_li = w_in
    for li in range(depth):
        of, ob = _in_proj(xf, row(g_mix_pre[li]), w_in_li, li)

        w_t = gmlp_w[li].transpose(1, 0, 2).reshape(GMLP_CHUNK, GMLP_GROUPS * GMLP_CHUNK)
        bias_full = jnp.repeat(gmlp_b[li].T, GROUP_WIDTH // GMLP_GROUPS, axis=1)
        out_a = _gmlp(of, w_t, bias_full)

        lam_init = 0.8 - 0.6 * float(np.exp(-0.3 * li))
        out_b, w_ff1_b = _diff_attention(ob, diff_lam[li], diff_subln[li].reshape(-1, 1),
                                         lam_init, batch, seq, w_ff1, li)
        out_c, w_ff2_b = _dil_attention(ob, batch, seq, w_ff2, li)
        casts = [(w_out, li)] + ([(w_in, li + 1)] if li + 1 < depth else [])
        out_d, w_out_b, *w_next = _conv(of, conv_w[li], row(conv_b[li]), row(conv_norm[li]),
                                        batch, seq, casts)
        if w_next:
            w_in_li = w_next[0]

        xf, h_ffn = _out_proj((out_a, out_b, out_c, out_d), w_out_b, xf, row(g_mix_post[li]),
                              row(g_ffn_pre[li]))
        xf = _ffn(xf, h_ffn, w_ff1_b, w_ff2_b, row(g_ffn_post[li]))
    return xf.reshape(batch, seq, D_MODEL)
```

```python
import functools

import numpy as np
import jax
import jax.numpy as jnp
from jax import lax
from jax.experimental import pallas as pl
from jax.experimental.pallas import tpu as pltpu

F32 = jnp.float32
BF16 = jnp.bfloat16

D_MODEL = 2048
GROUP_WIDTH = 512
N_IN_SPLITS = 10
D_FF = 4 * D_MODEL
NORM_EPS = 1e-6
GMLP_CHUNK = 128
GMLP_GROUPS = 8
DIFF_HEADS = 4
DIL_HEADS = 8
HEAD_DIM = 64
DIL_PATTERNS = ((128, 1), (512, 4), (2048, 16))
DIL_BLOCK = 128
CONV_WIDTH = 31
N_ALIBI_HEADS = DIFF_HEADS + DIL_HEADS
ATTN_SCALE = HEAD_DIM ** -0.5
MASKED = -1e30
LOG2E = 1.4426950408889634
DIFF_ONES_ROWS = 16
DIFF_HEAD_GROUP = 4

LANES = 128
VMEM_LIMIT = 48 * 1024 * 1024
VMEM_LIMIT_BIG = 60 * 1024 * 1024

TM_PROJ = 1024
TM_OUT = 512
OUT_ROW_CHUNK = 256
TM_FFN = 1024
TF_FFN = 1024
FFN_MID_CHUNK = 512
FFN_OUT_CHUNK = 512
GMLP_ROWS = 1024
DIFF_TQ = 256
DIFF_TK = 256
CONV_ROWS = 512
CONV_HALO = 32
CONV_CHUNK = 128
SUBLANES = 8


def _alibi_slopes():
    i = np.arange(1, N_ALIBI_HEADS + 1, dtype=np.float64)
    s = 2.0 ** (-8.0 * i / N_ALIBI_HEADS)
    diff_idx = np.arange(0, N_ALIBI_HEADS, 3)
    dil_idx = np.array([j for j in range(N_ALIBI_HEADS) if j % 3 != 0])
    return s[diff_idx], s[dil_idx]


def _rms(x, g):
    return x * lax.rsqrt(jnp.mean(x * x, axis=-1, keepdims=True) + NORM_EPS) * g


def _params(sem, limit=VMEM_LIMIT):
    return pltpu.CompilerParams(dimension_semantics=sem, vmem_limit_bytes=limit)


def _proj_kernel(x_hbm, g_ref, w_ref, of_ref, ob_ref, h_s, x_s, x_sem):
    i, j = pl.program_id(0), pl.program_id(1)
    nf, nb = 2 * GROUP_WIDTH, 3 * GROUP_WIDTH

    def x_copy(block):
        return pltpu.make_async_copy(x_hbm.at[pl.ds(block * TM_PROJ, TM_PROJ), :], x_s, x_sem)

    def project(col0, ncols, write):
        for c in range(ncols // GROUP_WIDTH):
            cols = slice(col0 + c * GROUP_WIDTH, col0 + (c + 1) * GROUP_WIDTH)
            write(slice(c * GROUP_WIDTH, (c + 1) * GROUP_WIDTH),
                  jnp.dot(h_s[...], w_ref[:, cols], preferred_element_type=F32))

    def write_f32(cols, y):
        of_ref[:, cols] = y

    def write_bf16(cols, y):
        ob_ref[:, cols] = y.astype(BF16)

    @pl.when(j == 0)
    def _():
        @pl.when(i == 0)
        def _():
            x_copy(0).start()
        x_copy(i).wait()
        h_s[...] = _rms(x_s[...], g_ref[...]).astype(BF16)

        @pl.when(i + 1 < pl.num_programs(0))
        def _():
            x_copy(i + 1).start()
        project(0, nf, write_f32)
        project(nf, nb, write_bf16)

    @pl.when(j == 1)
    def _():
        project(0, nb, write_bf16)
        project(nb, nf, write_f32)


def _in_proj(x, g, w):
    m = x.shape[0]
    tm = TM_PROJ
    half = N_IN_SPLITS * GROUP_WIDTH // 2
    return pl.pallas_call(
        _proj_kernel,
        grid=(m // tm, 2),
        in_specs=[
            pl.BlockSpec(memory_space=pl.ANY),
            pl.BlockSpec((1, D_MODEL), lambda i, j: (0, 0)),
            pl.BlockSpec((D_MODEL, half), lambda i, j: (0, j)),
        ],
        out_specs=[
            pl.BlockSpec((tm, 2 * GROUP_WIDTH), lambda i, j: (i, j)),
            pl.BlockSpec((tm, 3 * GROUP_WIDTH), lambda i, j: (i, j)),
        ],
        out_shape=[
            jax.ShapeDtypeStruct((m, 4 * GROUP_WIDTH), F32),
            jax.ShapeDtypeStruct((m, 6 * GROUP_WIDTH), BF16),
        ],
        scratch_shapes=[pltpu.VMEM((tm, D_MODEL), BF16), pltpu.VMEM((tm, D_MODEL), F32),
                        pltpu.SemaphoreType.DMA(())],
        compiler_params=_params(("arbitrary", "arbitrary"), VMEM_LIMIT_BIG),
        name="in_proj",
    )(x, g, w)


def _gelu(x):
    return jax.nn.gelu(x, approximate=True)


def _gmlp_kernel(u_ref, v_ref, w_ref, b_ref, o_ref):
    c = GMLP_CHUNK
    t_idx = lax.broadcasted_iota(jnp.int32, w_ref.shape, 0)
    s_idx = lax.broadcasted_iota(jnp.int32, w_ref.shape, 1) % c
    w = jnp.where(s_idx <= t_idx, w_ref[...], 0.0).astype(BF16)
    bias = b_ref[...]
    lane = lax.broadcasted_iota(jnp.int32, (c, LANES), 1)
    low = lane < HEAD_DIM
    for ci in range(GMLP_ROWS // c):
        rows = slice(ci * c, (ci + 1) * c)
        u = _gelu(u_ref[rows, :])
        v = _gelu(v_ref[rows, :])
        mu = jnp.mean(v, axis=-1, keepdims=True)
        vc = v - mu
        v = vc * lax.rsqrt(jnp.mean(vc * vc, axis=-1, keepdims=True) + NORM_EPS)
        zs = []
        for p in range(GMLP_GROUPS // 2):
            vp = v[:, p * LANES:(p + 1) * LANES]
            rhs = jnp.concatenate([jnp.where(low, vp, 0.0), jnp.where(low, 0.0, vp)], axis=0)
            zs.append(jnp.dot(w[:, p * 2 * c:(p + 1) * 2 * c], rhs.astype(BF16),
                              preferred_element_type=F32))
        z = jnp.concatenate(zs, axis=1) + bias
        o_ref[rows, :] = (u * z).astype(BF16)


def _gmlp(of, w_t, bias_full):
    m = of.shape[0]
    tr = GMLP_ROWS
    return pl.pallas_call(
        _gmlp_kernel,
        grid=(m // tr,),
        in_specs=[
            pl.BlockSpec((tr, GROUP_WIDTH), lambda i: (i, 0)),
            pl.BlockSpec((tr, GROUP_WIDTH), lambda i: (i, 1)),
            pl.BlockSpec((GMLP_CHUNK, GMLP_GROUPS * GMLP_CHUNK), lambda i: (0, 0)),
            pl.BlockSpec((GMLP_CHUNK, GROUP_WIDTH), lambda i: (0, 0)),
        ],
        out_specs=pl.BlockSpec((tr, GROUP_WIDTH), lambda i: (i, 0)),
        out_shape=jax.ShapeDtypeStruct((m, GROUP_WIDTH), BF16),
        compiler_params=_params(("parallel",)),
        name="gmlp",
    )(of, of, w_t, bias_full)


def _stack_queries(q):
    lane = lax.broadcasted_iota(jnp.int32, q.shape, 1)
    low = lane < HEAD_DIM
    zero = jnp.zeros_like(q)
    q2 = jnp.concatenate([jnp.where(low, q, zero), jnp.where(low, zero, q)], axis=0)
    return (q2.astype(F32) * (ATTN_SCALE * LOG2E)).astype(BF16)


def _qk(q2, k):
    return lax.dot_general(q2, k, (((1,), (1,)), ((), ())), preferred_element_type=F32)


def _cast_specs(w, li, nsteps, step_of):
    _, rows, cols = w.shape
    slab = rows // nsteps
    assert slab * nsteps == rows and slab % 16 == 0
    in_spec = pl.BlockSpec((None, slab, cols), lambda *ids: (li, step_of(*ids), 0))
    out_spec = pl.BlockSpec((slab, cols), lambda *ids: (step_of(*ids), 0))
    return in_spec, out_spec, jax.ShapeDtypeStruct((rows, cols), BF16)


def _diff_kernel(q_ref, k_ref, v_ref, bias_ref, lam_ref, g_ref, wf_ref, o_ref, wb_ref,
                 vt_s, acc_s, *, lam_init, slopes):
    wb_ref[...] = wf_ref[...].astype(BF16)
    qi = pl.program_id(1)
    tq, tk = DIFF_TQ, DIFF_TK
    seq = k_ref.shape[0]
    heads = range(DIFF_HEADS)
    hcols = lambda h: slice(h * LANES, (h + 1) * LANES)

    @pl.when(qi == 0)
    def _():
        for h in heads:
            for c in range(seq // tk):
                rows = slice(c * tk, (c + 1) * tk)
                vt_s[h, 0:LANES, rows] = v_ref[rows, hcols(h)].astype(F32).T.astype(BF16)
            vt_s[h, LANES:, :] = jnp.ones((DIFF_ONES_ROWS, seq), BF16)

    q2 = [_stack_queries(q_ref[:, hcols(h)]) for h in heads]
    acc_s[...] = jnp.zeros(acc_s.shape, F32)

    def step_group(hs, start, which, dist, m_prev):
        if which == 0:
            bias = {h: jnp.concatenate([bias_ref[h, 0, :, 0:LANES]] * (2 * tq // LANES), axis=1)
                    for h in hs}
        else:
            bias = {h: bias_ref[h, 1] for h in hs}
        sts = {h: _qk(k_ref[pl.ds(start, tk), hcols(h)], q2[h]) + bias[h]
               for h in hs}
        shifts = {h: (-slopes[h] * LOG2E) * dist for h in hs}
        m_next = {h: jnp.maximum(m_prev[h], jnp.max(sts[h], axis=0, keepdims=True) + shifts[h])
                  for h in hs}
        ps = {h: jnp.exp2(sts[h] - (m_next[h] - shifts[h])).astype(BF16) for h in hs}
        alphas = {h: jnp.exp2(m_prev[h] - m_next[h]) for h in hs}
        pvs = {h: jnp.dot(vt_s[h, :, pl.ds(start, tk)], ps[h], preferred_element_type=F32)
               for h in hs}
        for h in hs:
            acc_s[h] = alphas[h] * acc_s[h] + pvs[h]
        return m_next

    def step(start, which, dist, m_prev):
        m_next = {}
        for g in range(0, DIFF_HEADS, DIFF_HEAD_GROUP):
            m_next.update(step_group(range(g, g + DIFF_HEAD_GROUP), start, which, dist, m_prev))
        return tuple(m_next[h] for h in heads)

    def body(j, m_prev):
        return step(pl.multiple_of(j * tk, tk), 0, ((qi - j) * tq).astype(F32), m_prev)

    m0 = jnp.full((1, 2 * tq), MASKED, F32)
    m_prev = lax.fori_loop(0, qi, body, tuple(m0 for _ in heads))
    step(pl.multiple_of(qi * tq, tq), 1, 0.0, m_prev)

    lp = lam_ref[...]
    lam = (jnp.exp(jnp.sum(lp[0:1] * lp[1:2], axis=1, keepdims=True))
           - jnp.exp(jnp.sum(lp[2:3] * lp[3:4], axis=1, keepdims=True)) + lam_init)
    for h in heads:
        o = acc_s[h, 0:LANES, :] / acc_s[h, LANES:LANES + 1, :]
        o = o[:, :tq] - lam * o[:, tq:]
        y = o * lax.rsqrt(jnp.mean(o * o, axis=0, keepdims=True) + NORM_EPS) * g_ref[...]
        o_ref[:, hcols(h)] = (y * (1.0 - lam_init)).T.astype(BF16)


def _diff_bias_tiles(slopes):
    tq, tk = DIFF_TQ, DIFF_TK
    rq = (np.arange(2 * tq) % tq)[None, :].astype(np.float64)
    c = np.arange(tk)[:, None].astype(np.float64)
    tiles = np.zeros((DIFF_HEADS, 2, tk, 2 * tq), np.float64)
    for h, sl in enumerate(slopes):
        tiles[h, 0] = sl * LOG2E * c + 0.0 * rq
        tiles[h, 1] = np.where(rq - c >= 0, sl * LOG2E * c, MASKED)
    return tiles.astype(np.float32)


def _diff_attention(ob, lam_p, subln_g, lam_init, batch, seq, w_cast, li):
    m = ob.shape[0]
    tq, tk = DIFF_TQ, DIFF_TK
    assert tq == tk and seq % tq == 0
    nq = seq // tq
    slopes, _ = _alibi_slopes()
    bias = jnp.asarray(_diff_bias_tiles(slopes))
    kern = functools.partial(_diff_kernel, lam_init=lam_init,
                             slopes=tuple(float(s) for s in slopes))
    w_in_spec, w_out_spec, w_shape = _cast_specs(w_cast, li, batch * nq, lambda b, i: b * nq + i)
    return pl.pallas_call(
        kern,
        grid=(batch, nq),
        in_specs=[
            pl.BlockSpec((tq, GROUP_WIDTH), lambda b, i: (b * nq + i, 0)),
            pl.BlockSpec((seq, GROUP_WIDTH), lambda b, i: (b, 1)),
            pl.BlockSpec((seq, GROUP_WIDTH), lambda b, i: (b, 2)),
            pl.BlockSpec((DIFF_HEADS, 2, tk, 2 * tq), lambda b, i: (0, 0, 0, 0)),
            pl.BlockSpec((4, HEAD_DIM), lambda b, i: (0, 0)),
            pl.BlockSpec((2 * HEAD_DIM, 1), lambda b, i: (0, 0)),
            w_in_spec,
        ],
        out_specs=[pl.BlockSpec((tq, GROUP_WIDTH), lambda b, i: (b * nq + i, 0)), w_out_spec],
        out_shape=[jax.ShapeDtypeStruct((m, GROUP_WIDTH), BF16), w_shape],
        scratch_shapes=[
            pltpu.VMEM((DIFF_HEADS, LANES + DIFF_ONES_ROWS, seq), BF16),
            pltpu.VMEM((DIFF_HEADS, LANES + DIFF_ONES_ROWS, 2 * tq), F32),
        ],
        compiler_params=_params(("parallel", "arbitrary")),
        name="diff_attn",
    )(ob, ob, ob, bias, lam_p, subln_g, w_cast)


def _dil_kernel(q_ref, k_ref, v_ref, bias_ref, wf_ref, o_ref, wb_ref, qf, kf, vf, m_s, l_s, o_s,
                *, seq):
    wb_ref[...] = wf_ref[...].astype(BF16)
    n = DIL_BLOCK
    qf[...] = q_ref[...].astype(F32)
    kf[...] = k_ref[...].astype(F32)
    vf[...] = v_ref[...].astype(F32)
    lane = lax.broadcasted_iota(jnp.int32, (n, LANES), 1)
    low = lane < HEAD_DIM

    for pi, (window, dil) in enumerate(DIL_PATTERNS):
        nblk = seq // (n * dil)
        for r in range(dil):
            for c in range(nblk):
                cur = pl.ds(r + dil * n * c, n, stride=dil)
                q2 = _stack_queries(qf[cur, :])
                if c > 0:
                    prev = pl.ds(r + dil * n * (c - 1), n, stride=dil)
                    kb = jnp.concatenate([kf[prev, :], kf[cur, :]], axis=0).astype(BF16)
                    vb = jnp.concatenate([vf[prev, :], vf[cur, :]], axis=0).astype(BF16)
                    bias = bias_ref[pi]
                else:
                    kb = kf[cur, :].astype(BF16)
                    vb = vf[cur, :].astype(BF16)
                    bias = bias_ref[pi, :, n:]
                s = _qk(q2, kb) + bias
                mx = jnp.max(s, axis=1, keepdims=True)
                p = jnp.exp2(s - mx).astype(BF16)
                vb1 = jnp.concatenate([vb, jnp.ones(vb.shape, BF16)], axis=1)
                ol = jnp.dot(p, vb1, preferred_element_type=F32)
                o, l = ol[:, :LANES], ol[:, LANES:]
                o_s[pi, cur, :] = jnp.where(low, o[:n], o[n:])
                m_s[pi, cur, :] = jnp.where(low, mx[:n], mx[n:])
                l_s[pi, cur, :] = jnp.where(low, l[:n], l[n:])

    m_all = jnp.maximum(jnp.maximum(m_s[0], m_s[1]), m_s[2])
    num = jnp.zeros((seq, LANES), F32)
    den = jnp.zeros((seq, LANES), F32)
    for pi in range(len(DIL_PATTERNS)):
        w = jnp.exp2(m_s[pi] - m_all)
        num = num + w * o_s[pi]
        den = den + w * l_s[pi]
    o_ref[...] = (num / den).astype(BF16)


def _dil_bias_tiles(slopes):
    n = DIL_BLOCK
    qi = (np.arange(2 * n) % n)[:, None]
    ki = np.arange(2 * n)[None, :]
    step = n + qi - ki
    valid = (step >= 0) & (step <= n)
    tiles = np.zeros((len(DIL_PATTERNS), DIL_HEADS // 2, 2 * n, 2 * n), np.float64)
    for pi, (_, dil) in enumerate(DIL_PATTERNS):
        for p in range(DIL_HEADS // 2):
            sl = np.where(np.arange(2 * n) < n, slopes[2 * p], slopes[2 * p + 1])[:, None]
            tiles[pi, p] = np.where(valid, -sl * LOG2E * (step * dil), MASKED)
    return tiles.astype(np.float32)


def _dil_attention(ob, batch, seq, w_cast, li):
    m = ob.shape[0]
    _, slopes = _alibi_slopes()
    bias = jnp.asarray(_dil_bias_tiles(slopes))
    npairs = DIL_HEADS // 2
    npat = len(DIL_PATTERNS)
    col0 = 3 * GROUP_WIDTH // LANES
    kern = functools.partial(_dil_kernel, seq=seq)
    w_in_spec, w_out_spec, w_shape = _cast_specs(w_cast, li, batch * npairs,
                                                 lambda b, p: b * npairs + p)
    return pl.pallas_call(
        kern,
        grid=(batch, npairs),
        in_specs=[
            pl.BlockSpec((seq, LANES), lambda b, p: (b, col0 + p)),
            pl.BlockSpec((seq, LANES), lambda b, p: (b, col0 + npairs + p)),
            pl.BlockSpec((seq, LANES), lambda b, p: (b, col0 + 2 * npairs + p)),
            pl.BlockSpec((npat, None, 2 * DIL_BLOCK, 2 * DIL_BLOCK), lambda b, p: (0, p, 0, 0)),
            w_in_spec,
        ],
        out_specs=[pl.BlockSpec((seq, LANES), lambda b, p: (b, p)), w_out_spec],
        out_shape=[jax.ShapeDtypeStruct((m, GROUP_WIDTH), BF16), w_shape],
        scratch_shapes=[
            pltpu.VMEM((seq, LANES), F32),
            pltpu.VMEM((seq, LANES), F32),
            pltpu.VMEM((seq, LANES), F32),
            pltpu.VMEM((npat, seq, LANES), F32),
            pltpu.VMEM((npat, seq, LANES), F32),
            pltpu.VMEM((npat, seq, LANES), F32),
        ],
        compiler_params=_params(("parallel", "parallel")),
        name="dil_attn",
    )(ob, ob, ob, bias, w_cast)


def _conv_kernel(a_ref, g_ref, ap_ref, gp_ref, w_ref, b_ref, n_ref, *rest, ncast):
    wf_refs, o_ref, wb_refs = rest[:ncast], rest[ncast], rest[ncast + 1:2 * ncast + 1]
    h_s, y_s = rest[2 * ncast + 1:]
    for wf_ref, wb_ref in zip(wf_refs, wb_refs):
        wb_ref[...] = wf_ref[...].astype(BF16)
    i = pl.program_id(1)
    tr, halo = CONV_ROWS, CONV_HALO
    hp = ap_ref[...] * jax.nn.sigmoid(gp_ref[...])
    h_s[0:halo, :] = jnp.where(i > 0, hp, 0.0)
    h_s[halo:, :] = a_ref[...] * jax.nn.sigmoid(g_ref[...])
    _conv_rows(h_s, w_ref, b_ref, n_ref, y_s, o_ref, 0, tr)


def _conv_rows(h_s, w_ref, b_ref, n_ref, y_s, o_ref, row0, nrows):
    rc, sub, last, halo = CONV_CHUNK, SUBLANES, CONV_WIDTH - 1, CONV_HALO
    for lt in range(GROUP_WIDTH // LANES):
        lanes = slice(lt * LANES, (lt + 1) * LANES)
        for c in range(nrows // rc):
            base = row0 + c * rc
            y = jnp.zeros((rc, LANES), F32)
            for r in range(sub):
                a_r = None
                for q in range((last - r) // sub + 1):
                    d = sub * q + r
                    start = base + halo - sub - sub * q
                    win = h_s[start:start + rc + sub, lanes]
                    term = w_ref[last - d, :, lanes] * win.reshape(rc // sub + 1, sub, LANES)
                    a_r = term if a_r is None else a_r + term
                y = y + a_r.reshape(rc + sub, LANES)[sub - r:sub - r + rc, :]
            y_s[c * rc:(c + 1) * rc, lanes] = y
    y = _rms(y_s[0:nrows, :] + b_ref[...], n_ref[...])
    o_ref[row0:row0 + nrows, :] = (y * jax.nn.sigmoid(y)).astype(BF16)


def _conv(of, w, b, g, batch, seq, casts):
    m = of.shape[0]
    tr, halo = CONV_ROWS, CONV_HALO
    nt = seq // tr
    per = tr // halo
    cur = lambda col: (lambda bb, i: (bb * nt + i, col))
    prev = lambda col: (lambda bb, i: (jnp.maximum((bb * nt + i) * per - 1, 0), col))
    cast_specs = [_cast_specs(wc, lc, batch * nt, lambda bb, i: bb * nt + i) for wc, lc in casts]
    return pl.pallas_call(
        functools.partial(_conv_kernel, ncast=len(casts)),
        grid=(batch, nt),
        in_specs=[
            pl.BlockSpec((tr, GROUP_WIDTH), cur(2)),
            pl.BlockSpec((tr, GROUP_WIDTH), cur(3)),
            pl.BlockSpec((halo, GROUP_WIDTH), prev(2)),
            pl.BlockSpec((halo, GROUP_WIDTH), prev(3)),
            pl.BlockSpec((CONV_WIDTH, SUBLANES, GROUP_WIDTH), lambda bb, i: (0, 0, 0)),
            pl.BlockSpec((1, GROUP_WIDTH), lambda bb, i: (0, 0)),
            pl.BlockSpec((1, GROUP_WIDTH), lambda bb, i: (0, 0)),
            *[cs[0] for cs in cast_specs],
        ],
        out_specs=[pl.BlockSpec((tr, GROUP_WIDTH), lambda bb, i: (bb * nt + i, 0)),
                   *[cs[1] for cs in cast_specs]],
        out_shape=[jax.ShapeDtypeStruct((m, GROUP_WIDTH), BF16), *[cs[2] for cs in cast_specs]],
        scratch_shapes=[pltpu.VMEM((halo + tr, GROUP_WIDTH), F32),
                        pltpu.VMEM((tr, GROUP_WIDTH), F32)],
        compiler_params=_params(("parallel", "parallel")),
        name="conformer_conv",
    )(of, of, of, of, jnp.broadcast_to(w[:, None, :], (CONV_WIDTH, SUBLANES, GROUP_WIDTH)), b, g,
      *[wc for wc, _ in casts])


def _out_kernel(a_ref, b_ref, c_ref, d_ref, w_ref, x_ref, g_ref, gn_ref, o_ref, h_ref):
    gw = GROUP_WIDTH
    for c in range(TM_OUT // OUT_ROW_CHUNK):
        rows = slice(c * OUT_ROW_CHUNK, (c + 1) * OUT_ROW_CHUNK)
        y = jnp.dot(a_ref[rows, :], w_ref[0:gw, :], preferred_element_type=F32)
        y = y + jnp.dot(b_ref[rows, :], w_ref[gw:2 * gw, :], preferred_element_type=F32)
        y = y + jnp.dot(c_ref[rows, :], w_ref[2 * gw:3 * gw, :], preferred_element_type=F32)
        y = y + jnp.dot(d_ref[rows, :], w_ref[3 * gw:4 * gw, :], preferred_element_type=F32)
        xn = x_ref[rows, :] + _rms(y, g_ref[...])
        o_ref[rows, :] = xn
        h_ref[rows, :] = _rms(xn, gn_ref[...]).astype(BF16)


def _out_proj(mix, w, x, g, g_next):
    m = x.shape[0]
    tm = TM_OUT
    mix_spec = pl.BlockSpec((tm, GROUP_WIDTH), lambda i: (i, 0))
    row_spec = pl.BlockSpec((tm, D_MODEL), lambda i: (i, 0))
    gain_spec = pl.BlockSpec((1, D_MODEL), lambda i: (0, 0))
    return pl.pallas_call(
        _out_kernel,
        grid=(m // tm,),
        in_specs=[mix_spec, mix_spec, mix_spec, mix_spec,
                  pl.BlockSpec((D_MODEL, D_MODEL), lambda i: (0, 0), pipeline_mode=pl.Buffered(1)),
                  row_spec, gain_spec, gain_spec],
        out_specs=[row_spec, row_spec],
        out_shape=[jax.ShapeDtypeStruct((m, D_MODEL), F32),
                   jax.ShapeDtypeStruct((m, D_MODEL), BF16)],
        compiler_params=_params(("parallel",)),
        name="out_proj",
    )(*mix, w, x, g, g_next)


def _ffn_kernel(x_hbm, h_ref, w1_ref, w2_ref, gpost_ref, o_ref, x_s, x_sem):
    i, j = pl.program_id(0), pl.program_id(1)
    x_copy = pltpu.make_async_copy(x_hbm.at[pl.ds(i * TM_FFN, TM_FFN), :], x_s, x_sem)

    @pl.when(j == 0)
    def _():
        x_copy.start()
        o_ref[...] = jnp.zeros(o_ref.shape, F32)

    for cf in range(TF_FFN // FFN_MID_CHUNK):
        mid = slice(cf * FFN_MID_CHUNK, (cf + 1) * FFN_MID_CHUNK)
        f = jnp.dot(h_ref[...], w1_ref[:, mid], preferred_element_type=F32)
        f = jnp.square(jnp.maximum(f, 0.0)).astype(BF16)
        for c in range(D_MODEL // FFN_OUT_CHUNK):
            cols = slice(c * FFN_OUT_CHUNK, (c + 1) * FFN_OUT_CHUNK)
            o_ref[:, cols] += jnp.dot(f, w2_ref[mid, cols], preferred_element_type=F32)

    @pl.when(j == pl.num_programs(1) - 1)
    def _():
        x_copy.wait()
        o_ref[...] = x_s[...] + _rms(o_ref[...], gpost_ref[...])


def _ffn(x, h, w1, w2, gpost):
    m = x.shape[0]
    tm, tf = TM_FFN, TF_FFN
    return pl.pallas_call(
        _ffn_kernel,
        grid=(m // tm, D_FF // tf),
        in_specs=[
            pl.BlockSpec(memory_space=pl.ANY),
            pl.BlockSpec((tm, D_MODEL), lambda i, j: (i, 0)),
            pl.BlockSpec((D_MODEL, tf), lambda i, j: (0, j)),
            pl.BlockSpec((tf, D_MODEL), lambda i, j: (j, 0)),
            pl.BlockSpec((1, D_MODEL), lambda i, j: (0, 0)),
        ],
        out_specs=pl.BlockSpec((tm, D_MODEL), lambda i, j: (i, 0)),
        out_shape=jax.ShapeDtypeStruct((m, D_MODEL), F32),
        scratch_shapes=[pltpu.VMEM((tm, D_MODEL), F32), pltpu.SemaphoreType.DMA(())],
        compiler_params=_params(("arbitrary", "arbitrary"), VMEM_LIMIT_BIG),
        name="ffn",
    )(x, h, w1, w2, gpost)


def kernel(x, g_mix_pre, g_mix_post, w_in, gmlp_w, gmlp_b, diff_lam, diff_subln, conv_w, conv_b,
           conv_norm, w_out, g_ffn_pre, g_ffn_post, w_ff1, w_ff2):
    batch, seq, _ = x.shape
    depth = w_in.shape[0]
    xf = x.reshape(batch * seq, D_MODEL)
    row = lambda v: v.reshape(1, -1)
    w_in_b = w_in[0].astype(BF16)
    for li in range(depth):
        of, ob = _in_proj(xf, row(g_mix_pre[li]), w_in_b)

        w_t = gmlp_w[li].transpose(1, 0, 2).reshape(GMLP_CHUNK, GMLP_GROUPS * GMLP_CHUNK)
        bias_full = jnp.repeat(gmlp_b[li].T, GROUP_WIDTH // GMLP_GROUPS, axis=1)
        out_a = _gmlp(of, w_t, bias_full)

        lam_init = 0.8 - 0.6 * float(np.exp(-0.3 * li))
        out_b, w_ff1_b = _diff_attention(ob, diff_lam[li], diff_subln[li].reshape(-1, 1),
                                         lam_init, batch, seq, w_ff1, li)
        out_c, w_ff2_b = _dil_attention(ob, batch, seq, w_ff2, li)
        casts = [(w_out, li)] + ([(w_in, li + 1)] if li + 1 < depth else [])
        out_d, w_out_b, *w_next = _conv(of, conv_w[li], row(conv_b[li]), row(conv_norm[li]),
                                        batch, seq, casts)
        if w_next:
            w_in_b = w_next[0]

        xf, h_ffn = _out_proj((out_a, out_b, out_c, out_d), w_out_b, xf, row(g_mix_post[li]),
                              row(g_ffn_pre[li]))
        xf = _ffn(xf, h_ffn, w_ff1_b, w_ff2_b, row(g_ffn_post[li]))
    return xf.reshape(batch, seq, D_MODEL)
```

```python
import functools

import numpy as np
import jax
import jax.numpy as jnp
from jax import lax
from jax.experimental import pallas as pl
from jax.experimental.pallas import tpu as pltpu

F32 = jnp.float32
BF16 = jnp.bfloat16

D_MODEL = 2048
GROUP_WIDTH = 512
N_IN_SPLITS = 10
D_FF = 4 * D_MODEL
NORM_EPS = 1e-6
GMLP_CHUNK = 128
GMLP_GROUPS = 8
DIFF_HEADS = 4
DIL_HEADS = 8
HEAD_DIM = 64
DIL_PATTERNS = ((128, 1), (512, 4), (2048, 16))
DIL_BLOCK = 128
CONV_WIDTH = 31
N_ALIBI_HEADS = DIFF_HEADS + DIL_HEADS
ATTN_SCALE = HEAD_DIM ** -0.5
MASKED = -1e30
LOG2E = 1.4426950408889634
DIFF_ONES_ROWS = 16
DIFF_HEAD_GROUP = 4

LANES = 128
VMEM_LIMIT = 48 * 1024 * 1024
VMEM_LIMIT_BIG = 60 * 1024 * 1024

TM_PROJ = 1024
TM_OUT = 512
OUT_ROW_CHUNK = 256
TM_FFN = 1024
TF_FFN = 1024
FFN_MID_CHUNK = 512
FFN_OUT_CHUNK = 512
NORM_ROWS = 16
GMLP_ROWS = 1024
DIFF_TQ = 256
DIFF_TK = 256
CONV_ROWS = 512
CONV_HALO = 32
CONV_CHUNK = 128
SUBLANES = 8


def _alibi_slopes():
    i = np.arange(1, N_ALIBI_HEADS + 1, dtype=np.float64)
    s = 2.0 ** (-8.0 * i / N_ALIBI_HEADS)
    diff_idx = np.arange(0, N_ALIBI_HEADS, 3)
    dil_idx = np.array([j for j in range(N_ALIBI_HEADS) if j % 3 != 0])
    return s[diff_idx], s[dil_idx]


def _rms(x, g):
    return x * lax.rsqrt(jnp.mean(x * x, axis=-1, keepdims=True) + NORM_EPS) * g


def _params(sem, limit=VMEM_LIMIT):
    return pltpu.CompilerParams(dimension_semantics=sem, vmem_limit_bytes=limit)


def _layer_spec(shape, li):
    return pl.BlockSpec((None, *shape), lambda *ids: (li,) + (0,) * len(shape))


def _proj_kernel(x_hbm, g_ref, w_ref, of_ref, ob_ref, h_s, x_s, x_sem):
    i, j = pl.program_id(0), pl.program_id(1)
    nf, nb = 2 * GROUP_WIDTH, 3 * GROUP_WIDTH

    def x_copy(block):
        return pltpu.make_async_copy(x_hbm.at[pl.ds(block * TM_PROJ, TM_PROJ), :], x_s, x_sem)

    def project(col0, ncols, write):
        for c in range(ncols // GROUP_WIDTH):
            cols = slice(col0 + c * GROUP_WIDTH, col0 + (c + 1) * GROUP_WIDTH)
            write(slice(c * GROUP_WIDTH, (c + 1) * GROUP_WIDTH),
                  jnp.dot(h_s[...], w_ref[:, cols], preferred_element_type=F32))

    def write_f32(cols, y):
        of_ref[:, cols] = y

    def write_bf16(cols, y):
        ob_ref[:, cols] = y.astype(BF16)

    @pl.when(j == 0)
    def _():
        @pl.when(i == 0)
        def _():
            x_copy(0).start()
        x_copy(i).wait()
        h_s[...] = _rms(x_s[...], g_ref[...]).astype(BF16)

        @pl.when(i + 1 < pl.num_programs(0))
        def _():
            x_copy(i + 1).start()
        project(0, nf, write_f32)
        project(nf, nb, write_bf16)

    @pl.when(j == 1)
    def _():
        project(0, nb, write_bf16)
        project(nb, nf, write_f32)


def _in_proj(x, g, w, li):
    m = x.shape[0]
    tm = TM_PROJ
    half = N_IN_SPLITS * GROUP_WIDTH // 2
    return pl.pallas_call(
        _proj_kernel,
        grid=(m // tm, 2),
        in_specs=[
            pl.BlockSpec(memory_space=pl.ANY),
            _layer_spec((1, D_MODEL), li),
            pl.BlockSpec((D_MODEL, half), lambda i, j: (0, j)),
        ],
        out_specs=[
            pl.BlockSpec((tm, 2 * GROUP_WIDTH), lambda i, j: (i, j)),
            pl.BlockSpec((tm, 3 * GROUP_WIDTH), lambda i, j: (i, j)),
        ],
        out_shape=[
            jax.ShapeDtypeStruct((m, 4 * GROUP_WIDTH), F32),
            jax.ShapeDtypeStruct((m, 6 * GROUP_WIDTH), BF16),
        ],
        scratch_shapes=[pltpu.VMEM((tm, D_MODEL), BF16), pltpu.VMEM((tm, D_MODEL), F32),
                        pltpu.SemaphoreType.DMA(())],
        compiler_params=_params(("arbitrary", "arbitrary"), VMEM_LIMIT_BIG),
        name="in_proj",
    )(x, g, w)


def _gelu(x):
    return jax.nn.gelu(x, approximate=True)


def _gmlp_kernel(u_ref, v_ref, w_ref, b_ref, o_ref):
    c = GMLP_CHUNK
    t_idx = lax.broadcasted_iota(jnp.int32, w_ref.shape, 0)
    s_idx = lax.broadcasted_iota(jnp.int32, w_ref.shape, 1) % c
    w = jnp.where(s_idx <= t_idx, w_ref[...], 0.0).astype(BF16)
    bias = b_ref[...]
    lane = lax.broadcasted_iota(jnp.int32, (c, LANES), 1)
    low = lane < HEAD_DIM
    for ci in range(GMLP_ROWS // c):
        rows = slice(ci * c, (ci + 1) * c)
        u = _gelu(u_ref[rows, :])
        v = _gelu(v_ref[rows, :])
        mu = jnp.mean(v, axis=-1, keepdims=True)
        vc = v - mu
        v = vc * lax.rsqrt(jnp.mean(vc * vc, axis=-1, keepdims=True) + NORM_EPS)
        zs = []
        for p in range(GMLP_GROUPS // 2):
            vp = v[:, p * LANES:(p + 1) * LANES]
            rhs = jnp.concatenate([jnp.where(low, vp, 0.0), jnp.where(low, 0.0, vp)], axis=0)
            zs.append(jnp.dot(w[:, p * 2 * c:(p + 1) * 2 * c], rhs.astype(BF16),
                              preferred_element_type=F32))
        z = jnp.concatenate(zs, axis=1) + bias
        o_ref[rows, :] = (u * z).astype(BF16)


def _gmlp(of, w_t, bias_full, li):
    m = of.shape[0]
    tr = GMLP_ROWS
    return pl.pallas_call(
        _gmlp_kernel,
        grid=(m // tr,),
        in_specs=[
            pl.BlockSpec((tr, GROUP_WIDTH), lambda i: (i, 0)),
            pl.BlockSpec((tr, GROUP_WIDTH), lambda i: (i, 1)),
            _layer_spec((GMLP_CHUNK, GMLP_GROUPS * GMLP_CHUNK), li),
            _layer_spec((GMLP_CHUNK, GROUP_WIDTH), li),
        ],
        out_specs=pl.BlockSpec((tr, GROUP_WIDTH), lambda i: (i, 0)),
        out_shape=jax.ShapeDtypeStruct((m, GROUP_WIDTH), BF16),
        compiler_params=_params(("parallel",)),
        name="gmlp",
    )(of, of, w_t, bias_full)


def _stack_queries(q):
    lane = lax.broadcasted_iota(jnp.int32, q.shape, 1)
    low = lane < HEAD_DIM
    zero = jnp.zeros_like(q)
    q2 = jnp.concatenate([jnp.where(low, q, zero), jnp.where(low, zero, q)], axis=0)
    return (q2.astype(F32) * (ATTN_SCALE * LOG2E)).astype(BF16)


def _qk(q2, k):
    return lax.dot_general(q2, k, (((1,), (1,)), ((), ())), preferred_element_type=F32)


def _cast_specs(w, li, nsteps, step_of):
    _, rows, cols = w.shape
    slab = rows // nsteps
    assert slab * nsteps == rows and slab % 16 == 0
    in_spec = pl.BlockSpec((None, slab, cols), lambda *ids: (li, step_of(*ids), 0))
    out_spec = pl.BlockSpec((slab, cols), lambda *ids: (step_of(*ids), 0))
    return in_spec, out_spec, jax.ShapeDtypeStruct((rows, cols), BF16)


def _diff_kernel(q_ref, k_ref, v_ref, bias_ref, lam_ref, g_ref, wf_ref, o_ref, wb_ref,
                 vt_s, acc_s, *, lam_init, slopes):
    wb_ref[...] = wf_ref[...].astype(BF16)
    qi = pl.program_id(1)
    tq, tk = DIFF_TQ, DIFF_TK
    seq = k_ref.shape[0]
    heads = range(DIFF_HEADS)
    hcols = lambda h: slice(h * LANES, (h + 1) * LANES)

    @pl.when(qi == 0)
    def _():
        for h in heads:
            for c in range(seq // tk):
                rows = slice(c * tk, (c + 1) * tk)
                vt_s[h, 0:LANES, rows] = v_ref[rows, hcols(h)].astype(F32).T.astype(BF16)
            vt_s[h, LANES:, :] = jnp.ones((DIFF_ONES_ROWS, seq), BF16)

    q2 = [_stack_queries(q_ref[:, hcols(h)]) for h in heads]
    acc_s[...] = jnp.zeros(acc_s.shape, F32)

    def step_group(hs, start, which, dist, m_prev):
        if which == 0:
            bias = {h: jnp.concatenate([bias_ref[h, 0, :, 0:LANES]] * (2 * tq // LANES), axis=1)
                    for h in hs}
        else:
            bias = {h: bias_ref[h, 1] for h in hs}
        sts = {h: _qk(k_ref[pl.ds(start, tk), hcols(h)], q2[h]) + bias[h]
               for h in hs}
        shifts = {h: (-slopes[h] * LOG2E) * dist for h in hs}
        m_next = {h: jnp.maximum(m_prev[h], jnp.max(sts[h], axis=0, keepdims=True) + shifts[h])
                  for h in hs}
        ps = {h: jnp.exp2(sts[h] - (m_next[h] - shifts[h])).astype(BF16) for h in hs}
        alphas = {h: jnp.exp2(m_prev[h] - m_next[h]) for h in hs}
        pvs = {h: jnp.dot(vt_s[h, :, pl.ds(start, tk)], ps[h], preferred_element_type=F32)
               for h in hs}
        for h in hs:
            acc_s[h] = alphas[h] * acc_s[h] + pvs[h]
        return m_next

    def step(start, which, dist, m_prev):
        m_next = {}
        for g in range(0, DIFF_HEADS, DIFF_HEAD_GROUP):
            m_next.update(step_group(range(g, g + DIFF_HEAD_GROUP), start, which, dist, m_prev))
        return tuple(m_next[h] for h in heads)

    def body(j, m_prev):
        return step(pl.multiple_of(j * tk, tk), 0, ((qi - j) * tq).astype(F32), m_prev)

    m0 = jnp.full((1, 2 * tq), MASKED, F32)
    m_prev = lax.fori_loop(0, qi, body, tuple(m0 for _ in heads))
    step(pl.multiple_of(qi * tq, tq), 1, 0.0, m_prev)

    lp = lam_ref[...]
    lam = (jnp.exp(jnp.sum(lp[0:1] * lp[1:2], axis=1, keepdims=True))
           - jnp.exp(jnp.sum(lp[2:3] * lp[3:4], axis=1, keepdims=True)) + lam_init)
    for h in heads:
        o = acc_s[h, 0:LANES, :] / acc_s[h, LANES:LANES + 1, :]
        o = o[:, :tq] - lam * o[:, tq:]
        y = o * lax.rsqrt(jnp.mean(o * o, axis=0, keepdims=True) + NORM_EPS) * g_ref[...]
        o_ref[:, hcols(h)] = (y * (1.0 - lam_init)).T.astype(BF16)


def _diff_bias_tiles(slopes):
    tq, tk = DIFF_TQ, DIFF_TK
    rq = (np.arange(2 * tq) % tq)[None, :].astype(np.float64)
    c = np.arange(tk)[:, None].astype(np.float64)
    tiles = np.zeros((DIFF_HEADS, 2, tk, 2 * tq), np.float64)
    for h, sl in enumerate(slopes):
        tiles[h, 0] = sl * LOG2E * c + 0.0 * rq
        tiles[h, 1] = np.where(rq - c >= 0, sl * LOG2E * c, MASKED)
    return tiles.astype(np.float32)


def _diff_attention(ob, lam_p, subln_g, lam_init, batch, seq, w_cast, li):
    m = ob.shape[0]
    tq, tk = DIFF_TQ, DIFF_TK
    assert tq == tk and seq % tq == 0
    nq = seq // tq
    slopes, _ = _alibi_slopes()
    bias = jnp.asarray(_diff_bias_tiles(slopes))
    kern = functools.partial(_diff_kernel, lam_init=lam_init,
                             slopes=tuple(float(s) for s in slopes))
    w_in_spec, w_out_spec, w_shape = _cast_specs(w_cast, li, batch * nq, lambda b, i: b * nq + i)
    return pl.pallas_call(
        kern,
        grid=(batch, nq),
        in_specs=[
            pl.BlockSpec((tq, GROUP_WIDTH), lambda b, i: (b * nq + i, 0)),
            pl.BlockSpec((seq, GROUP_WIDTH), lambda b, i: (b, 1)),
            pl.BlockSpec((seq, GROUP_WIDTH), lambda b, i: (b, 2)),
            pl.BlockSpec((DIFF_HEADS, 2, tk, 2 * tq), lambda b, i: (0, 0, 0, 0)),
            _layer_spec((4, HEAD_DIM), li),
            _layer_spec((2 * HEAD_DIM, 1), li),
            w_in_spec,
        ],
        out_specs=[pl.BlockSpec((tq, GROUP_WIDTH), lambda b, i: (b * nq + i, 0)), w_out_spec],
        out_shape=[jax.ShapeDtypeStruct((m, GROUP_WIDTH), BF16), w_shape],
        scratch_shapes=[
            pltpu.VMEM((DIFF_HEADS, LANES + DIFF_ONES_ROWS, seq), BF16),
            pltpu.VMEM((DIFF_HEADS, LANES + DIFF_ONES_ROWS, 2 * tq), F32),
        ],
        compiler_params=_params(("parallel", "arbitrary")),
        name="diff_attn",
    )(ob, ob, ob, bias, lam_p, subln_g, w_cast)


def _dil_kernel(q_ref, k_ref, v_ref, bias_ref, wf_ref, o_ref, wb_ref, qf, kf, vf, m_s, l_s, o_s,
                *, seq):
    wb_ref[...] = wf_ref[...].astype(BF16)
    n = DIL_BLOCK
    qf[...] = q_ref[...].astype(F32)
    kf[...] = k_ref[...].astype(F32)
    vf[...] = v_ref[...].astype(F32)
    lane = lax.broadcasted_iota(jnp.int32, (n, LANES), 1)
    low = lane < HEAD_DIM

    for pi, (window, dil) in enumerate(DIL_PATTERNS):
        nblk = seq // (n * dil)
        for r in range(dil):
            for c in range(nblk):
                cur = pl.ds(r + dil * n * c, n, stride=dil)
                q2 = _stack_queries(qf[cur, :])
                if c > 0:
                    prev = pl.ds(r + dil * n * (c - 1), n, stride=dil)
                    kb = jnp.concatenate([kf[prev, :], kf[cur, :]], axis=0).astype(BF16)
                    vb = jnp.concatenate([vf[prev, :], vf[cur, :]], axis=0).astype(BF16)
                    bias = bias_ref[pi]
                else:
                    kb = kf[cur, :].astype(BF16)
                    vb = vf[cur, :].astype(BF16)
                    bias = bias_ref[pi, :, n:]
                s = _qk(q2, kb) + bias
                mx = jnp.max(s, axis=1, keepdims=True)
                p = jnp.exp2(s - mx).astype(BF16)
                vb1 = jnp.concatenate([vb, jnp.ones(vb.shape, BF16)], axis=1)
                ol = jnp.dot(p, vb1, preferred_element_type=F32)
                o, l = ol[:, :LANES], ol[:, LANES:]
                o_s[pi, cur, :] = jnp.where(low, o[:n], o[n:])
                m_s[pi, cur, :] = jnp.where(low, mx[:n], mx[n:])
                l_s[pi, cur, :] = jnp.where(low, l[:n], l[n:])

    m_all = jnp.maximum(jnp.maximum(m_s[0], m_s[1]), m_s[2])
    num = jnp.zeros((seq, LANES), F32)
    den = jnp.zeros((seq, LANES), F32)
    for pi in range(len(DIL_PATTERNS)):
        w = jnp.exp2(m_s[pi] - m_all)
        num = num + w * o_s[pi]
        den = den + w * l_s[pi]
    o_ref[...] = (num / den).astype(BF16)


def _dil_bias_tiles(slopes):
    n = DIL_BLOCK
    qi = (np.arange(2 * n) % n)[:, None]
    ki = np.arange(2 * n)[None, :]
    step = n + qi - ki
    valid = (step >= 0) & (step <= n)
    tiles = np.zeros((len(DIL_PATTERNS), DIL_HEADS // 2, 2 * n, 2 * n), np.float64)
    for pi, (_, dil) in enumerate(DIL_PATTERNS):
        for p in range(DIL_HEADS // 2):
            sl = np.where(np.arange(2 * n) < n, slopes[2 * p], slopes[2 * p + 1])[:, None]
            tiles[pi, p] = np.where(valid, -sl * LOG2E * (step * dil), MASKED)
    return tiles.astype(np.float32)


def _dil_attention(ob, batch, seq, w_cast, li):
    m = ob.shape[0]
    _, slopes = _alibi_slopes()
    bias = jnp.asarray(_dil_bias_tiles(slopes))
    npairs = DIL_HEADS // 2
    npat = len(DIL_PATTERNS)
    col0 = 3 * GROUP_WIDTH // LANES
    kern = functools.partial(_dil_kernel, seq=seq)
    w_in_spec, w_out_spec, w_shape = _cast_specs(w_cast, li, batch * npairs,
                                                 lambda b, p: b * npairs + p)
    return pl.pallas_call(
        kern,
        grid=(batch, npairs),
        in_specs=[
            pl.BlockSpec((seq, LANES), lambda b, p: (b, col0 + p)),
            pl.BlockSpec((seq, LANES), lambda b, p: (b, col0 + npairs + p)),
            pl.BlockSpec((seq, LANES), lambda b, p: (b, col0 + 2 * npairs + p)),
            pl.BlockSpec((npat, None, 2 * DIL_BLOCK, 2 * DIL_BLOCK), lambda b, p: (0, p, 0, 0)),
            w_in_spec,
        ],
        out_specs=[pl.BlockSpec((seq, LANES), lambda b, p: (b, p)), w_out_spec],
        out_shape=[jax.ShapeDtypeStruct((m, GROUP_WIDTH), BF16), w_shape],
        scratch_shapes=[
            pltpu.VMEM((seq, LANES), F32),
            pltpu.VMEM((seq, LANES), F32),
            pltpu.VMEM((seq, LANES), F32),
            pltpu.VMEM((npat, seq, LANES), F32),
            pltpu.VMEM((npat, seq, LANES), F32),
            pltpu.VMEM((npat, seq, LANES), F32),
        ],
        compiler_params=_params(("parallel", "parallel")),
        name="dil_attn",
    )(ob, ob, ob, bias, w_cast)


def _conv_kernel(a_ref, g_ref, ap_ref, gp_ref, w_ref, b_ref, n_ref, *rest, ncast):
    wf_refs, o_ref, wb_refs = rest[:ncast], rest[ncast], rest[ncast + 1:2 * ncast + 1]
    h_s, y_s = rest[2 * ncast + 1:]
    for wf_ref, wb_ref in zip(wf_refs, wb_refs):
        wb_ref[...] = wf_ref[...].astype(BF16)
    i = pl.program_id(1)
    tr, halo = CONV_ROWS, CONV_HALO
    hp = ap_ref[...] * jax.nn.sigmoid(gp_ref[...])
    h_s[0:halo, :] = jnp.where(i > 0, hp, 0.0)
    h_s[halo:, :] = a_ref[...] * jax.nn.sigmoid(g_ref[...])
    _conv_rows(h_s, w_ref, b_ref, n_ref, y_s, o_ref, 0, tr)


def _conv_rows(h_s, w_ref, b_ref, n_ref, y_s, o_ref, row0, nrows):
    rc, sub, last, halo = CONV_CHUNK, SUBLANES, CONV_WIDTH - 1, CONV_HALO
    for lt in range(GROUP_WIDTH // LANES):
        lanes = slice(lt * LANES, (lt + 1) * LANES)
        for c in range(nrows // rc):
            base = row0 + c * rc
            y = jnp.zeros((rc, LANES), F32)
            for r in range(sub):
                a_r = None
                for q in range((last - r) // sub + 1):
                    d = sub * q + r
                    start = base + halo - sub - sub * q
                    win = h_s[start:start + rc + sub, lanes]
                    term = w_ref[last - d, :, lanes] * win.reshape(rc // sub + 1, sub, LANES)
                    a_r = term if a_r is None else a_r + term
                y = y + a_r.reshape(rc + sub, LANES)[sub - r:sub - r + rc, :]
            y_s[c * rc:(c + 1) * rc, lanes] = y
    y = _rms(y_s[0:nrows, :] + b_ref[...], n_ref[...])
    o_ref[row0:row0 + nrows, :] = (y * jax.nn.sigmoid(y)).astype(BF16)


def _conv(of, w, b, g, batch, seq, casts, li):
    m = of.shape[0]
    tr, halo = CONV_ROWS, CONV_HALO
    nt = seq // tr
    per = tr // halo
    cur = lambda col: (lambda bb, i: (bb * nt + i, col))
    prev = lambda col: (lambda bb, i: (jnp.maximum((bb * nt + i) * per - 1, 0), col))
    cast_specs = [_cast_specs(wc, lc, batch * nt, lambda bb, i: bb * nt + i) for wc, lc in casts]
    return pl.pallas_call(
        functools.partial(_conv_kernel, ncast=len(casts)),
        grid=(batch, nt),
        in_specs=[
            pl.BlockSpec((tr, GROUP_WIDTH), cur(2)),
            pl.BlockSpec((tr, GROUP_WIDTH), cur(3)),
            pl.BlockSpec((halo, GROUP_WIDTH), prev(2)),
            pl.BlockSpec((halo, GROUP_WIDTH), prev(3)),
            _layer_spec((CONV_WIDTH, SUBLANES, GROUP_WIDTH), li),
            _layer_spec((1, GROUP_WIDTH), li),
            _layer_spec((1, GROUP_WIDTH), li),
            *[cs[0] for cs in cast_specs],
        ],
        out_specs=[pl.BlockSpec((tr, GROUP_WIDTH), lambda bb, i: (bb * nt + i, 0)),
                   *[cs[1] for cs in cast_specs]],
        out_shape=[jax.ShapeDtypeStruct((m, GROUP_WIDTH), BF16), *[cs[2] for cs in cast_specs]],
        scratch_shapes=[pltpu.VMEM((halo + tr, GROUP_WIDTH), F32),
                        pltpu.VMEM((tr, GROUP_WIDTH), F32)],
        compiler_params=_params(("parallel", "parallel")),
        name="conformer_conv",
    )(of, of, of, of, w, b, g, *[wc for wc, _ in casts])


def _out_kernel(a_ref, b_ref, c_ref, d_ref, w_ref, x_ref, g_ref, gn_ref, o_ref, h_ref):
    gw = GROUP_WIDTH
    for c in range(TM_OUT // OUT_ROW_CHUNK):
        rows = slice(c * OUT_ROW_CHUNK, (c + 1) * OUT_ROW_CHUNK)
        y = jnp.dot(a_ref[rows, :], w_ref[0:gw, :], preferred_element_type=F32)
        y = y + jnp.dot(b_ref[rows, :], w_ref[gw:2 * gw, :], preferred_element_type=F32)
        y = y + jnp.dot(c_ref[rows, :], w_ref[2 * gw:3 * gw, :], preferred_element_type=F32)
        y = y + jnp.dot(d_ref[rows, :], w_ref[3 * gw:4 * gw, :], preferred_element_type=F32)
        xn = x_ref[rows, :] + _rms(y, g_ref[...])
        o_ref[rows, :] = xn
        h_ref[rows, :] = _rms(xn, gn_ref[...]).astype(BF16)


def _out_proj(mix, w, x, g, g_next, li):
    m = x.shape[0]
    tm = TM_OUT
    mix_spec = pl.BlockSpec((tm, GROUP_WIDTH), lambda i: (i, 0))
    row_spec = pl.BlockSpec((tm, D_MODEL), lambda i: (i, 0))
    gain_spec = _layer_spec((1, D_MODEL), li)
    return pl.pallas_call(
        _out_kernel,
        grid=(m // tm,),
        in_specs=[mix_spec, mix_spec, mix_spec, mix_spec,
                  pl.BlockSpec((D_MODEL, D_MODEL), lambda i: (0, 0), pipeline_mode=pl.Buffered(1)),
                  row_spec, gain_spec, gain_spec],
        out_specs=[row_spec, row_spec],
        out_shape=[jax.ShapeDtypeStruct((m, D_MODEL), F32),
                   jax.ShapeDtypeStruct((m, D_MODEL), BF16)],
        compiler_params=_params(("parallel",)),
        name="out_proj",
    )(*mix, w, x, g, g_next)


def _ffn_kernel(x_hbm, h_ref, w1_ref, w2_ref, gpost_ref, o_ref, x_s, x_sem):
    i, j = pl.program_id(0), pl.program_id(1)
    x_copy = pltpu.make_async_copy(x_hbm.at[pl.ds(i * TM_FFN, TM_FFN), :], x_s, x_sem)

    def step(first):
        for cf in range(TF_FFN // FFN_MID_CHUNK):
            mid = slice(cf * FFN_MID_CHUNK, (cf + 1) * FFN_MID_CHUNK)
            f = jnp.dot(h_ref[...], w1_ref[:, mid], preferred_element_type=F32)
            f = jnp.square(jnp.maximum(f, 0.0)).astype(BF16)
            for c in range(D_MODEL // FFN_OUT_CHUNK):
                cols = slice(c * FFN_OUT_CHUNK, (c + 1) * FFN_OUT_CHUNK)
                y = jnp.dot(f, w2_ref[mid, cols], preferred_element_type=F32)
                if first and cf == 0:
                    o_ref[:, cols] = y
                else:
                    o_ref[:, cols] += y

    @pl.when(j == 0)
    def _():
        x_copy.start()
        step(True)

    @pl.when(j > 0)
    def _():
        step(False)

    @pl.when(j == pl.num_programs(1) - 1)
    def _():
        x_copy.wait()
        for r in range(TM_FFN // NORM_ROWS):
            rows = slice(r * NORM_ROWS, (r + 1) * NORM_ROWS)
            o_ref[rows, :] = x_s[rows, :] + _rms(o_ref[rows, :], gpost_ref[...])


def _ffn(x, h, w1, w2, gpost, li):
    m = x.shape[0]
    tm, tf = TM_FFN, TF_FFN
    return pl.pallas_call(
        _ffn_kernel,
        grid=(m // tm, D_FF // tf),
        in_specs=[
            pl.BlockSpec(memory_space=pl.ANY),
            pl.BlockSpec((tm, D_MODEL), lambda i, j: (i, 0)),
            pl.BlockSpec((D_MODEL, tf), lambda i, j: (0, j)),
            pl.BlockSpec((tf, D_MODEL), lambda i, j: (j, 0)),
            _layer_spec((1, D_MODEL), li),
        ],
        out_specs=pl.BlockSpec((tm, D_MODEL), lambda i, j: (i, 0)),
        out_shape=jax.ShapeDtypeStruct((m, D_MODEL), F32),
        scratch_shapes=[pltpu.VMEM((tm, D_MODEL), F32), pltpu.SemaphoreType.DMA(())],
        compiler_params=_params(("arbitrary", "arbitrary"), VMEM_LIMIT_BIG),
        name="ffn",
    )(x, h, w1, w2, gpost)


def kernel(x, g_mix_pre, g_mix_post, w_in, gmlp_w, gmlp_b, diff_lam, diff_subln, conv_w, conv_b,
           conv_norm, w_out, g_ffn_pre, g_ffn_post, w_ff1, w_ff2):
    batch, seq, _ = x.shape
    depth = w_in.shape[0]
    xf = x.reshape(batch * seq, D_MODEL)
    rows = lambda v: v.reshape(depth, 1, -1)
    g_mix_pre, g_mix_post, g_ffn_pre, g_ffn_post = map(
        rows, (g_mix_pre, g_mix_post, g_ffn_pre, g_ffn_post))
    conv_b, conv_norm = rows(conv_b), rows(conv_norm)
    subln = diff_subln.reshape(depth, -1, 1)
    gmlp_w_t = gmlp_w.transpose(0, 2, 1, 3).reshape(depth, GMLP_CHUNK, GMLP_GROUPS * GMLP_CHUNK)
    gmlp_bias = jnp.repeat(gmlp_b.transpose(0, 2, 1), GROUP_WIDTH // GMLP_GROUPS, axis=2)
    conv_w8 = jnp.broadcast_to(conv_w[:, :, None, :],
                               (depth, CONV_WIDTH, SUBLANES, GROUP_WIDTH))
    w_in_b = w_in[0].astype(BF16)
    for li in range(depth):
        of, ob = _in_proj(xf, g_mix_pre, w_in_b, li)
        out_a = _gmlp(of, gmlp_w_t, gmlp_bias, li)

        lam_init = 0.8 - 0.6 * float(np.exp(-0.3 * li))
        out_b, w_ff1_b = _diff_attention(ob, diff_lam, subln, lam_init, batch, seq, w_ff1, li)
        out_c, w_ff2_b = _dil_attention(ob, batch, seq, w_ff2, li)
        casts = [(w_out, li)] + ([(w_in, li + 1)] if li + 1 < depth else [])
        out_d, w_out_b, *w_next = _conv(of, conv_w8, conv_b, conv_norm, batch, seq, casts, li)
        if w_next:
            w_in_b = w_next[0]

        xf, h_ffn = _out_proj((out_a, out_b, out_c, out_d), w_out_b, xf, g_mix_post, g_ffn_pre,
                              li)
        xf = _ffn(xf, h_ffn, w_ff1_b, w_ff2_b, g_ffn_post, li)
    return xf.reshape(batch, seq, D_MODEL)
```

```python
import functools

import numpy as np
import jax
import jax.numpy as jnp
from jax import lax
from jax.experimental import pallas as pl
from jax.experimental.pallas import tpu as pltpu

F32 = jnp.float32
BF16 = jnp.bfloat16

D_MODEL = 2048
GROUP_WIDTH = 512
N_IN_SPLITS = 10
D_FF = 4 * D_MODEL
NORM_EPS = 1e-6
GMLP_CHUNK = 128
GMLP_GROUPS = 8
DIFF_HEADS = 4
DIL_HEADS = 8
HEAD_DIM = 64
DIL_PATTERNS = ((128, 1), (512, 4), (2048, 16))
DIL_BLOCK = 128
CONV_WIDTH = 31
N_ALIBI_HEADS = DIFF_HEADS + DIL_HEADS
ATTN_SCALE = HEAD_DIM ** -0.5
MASKED = -1e30
LOG2E = 1.4426950408889634
DIFF_ONES_ROWS = 16
DIFF_HEAD_GROUP = 4

LANES = 128
VMEM_LIMIT = 48 * 1024 * 1024
VMEM_LIMIT_BIG = 60 * 1024 * 1024

TM_PROJ = 1024
TM_OUT = 512
OUT_ROW_CHUNK = 256
TM_FFN = 1024
TF_FFN = 1024
FFN_MID_CHUNK = 512
FFN_OUT_CHUNK = 512
NORM_ROWS = 16
GMLP_ROWS = 1024
DIFF_TQ = 256
DIFF_TK = 256
CONV_ROWS = 512
CONV_HALO = 32
CONV_CHUNK = 128
SUBLANES = 8


def _alibi_slopes():
    i = np.arange(1, N_ALIBI_HEADS + 1, dtype=np.float64)
    s = 2.0 ** (-8.0 * i / N_ALIBI_HEADS)
    diff_idx = np.arange(0, N_ALIBI_HEADS, 3)
    dil_idx = np.array([j for j in range(N_ALIBI_HEADS) if j % 3 != 0])
    return s[diff_idx], s[dil_idx]


def _rms(x, g):
    return x * lax.rsqrt(jnp.mean(x * x, axis=-1, keepdims=True) + NORM_EPS) * g


def _params(sem, limit=VMEM_LIMIT):
    return pltpu.CompilerParams(dimension_semantics=sem, vmem_limit_bytes=limit)


def _layer_spec(shape, li):
    return pl.BlockSpec((None, *shape), lambda *ids: (li,) + (0,) * len(shape))


def _whole_spec(arr):
    return pl.BlockSpec(arr.shape, lambda *ids: (0,) * arr.ndim)


def _proj_kernel(x_hbm, g_ref, w_ref, of_ref, ob_ref, h_s, x_s, x_sem, *, li):
    i, j = pl.program_id(0), pl.program_id(1)
    nf, nb = 2 * GROUP_WIDTH, 3 * GROUP_WIDTH

    def x_copy(block):
        return pltpu.make_async_copy(x_hbm.at[pl.ds(block * TM_PROJ, TM_PROJ), :], x_s, x_sem)

    def project(col0, ncols, write):
        for c in range(ncols // GROUP_WIDTH):
            cols = slice(col0 + c * GROUP_WIDTH, col0 + (c + 1) * GROUP_WIDTH)
            write(slice(c * GROUP_WIDTH, (c + 1) * GROUP_WIDTH),
                  jnp.dot(h_s[...], w_ref[:, cols], preferred_element_type=F32))

    def write_f32(cols, y):
        of_ref[:, cols] = y

    def write_bf16(cols, y):
        ob_ref[:, cols] = y.astype(BF16)

    @pl.when(j == 0)
    def _():
        @pl.when(i == 0)
        def _():
            x_copy(0).start()
        x_copy(i).wait()
        h_s[...] = _rms(x_s[...], g_ref[pl.ds(li, 1), :]).astype(BF16)

        @pl.when(i + 1 < pl.num_programs(0))
        def _():
            x_copy(i + 1).start()
        project(0, nf, write_f32)
        project(nf, nb, write_bf16)

    @pl.when(j == 1)
    def _():
        project(0, nb, write_bf16)
        project(nb, nf, write_f32)


def _in_proj(x, g, w, li):
    m = x.shape[0]
    tm = TM_PROJ
    half = N_IN_SPLITS * GROUP_WIDTH // 2
    return pl.pallas_call(
        functools.partial(_proj_kernel, li=li),
        grid=(m // tm, 2),
        in_specs=[
            pl.BlockSpec(memory_space=pl.ANY),
            _whole_spec(g),
            pl.BlockSpec((D_MODEL, half), lambda i, j: (0, j)),
        ],
        out_specs=[
            pl.BlockSpec((tm, 2 * GROUP_WIDTH), lambda i, j: (i, j)),
            pl.BlockSpec((tm, 3 * GROUP_WIDTH), lambda i, j: (i, j)),
        ],
        out_shape=[
            jax.ShapeDtypeStruct((m, 4 * GROUP_WIDTH), F32),
            jax.ShapeDtypeStruct((m, 6 * GROUP_WIDTH), BF16),
        ],
        scratch_shapes=[pltpu.VMEM((tm, D_MODEL), BF16), pltpu.VMEM((tm, D_MODEL), F32),
                        pltpu.SemaphoreType.DMA(())],
        compiler_params=_params(("arbitrary", "arbitrary"), VMEM_LIMIT_BIG),
        name="in_proj",
    )(x, g, w)


def _gelu(x):
    return jax.nn.gelu(x, approximate=True)


def _gmlp_kernel(u_ref, v_ref, w_ref, b_ref, o_ref):
    c = GMLP_CHUNK
    t_idx = lax.broadcasted_iota(jnp.int32, w_ref.shape, 0)
    s_idx = lax.broadcasted_iota(jnp.int32, w_ref.shape, 1) % c
    w = jnp.where(s_idx <= t_idx, w_ref[...], 0.0).astype(BF16)
    bias = b_ref[...]
    lane = lax.broadcasted_iota(jnp.int32, (c, LANES), 1)
    low = lane < HEAD_DIM
    for ci in range(GMLP_ROWS // c):
        rows = slice(ci * c, (ci + 1) * c)
        u = _gelu(u_ref[rows, :])
        v = _gelu(v_ref[rows, :])
        mu = jnp.mean(v, axis=-1, keepdims=True)
        vc = v - mu
        v = vc * lax.rsqrt(jnp.mean(vc * vc, axis=-1, keepdims=True) + NORM_EPS)
        zs = []
        for p in range(GMLP_GROUPS // 2):
            vp = v[:, p * LANES:(p + 1) * LANES]
            rhs = jnp.concatenate([jnp.where(low, vp, 0.0), jnp.where(low, 0.0, vp)], axis=0)
            zs.append(jnp.dot(w[:, p * 2 * c:(p + 1) * 2 * c], rhs.astype(BF16),
                              preferred_element_type=F32))
        z = jnp.concatenate(zs, axis=1) + bias
        o_ref[rows, :] = (u * z).astype(BF16)


def _gmlp(of, w_t, bias_full, li):
    m = of.shape[0]
    tr = GMLP_ROWS
    return pl.pallas_call(
        _gmlp_kernel,
        grid=(m // tr,),
        in_specs=[
            pl.BlockSpec((tr, GROUP_WIDTH), lambda i: (i, 0)),
            pl.BlockSpec((tr, GROUP_WIDTH), lambda i: (i, 1)),
            _layer_spec((GMLP_CHUNK, GMLP_GROUPS * GMLP_CHUNK), li),
            _layer_spec((GMLP_CHUNK, GROUP_WIDTH), li),
        ],
        out_specs=pl.BlockSpec((tr, GROUP_WIDTH), lambda i: (i, 0)),
        out_shape=jax.ShapeDtypeStruct((m, GROUP_WIDTH), BF16),
        compiler_params=_params(("parallel",)),
        name="gmlp",
    )(of, of, w_t, bias_full)


def _stack_queries(q):
    lane = lax.broadcasted_iota(jnp.int32, q.shape, 1)
    low = lane < HEAD_DIM
    zero = jnp.zeros_like(q)
    q2 = jnp.concatenate([jnp.where(low, q, zero), jnp.where(low, zero, q)], axis=0)
    return (q2.astype(F32) * (ATTN_SCALE * LOG2E)).astype(BF16)


def _qk(q2, k):
    return lax.dot_general(q2, k, (((1,), (1,)), ((), ())), preferred_element_type=F32)


def _cast_specs(w, li, nsteps, step_of):
    _, rows, cols = w.shape
    slab = rows // nsteps
    assert slab * nsteps == rows and slab % 16 == 0
    in_spec = pl.BlockSpec((None, slab, cols), lambda *ids: (li, step_of(*ids), 0))
    out_spec = pl.BlockSpec((slab, cols), lambda *ids: (step_of(*ids), 0))
    return in_spec, out_spec, jax.ShapeDtypeStruct((rows, cols), BF16)


def _diff_kernel(q_ref, k_ref, v_ref, bias_ref, lam_ref, g_ref, wf_ref, o_ref, wb_ref,
                 vt_s, acc_s, *, lam_init, slopes):
    wb_ref[...] = wf_ref[...].astype(BF16)
    qi = pl.program_id(1)
    tq, tk = DIFF_TQ, DIFF_TK
    seq = k_ref.shape[0]
    heads = range(DIFF_HEADS)
    hcols = lambda h: slice(h * LANES, (h + 1) * LANES)

    @pl.when(qi == 0)
    def _():
        for h in heads:
            for c in range(seq // tk):
                rows = slice(c * tk, (c + 1) * tk)
                vt_s[h, 0:LANES, rows] = v_ref[rows, hcols(h)].astype(F32).T.astype(BF16)
            vt_s[h, LANES:, :] = jnp.ones((DIFF_ONES_ROWS, seq), BF16)

    q2 = [_stack_queries(q_ref[:, hcols(h)]) for h in heads]
    acc_s[...] = jnp.zeros(acc_s.shape, F32)

    def step_group(hs, start, which, dist, m_prev):
        if which == 0:
            bias = {h: jnp.concatenate([bias_ref[h, 0, :, 0:LANES]] * (2 * tq // LANES), axis=1)
                    for h in hs}
        else:
            bias = {h: bias_ref[h, 1] for h in hs}
        sts = {h: _qk(k_ref[pl.ds(start, tk), hcols(h)], q2[h]) + bias[h]
               for h in hs}
        shifts = {h: (-slopes[h] * LOG2E) * dist for h in hs}
        m_next = {h: jnp.maximum(m_prev[h], jnp.max(sts[h], axis=0, keepdims=True) + shifts[h])
                  for h in hs}
        ps = {h: jnp.exp2(sts[h] - (m_next[h] - shifts[h])).astype(BF16) for h in hs}
        alphas = {h: jnp.exp2(m_prev[h] - m_next[h]) for h in hs}
        pvs = {h: jnp.dot(vt_s[h, :, pl.ds(start, tk)], ps[h], preferred_element_type=F32)
               for h in hs}
        for h in hs:
            acc_s[h] = alphas[h] * acc_s[h] + pvs[h]
        return m_next

    def step(start, which, dist, m_prev):
        m_next = {}
        for g in range(0, DIFF_HEADS, DIFF_HEAD_GROUP):
            m_next.update(step_group(range(g, g + DIFF_HEAD_GROUP), start, which, dist, m_prev))
        return tuple(m_next[h] for h in heads)

    def body(j, m_prev):
        return step(pl.multiple_of(j * tk, tk), 0, ((qi - j) * tq).astype(F32), m_prev)

    m0 = jnp.full((1, 2 * tq), MASKED, F32)
    m_prev = lax.fori_loop(0, qi, body, tuple(m0 for _ in heads))
    step(pl.multiple_of(qi * tq, tq), 1, 0.0, m_prev)

    lp = lam_ref[...]
    lam = (jnp.exp(jnp.sum(lp[0:1] * lp[1:2], axis=1, keepdims=True))
           - jnp.exp(jnp.sum(lp[2:3] * lp[3:4], axis=1, keepdims=True)) + lam_init)
    for h in heads:
        o = acc_s[h, 0:LANES, :] / acc_s[h, LANES:LANES + 1, :]
        o = o[:, :tq] - lam * o[:, tq:]
        y = o * lax.rsqrt(jnp.mean(o * o, axis=0, keepdims=True) + NORM_EPS) * g_ref[...]
        o_ref[:, hcols(h)] = (y * (1.0 - lam_init)).T.astype(BF16)


def _diff_bias_tiles(slopes):
    tq, tk = DIFF_TQ, DIFF_TK
    rq = (np.arange(2 * tq) % tq)[None, :].astype(np.float64)
    c = np.arange(tk)[:, None].astype(np.float64)
    tiles = np.zeros((DIFF_HEADS, 2, tk, 2 * tq), np.float64)
    for h, sl in enumerate(slopes):
        tiles[h, 0] = sl * LOG2E * c + 0.0 * rq
        tiles[h, 1] = np.where(rq - c >= 0, sl * LOG2E * c, MASKED)
    return tiles.astype(np.float32)


def _diff_attention(ob, lam_p, subln_g, lam_init, batch, seq, w_cast, li):
    m = ob.shape[0]
    tq, tk = DIFF_TQ, DIFF_TK
    assert tq == tk and seq % tq == 0
    nq = seq // tq
    slopes, _ = _alibi_slopes()
    bias = jnp.asarray(_diff_bias_tiles(slopes))
    kern = functools.partial(_diff_kernel, lam_init=lam_init,
                             slopes=tuple(float(s) for s in slopes))
    w_in_spec, w_out_spec, w_shape = _cast_specs(w_cast, li, batch * nq, lambda b, i: b * nq + i)
    return pl.pallas_call(
        kern,
        grid=(batch, nq),
        in_specs=[
            pl.BlockSpec((tq, GROUP_WIDTH), lambda b, i: (b * nq + i, 0)),
            pl.BlockSpec((seq, GROUP_WIDTH), lambda b, i: (b, 1)),
            pl.BlockSpec((seq, GROUP_WIDTH), lambda b, i: (b, 2)),
            pl.BlockSpec((DIFF_HEADS, 2, tk, 2 * tq), lambda b, i: (0, 0, 0, 0)),
            _layer_spec((4, HEAD_DIM), li),
            _layer_spec((2 * HEAD_DIM, 1), li),
            w_in_spec,
        ],
        out_specs=[pl.BlockSpec((tq, GROUP_WIDTH), lambda b, i: (b * nq + i, 0)), w_out_spec],
        out_shape=[jax.ShapeDtypeStruct((m, GROUP_WIDTH), BF16), w_shape],
        scratch_shapes=[
            pltpu.VMEM((DIFF_HEADS, LANES + DIFF_ONES_ROWS, seq), BF16),
            pltpu.VMEM((DIFF_HEADS, LANES + DIFF_ONES_ROWS, 2 * tq), F32),
        ],
        compiler_params=_params(("parallel", "arbitrary")),
        name="diff_attn",
    )(ob, ob, ob, bias, lam_p, subln_g, w_cast)


def _dil_kernel(q_ref, k_ref, v_ref, bias_ref, wf_ref, o_ref, wb_ref, qf, kf, vf, m_s, l_s, o_s,
                *, seq):
    wb_ref[...] = wf_ref[...].astype(BF16)
    n = DIL_BLOCK
    qf[...] = q_ref[...].astype(F32)
    kf[...] = k_ref[...].astype(F32)
    vf[...] = v_ref[...].astype(F32)
    lane = lax.broadcasted_iota(jnp.int32, (n, LANES), 1)
    low = lane < HEAD_DIM

    for pi, (window, dil) in enumerate(DIL_PATTERNS):
        nblk = seq // (n * dil)
        for r in range(dil):
            for c in range(nblk):
                cur = pl.ds(r + dil * n * c, n, stride=dil)
                q2 = _stack_queries(qf[cur, :])
                if c > 0:
                    prev = pl.ds(r + dil * n * (c - 1), n, stride=dil)
                    kb = jnp.concatenate([kf[prev, :], kf[cur, :]], axis=0).astype(BF16)
                    vb = jnp.concatenate([vf[prev, :], vf[cur, :]], axis=0).astype(BF16)
                    bias = bias_ref[pi]
                else:
                    kb = kf[cur, :].astype(BF16)
                    vb = vf[cur, :].astype(BF16)
                    bias = bias_ref[pi, :, n:]
                s = _qk(q2, kb) + bias
                mx = jnp.max(s, axis=1, keepdims=True)
                p = jnp.exp2(s - mx).astype(BF16)
                vb1 = jnp.concatenate([vb, jnp.ones(vb.shape, BF16)], axis=1)
                ol = jnp.dot(p, vb1, preferred_element_type=F32)
                o, l = ol[:, :LANES], ol[:, LANES:]
                o_s[pi, cur, :] = jnp.where(low, o[:n], o[n:])
                m_s[pi, cur, :] = jnp.where(low, mx[:n], mx[n:])
                l_s[pi, cur, :] = jnp.where(low, l[:n], l[n:])

    m_all = jnp.maximum(jnp.maximum(m_s[0], m_s[1]), m_s[2])
    num = jnp.zeros((seq, LANES), F32)
    den = jnp.zeros((seq, LANES), F32)
    for pi in range(len(DIL_PATTERNS)):
        w = jnp.exp2(m_s[pi] - m_all)
        num = num + w * o_s[pi]
        den = den + w * l_s[pi]
    o_ref[...] = (num / den).astype(BF16)


def _dil_bias_tiles(slopes):
    n = DIL_BLOCK
    qi = (np.arange(2 * n) % n)[:, None]
    ki = np.arange(2 * n)[None, :]
    step = n + qi - ki
    valid = (step >= 0) & (step <= n)
    tiles = np.zeros((len(DIL_PATTERNS), DIL_HEADS // 2, 2 * n, 2 * n), np.float64)
    for pi, (_, dil) in enumerate(DIL_PATTERNS):
        for p in range(DIL_HEADS // 2):
            sl = np.where(np.arange(2 * n) < n, slopes[2 * p], slopes[2 * p + 1])[:, None]
            tiles[pi, p] = np.where(valid, -sl * LOG2E * (step * dil), MASKED)
    return tiles.astype(np.float32)


def _dil_attention(ob, batch, seq, w_cast, li):
    m = ob.shape[0]
    _, slopes = _alibi_slopes()
    bias = jnp.asarray(_dil_bias_tiles(slopes))
    npairs = DIL_HEADS // 2
    npat = len(DIL_PATTERNS)
    col0 = 3 * GROUP_WIDTH // LANES
    kern = functools.partial(_dil_kernel, seq=seq)
    w_in_spec, w_out_spec, w_shape = _cast_specs(w_cast, li, batch * npairs,
                                                 lambda b, p: b * npairs + p)
    return pl.pallas_call(
        kern,
        grid=(batch, npairs),
        in_specs=[
            pl.BlockSpec((seq, LANES), lambda b, p: (b, col0 + p)),
            pl.BlockSpec((seq, LANES), lambda b, p: (b, col0 + npairs + p)),
            pl.BlockSpec((seq, LANES), lambda b, p: (b, col0 + 2 * npairs + p)),
            pl.BlockSpec((npat, None, 2 * DIL_BLOCK, 2 * DIL_BLOCK), lambda b, p: (0, p, 0, 0)),
            w_in_spec,
        ],
        out_specs=[pl.BlockSpec((seq, LANES), lambda b, p: (b, p)), w_out_spec],
        out_shape=[jax.ShapeDtypeStruct((m, GROUP_WIDTH), BF16), w_shape],
        scratch_shapes=[
            pltpu.VMEM((seq, LANES), F32),
            pltpu.VMEM((seq, LANES), F32),
            pltpu.VMEM((seq, LANES), F32),
            pltpu.VMEM((npat, seq, LANES), F32),
            pltpu.VMEM((npat, seq, LANES), F32),
            pltpu.VMEM((npat, seq, LANES), F32),
        ],
        compiler_params=_params(("parallel", "parallel")),
        name="dil_attn",
    )(ob, ob, ob, bias, w_cast)


def _conv_kernel(a_ref, g_ref, ap_ref, gp_ref, w_ref, b_ref, n_ref, *rest, ncast, li):
    wf_refs, o_ref, wb_refs = rest[:ncast], rest[ncast], rest[ncast + 1:2 * ncast + 1]
    h_s, y_s = rest[2 * ncast + 1:]
    for wf_ref, wb_ref in zip(wf_refs, wb_refs):
        wb_ref[...] = wf_ref[...].astype(BF16)
    i = pl.program_id(1)
    tr, halo = CONV_ROWS, CONV_HALO
    hp = ap_ref[...] * jax.nn.sigmoid(gp_ref[...])
    h_s[0:halo, :] = jnp.where(i > 0, hp, 0.0)
    h_s[halo:, :] = a_ref[...] * jax.nn.sigmoid(g_ref[...])
    _conv_rows(h_s, w_ref, b_ref.at[pl.ds(li, 1)], n_ref.at[pl.ds(li, 1)], y_s, o_ref, 0, tr)


def _conv_rows(h_s, w_ref, b_ref, n_ref, y_s, o_ref, row0, nrows):
    rc, sub, last, halo = CONV_CHUNK, SUBLANES, CONV_WIDTH - 1, CONV_HALO
    for lt in range(GROUP_WIDTH // LANES):
        lanes = slice(lt * LANES, (lt + 1) * LANES)
        for c in range(nrows // rc):
            base = row0 + c * rc
            y = jnp.zeros((rc, LANES), F32)
            for r in range(sub):
                a_r = None
                for q in range((last - r) // sub + 1):
                    d = sub * q + r
                    start = base + halo - sub - sub * q
                    win = h_s[start:start + rc + sub, lanes]
                    term = w_ref[last - d, :, lanes] * win.reshape(rc // sub + 1, sub, LANES)
                    a_r = term if a_r is None else a_r + term
                y = y + a_r.reshape(rc + sub, LANES)[sub - r:sub - r + rc, :]
            y_s[c * rc:(c + 1) * rc, lanes] = y
    y = _rms(y_s[0:nrows, :] + b_ref[...], n_ref[...])
    o_ref[row0:row0 + nrows, :] = (y * jax.nn.sigmoid(y)).astype(BF16)


def _conv(of, w, b, g, batch, seq, casts, li):
    m = of.shape[0]
    tr, halo = CONV_ROWS, CONV_HALO
    nt = seq // tr
    per = tr // halo
    cur = lambda col: (lambda bb, i: (bb * nt + i, col))
    prev = lambda col: (lambda bb, i: (jnp.maximum((bb * nt + i) * per - 1, 0), col))
    cast_specs = [_cast_specs(wc, lc, batch * nt, lambda bb, i: bb * nt + i) for wc, lc in casts]
    return pl.pallas_call(
        functools.partial(_conv_kernel, ncast=len(casts), li=li),
        grid=(batch, nt),
        in_specs=[
            pl.BlockSpec((tr, GROUP_WIDTH), cur(2)),
            pl.BlockSpec((tr, GROUP_WIDTH), cur(3)),
            pl.BlockSpec((halo, GROUP_WIDTH), prev(2)),
            pl.BlockSpec((halo, GROUP_WIDTH), prev(3)),
            _layer_spec((CONV_WIDTH, SUBLANES, GROUP_WIDTH), li),
            _whole_spec(b),
            _whole_spec(g),
            *[cs[0] for cs in cast_specs],
        ],
        out_specs=[pl.BlockSpec((tr, GROUP_WIDTH), lambda bb, i: (bb * nt + i, 0)),
                   *[cs[1] for cs in cast_specs]],
        out_shape=[jax.ShapeDtypeStruct((m, GROUP_WIDTH), BF16), *[cs[2] for cs in cast_specs]],
        scratch_shapes=[pltpu.VMEM((halo + tr, GROUP_WIDTH), F32),
                        pltpu.VMEM((tr, GROUP_WIDTH), F32)],
        compiler_params=_params(("parallel", "parallel")),
        name="conformer_conv",
    )(of, of, of, of, w, b, g, *[wc for wc, _ in casts])


def _out_kernel(a_ref, b_ref, c_ref, d_ref, w_ref, x_ref, g_ref, gn_ref, o_ref, h_ref, *, li):
    gw = GROUP_WIDTH
    g, gn = g_ref[pl.ds(li, 1), :], gn_ref[pl.ds(li, 1), :]
    for c in range(TM_OUT // OUT_ROW_CHUNK):
        rows = slice(c * OUT_ROW_CHUNK, (c + 1) * OUT_ROW_CHUNK)
        y = jnp.dot(a_ref[rows, :], w_ref[0:gw, :], preferred_element_type=F32)
        y = y + jnp.dot(b_ref[rows, :], w_ref[gw:2 * gw, :], preferred_element_type=F32)
        y = y + jnp.dot(c_ref[rows, :], w_ref[2 * gw:3 * gw, :], preferred_element_type=F32)
        y = y + jnp.dot(d_ref[rows, :], w_ref[3 * gw:4 * gw, :], preferred_element_type=F32)
        xn = x_ref[rows, :] + _rms(y, g)
        o_ref[rows, :] = xn
        h_ref[rows, :] = _rms(xn, gn).astype(BF16)


def _out_proj(mix, w, x, g, g_next, li):
    m = x.shape[0]
    tm = TM_OUT
    mix_spec = pl.BlockSpec((tm, GROUP_WIDTH), lambda i: (i, 0))
    row_spec = pl.BlockSpec((tm, D_MODEL), lambda i: (i, 0))
    gain_spec = _whole_spec(g)
    return pl.pallas_call(
        functools.partial(_out_kernel, li=li),
        grid=(m // tm,),
        in_specs=[mix_spec, mix_spec, mix_spec, mix_spec,
                  pl.BlockSpec((D_MODEL, D_MODEL), lambda i: (0, 0), pipeline_mode=pl.Buffered(1)),
                  row_spec, gain_spec, gain_spec],
        out_specs=[row_spec, row_spec],
        out_shape=[jax.ShapeDtypeStruct((m, D_MODEL), F32),
                   jax.ShapeDtypeStruct((m, D_MODEL), BF16)],
        compiler_params=_params(("parallel",)),
        name="out_proj",
    )(*mix, w, x, g, g_next)


def _ffn_kernel(x_hbm, h_ref, w1_ref, w2_ref, gpost_ref, o_ref, x_s, x_sem, *, li):
    i, j = pl.program_id(0), pl.program_id(1)
    x_copy = pltpu.make_async_copy(x_hbm.at[pl.ds(i * TM_FFN, TM_FFN), :], x_s, x_sem)

    def step(first):
        for cf in range(TF_FFN // FFN_MID_CHUNK):
            mid = slice(cf * FFN_MID_CHUNK, (cf + 1) * FFN_MID_CHUNK)
            f = jnp.dot(h_ref[...], w1_ref[:, mid], preferred_element_type=F32)
            f = jnp.square(jnp.maximum(f, 0.0)).astype(BF16)
            for c in range(D_MODEL // FFN_OUT_CHUNK):
                cols = slice(c * FFN_OUT_CHUNK, (c + 1) * FFN_OUT_CHUNK)
                y = jnp.dot(f, w2_ref[mid, cols], preferred_element_type=F32)
                if first and cf == 0:
                    o_ref[:, cols] = y
                else:
                    o_ref[:, cols] += y

    @pl.when(j == 0)
    def _():
        x_copy.start()
        step(True)

    @pl.when(j > 0)
    def _():
        step(False)

    @pl.when(j == pl.num_programs(1) - 1)
    def _():
        x_copy.wait()
        for r in range(TM_FFN // NORM_ROWS):
            rows = slice(r * NORM_ROWS, (r + 1) * NORM_ROWS)
            o_ref[rows, :] = x_s[rows, :] + _rms(o_ref[rows, :], gpost_ref[pl.ds(li, 1), :])


def _ffn(x, h, w1, w2, gpost, li):
    m = x.shape[0]
    tm, tf = TM_FFN, TF_FFN
    return pl.pallas_call(
        functools.partial(_ffn_kernel, li=li),
        grid=(m // tm, D_FF // tf),
        in_specs=[
            pl.BlockSpec(memory_space=pl.ANY),
            pl.BlockSpec((tm, D_MODEL), lambda i, j: (i, 0)),
            pl.BlockSpec((D_MODEL, tf), lambda i, j: (0, j)),
            pl.BlockSpec((tf, D_MODEL), lambda i, j: (j, 0)),
            _whole_spec(gpost),
        ],
        out_specs=pl.BlockSpec((tm, D_MODEL), lambda i, j: (i, 0)),
        out_shape=jax.ShapeDtypeStruct((m, D_MODEL), F32),
        scratch_shapes=[pltpu.VMEM((tm, D_MODEL), F32), pltpu.SemaphoreType.DMA(())],
        compiler_params=_params(("arbitrary", "arbitrary"), VMEM_LIMIT_BIG),
        name="ffn",
    )(x, h, w1, w2, gpost)


def kernel(x, g_mix_pre, g_mix_post, w_in, gmlp_w, gmlp_b, diff_lam, diff_subln, conv_w, conv_b,
           conv_norm, w_out, g_ffn_pre, g_ffn_post, w_ff1, w_ff2):
    batch, seq, _ = x.shape
    depth = w_in.shape[0]
    xf = x.reshape(batch * seq, D_MODEL)
    subln = diff_subln.reshape(depth, -1, 1)
    gmlp_w_t = gmlp_w.transpose(0, 2, 1, 3).reshape(depth, GMLP_CHUNK, GMLP_GROUPS * GMLP_CHUNK)
    gmlp_bias = jnp.repeat(gmlp_b.transpose(0, 2, 1), GROUP_WIDTH // GMLP_GROUPS, axis=2)
    conv_w8 = jnp.broadcast_to(conv_w[:, :, None, :],
                               (depth, CONV_WIDTH, SUBLANES, GROUP_WIDTH))
    w_in_b = w_in[0].astype(BF16)
    for li in range(depth):
        of, ob = _in_proj(xf, g_mix_pre, w_in_b, li)
        out_a = _gmlp(of, gmlp_w_t, gmlp_bias, li)

        lam_init = 0.8 - 0.6 * float(np.exp(-0.3 * li))
        out_b, w_ff1_b = _diff_attention(ob, diff_lam, subln, lam_init, batch, seq, w_ff1, li)
        out_c, w_ff2_b = _dil_attention(ob, batch, seq, w_ff2, li)
        casts = [(w_out, li)] + ([(w_in, li + 1)] if li + 1 < depth else [])
        out_d, w_out_b, *w_next = _conv(of, conv_w8, conv_b, conv_norm, batch, seq, casts, li)
        if w_next:
            w_in_b = w_next[0]

        xf, h_ffn = _out_proj((out_a, out_b, out_c, out_d), w_out_b, xf, g_mix_post, g_ffn_pre,
                              li)
        xf = _ffn(xf, h_ffn, w_ff1_b, w_ff2_b, g_ffn_post, li)
    return xf.reshape(batch, seq, D_MODEL)
```

```python
import functools

import numpy as np
import jax
import jax.numpy as jnp
from jax import lax
from jax.experimental import pallas as pl
from jax.experimental.pallas import tpu as pltpu

F32 = jnp.float32
BF16 = jnp.bfloat16

D_MODEL = 2048
GROUP_WIDTH = 512
N_IN_SPLITS = 10
D_FF = 4 * D_MODEL
NORM_EPS = 1e-6
GMLP_CHUNK = 128
GMLP_GROUPS = 8
DIFF_HEADS = 4
DIL_HEADS = 8
HEAD_DIM = 64
DIL_PATTERNS = ((128, 1), (512, 4), (2048, 16))
DIL_BLOCK = 128
CONV_WIDTH = 31
N_ALIBI_HEADS = DIFF_HEADS + DIL_HEADS
ATTN_SCALE = HEAD_DIM ** -0.5
MASKED = -1e30
LOG2E = 1.4426950408889634
DIFF_ONES_ROWS = 16
DIFF_HEAD_GROUP = 4

LANES = 128
VMEM_LIMIT = 48 * 1024 * 1024
VMEM_LIMIT_BIG = 60 * 1024 * 1024

TM_PROJ = 1024
TM_OUT = 512
OUT_ROW_CHUNK = 256
TM_FFN = 1024
TF_FFN = 1024
FFN_MID_CHUNK = 512
FFN_OUT_CHUNK = 512
NORM_ROWS = 16
GMLP_ROWS = 1024
DIFF_TQ = 512
DIFF_TK = 512
CONV_ROWS = 512
CONV_HALO = 32
CONV_CHUNK = 128
SUBLANES = 8


def _alibi_slopes():
    i = np.arange(1, N_ALIBI_HEADS + 1, dtype=np.float64)
    s = 2.0 ** (-8.0 * i / N_ALIBI_HEADS)
    diff_idx = np.arange(0, N_ALIBI_HEADS, 3)
    dil_idx = np.array([j for j in range(N_ALIBI_HEADS) if j % 3 != 0])
    return s[diff_idx], s[dil_idx]


def _rms(x, g):
    return x * lax.rsqrt(jnp.mean(x * x, axis=-1, keepdims=True) + NORM_EPS) * g


def _params(sem, limit=VMEM_LIMIT):
    return pltpu.CompilerParams(dimension_semantics=sem, vmem_limit_bytes=limit)


def _layer_spec(shape, li):
    return pl.BlockSpec((None, *shape), lambda *ids: (li,) + (0,) * len(shape))


def _proj_kernel(x_hbm, g_ref, w_ref, of_ref, ob_ref, h_s, x_s, x_sem):
    i, j = pl.program_id(0), pl.program_id(1)
    nf, nb = 2 * GROUP_WIDTH, 3 * GROUP_WIDTH

    def x_copy(block):
        return pltpu.make_async_copy(x_hbm.at[pl.ds(block * TM_PROJ, TM_PROJ), :], x_s, x_sem)

    def project(col0, ncols, write):
        for c in range(ncols // GROUP_WIDTH):
            cols = slice(col0 + c * GROUP_WIDTH, col0 + (c + 1) * GROUP_WIDTH)
            write(slice(c * GROUP_WIDTH, (c + 1) * GROUP_WIDTH),
                  jnp.dot(h_s[...], w_ref[:, cols], preferred_element_type=F32))

    def write_f32(cols, y):
        of_ref[:, cols] = y

    def write_bf16(cols, y):
        ob_ref[:, cols] = y.astype(BF16)

    @pl.when(j == 0)
    def _():
        @pl.when(i == 0)
        def _():
            x_copy(0).start()
        x_copy(i).wait()
        h_s[...] = _rms(x_s[...], g_ref[...]).astype(BF16)

        @pl.when(i + 1 < pl.num_programs(0))
        def _():
            x_copy(i + 1).start()
        project(0, nf, write_f32)
        project(nf, nb, write_bf16)

    @pl.when(j == 1)
    def _():
        project(0, nb, write_bf16)
        project(nb, nf, write_f32)


def _in_proj(x, g, w, li):
    m = x.shape[0]
    tm = TM_PROJ
    half = N_IN_SPLITS * GROUP_WIDTH // 2
    return pl.pallas_call(
        _proj_kernel,
        grid=(m // tm, 2),
        in_specs=[
            pl.BlockSpec(memory_space=pl.ANY),
            _layer_spec((1, D_MODEL), li),
            pl.BlockSpec((D_MODEL, half), lambda i, j: (0, j)),
        ],
        out_specs=[
            pl.BlockSpec((tm, 2 * GROUP_WIDTH), lambda i, j: (i, j)),
            pl.BlockSpec((tm, 3 * GROUP_WIDTH), lambda i, j: (i, j)),
        ],
        out_shape=[
            jax.ShapeDtypeStruct((m, 4 * GROUP_WIDTH), F32),
            jax.ShapeDtypeStruct((m, 6 * GROUP_WIDTH), BF16),
        ],
        scratch_shapes=[pltpu.VMEM((tm, D_MODEL), BF16), pltpu.VMEM((tm, D_MODEL), F32),
                        pltpu.SemaphoreType.DMA(())],
        compiler_params=_params(("arbitrary", "arbitrary"), VMEM_LIMIT_BIG),
        name="in_proj",
    )(x, g, w)


def _gelu(x):
    return jax.nn.gelu(x, approximate=True)


def _gmlp_kernel(u_ref, v_ref, w_ref, b_ref, o_ref):
    c = GMLP_CHUNK
    t_idx = lax.broadcasted_iota(jnp.int32, w_ref.shape, 0)
    s_idx = lax.broadcasted_iota(jnp.int32, w_ref.shape, 1) % c
    w = jnp.where(s_idx <= t_idx, w_ref[...], 0.0).astype(BF16)
    bias = b_ref[...]
    lane = lax.broadcasted_iota(jnp.int32, (c, LANES), 1)
    low = lane < HEAD_DIM
    for ci in range(GMLP_ROWS // c):
        rows = slice(ci * c, (ci + 1) * c)
        u = _gelu(u_ref[rows, :])
        v = _gelu(v_ref[rows, :])
        mu = jnp.mean(v, axis=-1, keepdims=True)
        vc = v - mu
        v = vc * lax.rsqrt(jnp.mean(vc * vc, axis=-1, keepdims=True) + NORM_EPS)
        zs = []
        for p in range(GMLP_GROUPS // 2):
            vp = v[:, p * LANES:(p + 1) * LANES]
            rhs = jnp.concatenate([jnp.where(low, vp, 0.0), jnp.where(low, 0.0, vp)], axis=0)
            zs.append(jnp.dot(w[:, p * 2 * c:(p + 1) * 2 * c], rhs.astype(BF16),
                              preferred_element_type=F32))
        z = jnp.concatenate(zs, axis=1) + bias
        o_ref[rows, :] = (u * z).astype(BF16)


def _gmlp(of, w_t, bias_full, li):
    m = of.shape[0]
    tr = GMLP_ROWS
    return pl.pallas_call(
        _gmlp_kernel,
        grid=(m // tr,),
        in_specs=[
            pl.BlockSpec((tr, GROUP_WIDTH), lambda i: (i, 0)),
            pl.BlockSpec((tr, GROUP_WIDTH), lambda i: (i, 1)),
            _layer_spec((GMLP_CHUNK, GMLP_GROUPS * GMLP_CHUNK), li),
            _layer_spec((GMLP_CHUNK, GROUP_WIDTH), li),
        ],
        out_specs=pl.BlockSpec((tr, GROUP_WIDTH), lambda i: (i, 0)),
        out_shape=jax.ShapeDtypeStruct((m, GROUP_WIDTH), BF16),
        compiler_params=_params(("parallel",)),
        name="gmlp",
    )(of, of, w_t, bias_full)


def _stack_queries(q):
    lane = lax.broadcasted_iota(jnp.int32, q.shape, 1)
    low = lane < HEAD_DIM
    zero = jnp.zeros_like(q)
    q2 = jnp.concatenate([jnp.where(low, q, zero), jnp.where(low, zero, q)], axis=0)
    return (q2.astype(F32) * (ATTN_SCALE * LOG2E)).astype(BF16)


def _qk(q2, k):
    return lax.dot_general(q2, k, (((1,), (1,)), ((), ())), preferred_element_type=F32)


def _cast_specs(w, li, nsteps, step_of):
    _, rows, cols = w.shape
    slab = rows // nsteps
    assert slab * nsteps == rows and slab % 16 == 0
    in_spec = pl.BlockSpec((None, slab, cols), lambda *ids: (li, step_of(*ids), 0))
    out_spec = pl.BlockSpec((slab, cols), lambda *ids: (step_of(*ids), 0))
    return in_spec, out_spec, jax.ShapeDtypeStruct((rows, cols), BF16)


def _diff_kernel(q_ref, k_ref, v_ref, bkey_ref, bdiag_ref, lam_ref, g_ref, wf_ref, o_ref, wb_ref,
                 vt_s, acc_s, *, lam_init, slopes):
    wb_ref[...] = wf_ref[...].astype(BF16)
    qi = pl.program_id(1)
    tq, tk = DIFF_TQ, DIFF_TK
    seq = k_ref.shape[0]
    heads = range(DIFF_HEADS)
    hcols = lambda h: slice(h * LANES, (h + 1) * LANES)

    @pl.when(qi == 0)
    def _():
        for h in heads:
            for c in range(seq // tk):
                rows = slice(c * tk, (c + 1) * tk)
                vt_s[h, 0:LANES, rows] = v_ref[rows, hcols(h)].astype(F32).T.astype(BF16)
            vt_s[h, LANES:, :] = jnp.ones((DIFF_ONES_ROWS, seq), BF16)

    q2 = [_stack_queries(q_ref[:, hcols(h)]) for h in heads]
    acc_s[...] = jnp.zeros(acc_s.shape, F32)

    def step_group(hs, start, which, dist, m_prev):
        if which == 0:
            bias = {h: jnp.concatenate([bkey_ref[h]] * (2 * tq // LANES), axis=1) for h in hs}
        else:
            bias = {h: jnp.concatenate([bdiag_ref[h]] * 2, axis=1) for h in hs}
        sts = {h: _qk(k_ref[pl.ds(start, tk), hcols(h)], q2[h]) + bias[h]
               for h in hs}
        shifts = {h: (-slopes[h] * LOG2E) * dist for h in hs}
        m_next = {h: jnp.maximum(m_prev[h], jnp.max(sts[h], axis=0, keepdims=True) + shifts[h])
                  for h in hs}
        ps = {h: jnp.exp2(sts[h] - (m_next[h] - shifts[h])).astype(BF16) for h in hs}
        alphas = {h: jnp.exp2(m_prev[h] - m_next[h]) for h in hs}
        pvs = {h: jnp.dot(vt_s[h, :, pl.ds(start, tk)], ps[h], preferred_element_type=F32)
               for h in hs}
        for h in hs:
            acc_s[h] = alphas[h] * acc_s[h] + pvs[h]
        return m_next

    def step(start, which, dist, m_prev):
        m_next = {}
        for g in range(0, DIFF_HEADS, DIFF_HEAD_GROUP):
            m_next.update(step_group(range(g, g + DIFF_HEAD_GROUP), start, which, dist, m_prev))
        return tuple(m_next[h] for h in heads)

    def body(j, m_prev):
        return step(pl.multiple_of(j * tk, tk), 0, ((qi - j) * tq).astype(F32), m_prev)

    m0 = jnp.full((1, 2 * tq), MASKED, F32)
    m_prev = lax.fori_loop(0, qi, body, tuple(m0 for _ in heads))
    step(pl.multiple_of(qi * tq, tq), 1, 0.0, m_prev)

    lp = lam_ref[...]
    lam = (jnp.exp(jnp.sum(lp[0:1] * lp[1:2], axis=1, keepdims=True))
           - jnp.exp(jnp.sum(lp[2:3] * lp[3:4], axis=1, keepdims=True)) + lam_init)
    for h in heads:
        o = acc_s[h, 0:LANES, :] / acc_s[h, LANES:LANES + 1, :]
        o = o[:, :tq] - lam * o[:, tq:]
        y = o * lax.rsqrt(jnp.mean(o * o, axis=0, keepdims=True) + NORM_EPS) * g_ref[...]
        o_ref[:, hcols(h)] = (y * (1.0 - lam_init)).T.astype(BF16)


def _diff_bias_tiles(slopes):
    tq, tk = DIFF_TQ, DIFF_TK
    rq = np.arange(tq)[None, :].astype(np.float64)
    c = np.arange(tk)[:, None].astype(np.float64)
    key = np.zeros((DIFF_HEADS, tk, LANES), np.float64)
    diag = np.zeros((DIFF_HEADS, tk, tq), np.float64)
    for h, sl in enumerate(slopes):
        key[h] = sl * LOG2E * c
        diag[h] = np.where(rq - c >= 0, sl * LOG2E * c, MASKED)
    return key.astype(np.float32), diag.astype(np.float32)


def _diff_attention(ob, lam_p, subln_g, lam_init, batch, seq, w_cast, li):
    m = ob.shape[0]
    tq, tk = DIFF_TQ, DIFF_TK
    assert tq == tk and seq % tq == 0
    nq = seq // tq
    slopes, _ = _alibi_slopes()
    bias_key, bias_diag = map(jnp.asarray, _diff_bias_tiles(slopes))
    kern = functools.partial(_diff_kernel, lam_init=lam_init,
                             slopes=tuple(float(s) for s in slopes))
    w_in_spec, w_out_spec, w_shape = _cast_specs(w_cast, li, batch * nq, lambda b, i: b * nq + i)
    return pl.pallas_call(
        kern,
        grid=(batch, nq),
        in_specs=[
            pl.BlockSpec((tq, GROUP_WIDTH), lambda b, i: (b * nq + i, 0)),
            pl.BlockSpec((seq, GROUP_WIDTH), lambda b, i: (b, 1)),
            pl.BlockSpec((seq, GROUP_WIDTH), lambda b, i: (b, 2)),
            pl.BlockSpec((DIFF_HEADS, tk, LANES), lambda b, i: (0, 0, 0)),
            pl.BlockSpec((DIFF_HEADS, tk, tq), lambda b, i: (0, 0, 0)),
            _layer_spec((4, HEAD_DIM), li),
            _layer_spec((2 * HEAD_DIM, 1), li),
            w_in_spec,
        ],
        out_specs=[pl.BlockSpec((tq, GROUP_WIDTH), lambda b, i: (b * nq + i, 0)), w_out_spec],
        out_shape=[jax.ShapeDtypeStruct((m, GROUP_WIDTH), BF16), w_shape],
        scratch_shapes=[
            pltpu.VMEM((DIFF_HEADS, LANES + DIFF_ONES_ROWS, seq), BF16),
            pltpu.VMEM((DIFF_HEADS, LANES + DIFF_ONES_ROWS, 2 * tq), F32),
        ],
        compiler_params=_params(("parallel", "arbitrary"), VMEM_LIMIT_BIG),
        name="diff_attn",
    )(ob, ob, ob, bias_key, bias_diag, lam_p, subln_g, w_cast)


def _dil_kernel(q_ref, k_ref, v_ref, bias_ref, wf_ref, o_ref, wb_ref, qf, kf, vf, m_s, l_s, o_s,
                *, seq):
    wb_ref[...] = wf_ref[...].astype(BF16)
    n = DIL_BLOCK
    qf[...] = q_ref[...].astype(F32)
    kf[...] = k_ref[...].astype(F32)
    vf[...] = v_ref[...].astype(F32)
    lane = lax.broadcasted_iota(jnp.int32, (n, LANES), 1)
    low = lane < HEAD_DIM

    for pi, (window, dil) in enumerate(DIL_PATTERNS):
        nblk = seq // (n * dil)
        for r in range(dil):
            for c in range(nblk):
                cur = pl.ds(r + dil * n * c, n, stride=dil)
                q2 = _stack_queries(qf[cur, :])
                if c > 0:
                    prev = pl.ds(r + dil * n * (c - 1), n, stride=dil)
                    kb = jnp.concatenate([kf[prev, :], kf[cur, :]], axis=0).astype(BF16)
                    vb = jnp.concatenate([vf[prev, :], vf[cur, :]], axis=0).astype(BF16)
                    bias = bias_ref[pi]
                else:
                    kb = kf[cur, :].astype(BF16)
                    vb = vf[cur, :].astype(BF16)
                    bias = bias_ref[pi, :, n:]
                s = _qk(q2, kb) + bias
                mx = jnp.max(s, axis=1, keepdims=True)
                p = jnp.exp2(s - mx).astype(BF16)
                vb1 = jnp.concatenate([vb, jnp.ones(vb.shape, BF16)], axis=1)
                ol = jnp.dot(p, vb1, preferred_element_type=F32)
                o, l = ol[:, :LANES], ol[:, LANES:]
                o_s[pi, cur, :] = jnp.where(low, o[:n], o[n:])
                m_s[pi, cur, :] = jnp.where(low, mx[:n], mx[n:])
                l_s[pi, cur, :] = jnp.where(low, l[:n], l[n:])

    m_all = jnp.maximum(jnp.maximum(m_s[0], m_s[1]), m_s[2])
    num = jnp.zeros((seq, LANES), F32)
    den = jnp.zeros((seq, LANES), F32)
    for pi in range(len(DIL_PATTERNS)):
        w = jnp.exp2(m_s[pi] - m_all)
        num = num + w * o_s[pi]
        den = den + w * l_s[pi]
    o_ref[...] = (num / den).astype(BF16)


def _dil_bias_tiles(slopes):
    n = DIL_BLOCK
    qi = (np.arange(2 * n) % n)[:, None]
    ki = np.arange(2 * n)[None, :]
    step = n + qi - ki
    valid = (step >= 0) & (step <= n)
    tiles = np.zeros((len(DIL_PATTERNS), DIL_HEADS // 2, 2 * n, 2 * n), np.float64)
    for pi, (_, dil) in enumerate(DIL_PATTERNS):
        for p in range(DIL_HEADS // 2):
            sl = np.where(np.arange(2 * n) < n, slopes[2 * p], slopes[2 * p + 1])[:, None]
            tiles[pi, p] = np.where(valid, -sl * LOG2E * (step * dil), MASKED)
    return tiles.astype(np.float32)


def _dil_attention(ob, batch, seq, w_cast, li):
    m = ob.shape[0]
    _, slopes = _alibi_slopes()
    bias = jnp.asarray(_dil_bias_tiles(slopes))
    npairs = DIL_HEADS // 2
    npat = len(DIL_PATTERNS)
    col0 = 3 * GROUP_WIDTH // LANES
    kern = functools.partial(_dil_kernel, seq=seq)
    w_in_spec, w_out_spec, w_shape = _cast_specs(w_cast, li, batch * npairs,
                                                 lambda b, p: b * npairs + p)
    return pl.pallas_call(
        kern,
        grid=(batch, npairs),
        in_specs=[
            pl.BlockSpec((seq, LANES), lambda b, p: (b, col0 + p)),
            pl.BlockSpec((seq, LANES), lambda b, p: (b, col0 + npairs + p)),
            pl.BlockSpec((seq, LANES), lambda b, p: (b, col0 + 2 * npairs + p)),
            pl.BlockSpec((npat, None, 2 * DIL_BLOCK, 2 * DIL_BLOCK), lambda b, p: (0, p, 0, 0)),
            w_in_spec,
        ],
        out_specs=[pl.BlockSpec((seq, LANES), lambda b, p: (b, p)), w_out_spec],
        out_shape=[jax.ShapeDtypeStruct((m, GROUP_WIDTH), BF16), w_shape],
        scratch_shapes=[
            pltpu.VMEM((seq, LANES), F32),
            pltpu.VMEM((seq, LANES), F32),
            pltpu.VMEM((seq, LANES), F32),
            pltpu.VMEM((npat, seq, LANES), F32),
            pltpu.VMEM((npat, seq, LANES), F32),
            pltpu.VMEM((npat, seq, LANES), F32),
        ],
        compiler_params=_params(("parallel", "parallel")),
        name="dil_attn",
    )(ob, ob, ob, bias, w_cast)


def _conv_kernel(a_ref, g_ref, ap_ref, gp_ref, w_ref, b_ref, n_ref, *rest, ncast):
    wf_refs, o_ref, wb_refs = rest[:ncast], rest[ncast], rest[ncast + 1:2 * ncast + 1]
    h_s, y_s = rest[2 * ncast + 1:]
    for wf_ref, wb_ref in zip(wf_refs, wb_refs):
        wb_ref[...] = wf_ref[...].astype(BF16)
    i = pl.program_id(1)
    tr, halo = CONV_ROWS, CONV_HALO
    hp = ap_ref[...] * jax.nn.sigmoid(gp_ref[...])
    h_s[0:halo, :] = jnp.where(i > 0, hp, 0.0)
    h_s[halo:, :] = a_ref[...] * jax.nn.sigmoid(g_ref[...])
    _conv_rows(h_s, w_ref, b_ref, n_ref, y_s, o_ref, 0, tr)


def _conv_rows(h_s, w_ref, b_ref, n_ref, y_s, o_ref, row0, nrows):
    rc, sub, last, halo = CONV_CHUNK, SUBLANES, CONV_WIDTH - 1, CONV_HALO
    for lt in range(GROUP_WIDTH // LANES):
        lanes = slice(lt * LANES, (lt + 1) * LANES)
        for c in range(nrows // rc):
            base = row0 + c * rc
            y = jnp.zeros((rc, LANES), F32)
            for r in range(sub):
                a_r = None
                for q in range((last - r) // sub + 1):
                    d = sub * q + r
                    start = base + halo - sub - sub * q
                    win = h_s[start:start + rc + sub, lanes]
                    term = w_ref[last - d, :, lanes] * win.reshape(rc // sub + 1, sub, LANES)
                    a_r = term if a_r is None else a_r + term
                y = y + a_r.reshape(rc + sub, LANES)[sub - r:sub - r + rc, :]
            y_s[c * rc:(c + 1) * rc, lanes] = y
    y = _rms(y_s[0:nrows, :] + b_ref[...], n_ref[...])
    o_ref[row0:row0 + nrows, :] = (y * jax.nn.sigmoid(y)).astype(BF16)


def _conv(of, w, b, g, batch, seq, casts, li):
    m = of.shape[0]
    tr, halo = CONV_ROWS, CONV_HALO
    nt = seq // tr
    per = tr // halo
    cur = lambda col: (lambda bb, i: (bb * nt + i, col))
    prev = lambda col: (lambda bb, i: (jnp.maximum((bb * nt + i) * per - 1, 0), col))
    cast_specs = [_cast_specs(wc, lc, batch * nt, lambda bb, i: bb * nt + i) for wc, lc in casts]
    return pl.pallas_call(
        functools.partial(_conv_kernel, ncast=len(casts)),
        grid=(batch, nt),
        in_specs=[
            pl.BlockSpec((tr, GROUP_WIDTH), cur(2)),
            pl.BlockSpec((tr, GROUP_WIDTH), cur(3)),
            pl.BlockSpec((halo, GROUP_WIDTH), prev(2)),
            pl.BlockSpec((halo, GROUP_WIDTH), prev(3)),
            _layer_spec((CONV_WIDTH, SUBLANES, GROUP_WIDTH), li),
            _layer_spec((1, GROUP_WIDTH), li),
            _layer_spec((1, GROUP_WIDTH), li),
            *[cs[0] for cs in cast_specs],
        ],
        out_specs=[pl.BlockSpec((tr, GROUP_WIDTH), lambda bb, i: (bb * nt + i, 0)),
                   *[cs[1] for cs in cast_specs]],
        out_shape=[jax.ShapeDtypeStruct((m, GROUP_WIDTH), BF16), *[cs[2] for cs in cast_specs]],
        scratch_shapes=[pltpu.VMEM((halo + tr, GROUP_WIDTH), F32),
                        pltpu.VMEM((tr, GROUP_WIDTH), F32)],
        compiler_params=_params(("parallel", "parallel")),
        name="conformer_conv",
    )(of, of, of, of, w, b, g, *[wc for wc, _ in casts])


def _out_kernel(a_ref, b_ref, c_ref, d_ref, w_ref, x_ref, g_ref, gn_ref, o_ref, h_ref):
    gw = GROUP_WIDTH
    for c in range(TM_OUT // OUT_ROW_CHUNK):
        rows = slice(c * OUT_ROW_CHUNK, (c + 1) * OUT_ROW_CHUNK)
        y = jnp.dot(a_ref[rows, :], w_ref[0:gw, :], preferred_element_type=F32)
        y = y + jnp.dot(b_ref[rows, :], w_ref[gw:2 * gw, :], preferred_element_type=F32)
        y = y + jnp.dot(c_ref[rows, :], w_ref[2 * gw:3 * gw, :], preferred_element_type=F32)
        y = y + jnp.dot(d_ref[rows, :], w_ref[3 * gw:4 * gw, :], preferred_element_type=F32)
        xn = x_ref[rows, :] + _rms(y, g_ref[...])
        o_ref[rows, :] = xn
        h_ref[rows, :] = _rms(xn, gn_ref[...]).astype(BF16)


def _out_proj(mix, w, x, g, g_next, li):
    m = x.shape[0]
    tm = TM_OUT
    mix_spec = pl.BlockSpec((tm, GROUP_WIDTH), lambda i: (i, 0))
    row_spec = pl.BlockSpec((tm, D_MODEL), lambda i: (i, 0))
    gain_spec = _layer_spec((1, D_MODEL), li)
    return pl.pallas_call(
        _out_kernel,
        grid=(m // tm,),
        in_specs=[mix_spec, mix_spec, mix_spec, mix_spec,
                  pl.BlockSpec((D_MODEL, D_MODEL), lambda i: (0, 0), pipeline_mode=pl.Buffered(1)),
                  row_spec, gain_spec, gain_spec],
        out_specs=[row_spec, row_spec],
        out_shape=[jax.ShapeDtypeStruct((m, D_MODEL), F32),
                   jax.ShapeDtypeStruct((m, D_MODEL), BF16)],
        compiler_params=_params(("parallel",)),
        name="out_proj",
    )(*mix, w, x, g, g_next)


def _ffn_kernel(x_hbm, h_ref, w1_ref, w2_ref, gpost_ref, o_ref, x_s, x_sem):
    i, j = pl.program_id(0), pl.program_id(1)
    x_copy = pltpu.make_async_copy(x_hbm.at[pl.ds(i * TM_FFN, TM_FFN), :], x_s, x_sem)

    def step(first):
        for cf in range(TF_FFN // FFN_MID_CHUNK):
            mid = slice(cf * FFN_MID_CHUNK, (cf + 1) * FFN_MID_CHUNK)
            f = jnp.dot(h_ref[...], w1_ref[:, mid], preferred_element_type=F32)
            f = jnp.square(jnp.maximum(f, 0.0)).astype(BF16)
            for c in range(D_MODEL // FFN_OUT_CHUNK):
                cols = slice(c * FFN_OUT_CHUNK, (c + 1) * FFN_OUT_CHUNK)
                y = jnp.dot(f, w2_ref[mid, cols], preferred_element_type=F32)
                if first and cf == 0:
                    o_ref[:, cols] = y
                else:
                    o_ref[:, cols] += y

    @pl.when(j == 0)
    def _():
        x_copy.start()
        step(True)

    @pl.when(j > 0)
    def _():
        step(False)

    @pl.when(j == pl.num_programs(1) - 1)
    def _():
        x_copy.wait()
        for r in range(TM_FFN // NORM_ROWS):
            rows = slice(r * NORM_ROWS, (r + 1) * NORM_ROWS)
            o_ref[rows, :] = x_s[rows, :] + _rms(o_ref[rows, :], gpost_ref[...])


def _ffn(x, h, w1, w2, gpost, li):
    m = x.shape[0]
    tm, tf = TM_FFN, TF_FFN
    return pl.pallas_call(
        _ffn_kernel,
        grid=(m // tm, D_FF // tf),
        in_specs=[
            pl.BlockSpec(memory_space=pl.ANY),
            pl.BlockSpec((tm, D_MODEL), lambda i, j: (i, 0)),
            pl.BlockSpec((D_MODEL, tf), lambda i, j: (0, j)),
            pl.BlockSpec((tf, D_MODEL), lambda i, j: (j, 0)),
            _layer_spec((1, D_MODEL), li),
        ],
        out_specs=pl.BlockSpec((tm, D_MODEL), lambda i, j: (i, 0)),
        out_shape=jax.ShapeDtypeStruct((m, D_MODEL), F32),
        scratch_shapes=[pltpu.VMEM((tm, D_MODEL), F32), pltpu.SemaphoreType.DMA(())],
        compiler_params=_params(("arbitrary", "arbitrary"), VMEM_LIMIT_BIG),
        name="ffn",
    )(x, h, w1, w2, gpost)


def kernel(x, g_mix_pre, g_mix_post, w_in, gmlp_w, gmlp_b, diff_lam, diff_subln, conv_w, conv_b,
           conv_norm, w_out, g_ffn_pre, g_ffn_post, w_ff1, w_ff2):
    batch, seq, _ = x.shape
    depth = w_in.shape[0]
    xf = x.reshape(batch * seq, D_MODEL)
    rows = lambda v: v.reshape(depth, 1, -1)
    g_mix_pre, g_mix_post, g_ffn_pre, g_ffn_post = map(
        rows, (g_mix_pre, g_mix_post, g_ffn_pre, g_ffn_post))
    conv_b, conv_norm = rows(conv_b), rows(conv_norm)
    subln = diff_subln.reshape(depth, -1, 1)
    gmlp_w_t = gmlp_w.transpose(0, 2, 1, 3).reshape(depth, GMLP_CHUNK, GMLP_GROUPS * GMLP_CHUNK)
    gmlp_bias = jnp.repeat(gmlp_b.transpose(0, 2, 1), GROUP_WIDTH // GMLP_GROUPS, axis=2)
    conv_w8 = jnp.broadcast_to(conv_w[:, :, None, :],
                               (depth, CONV_WIDTH, SUBLANES, GROUP_WIDTH))
    w_in_b = w_in[0].astype(BF16)
    for li in range(depth):
        of, ob = _in_proj(xf, g_mix_pre, w_in_b, li)
        out_a = _gmlp(of, gmlp_w_t, gmlp_bias, li)

        lam_init = 0.8 - 0.6 * float(np.exp(-0.3 * li))
        out_b, w_ff1_b = _diff_attention(ob, diff_lam, subln, lam_init, batch, seq, w_ff1, li)
        out_c, w_ff2_b = _dil_attention(ob, batch, seq, w_ff2, li)
        casts = [(w_out, li)] + ([(w_in, li + 1)] if li + 1 < depth else [])
        out_d, w_out_b, *w_next = _conv(of, conv_w8, conv_b, conv_norm, batch, seq, casts, li)
        if w_next:
            w_in_b = w_next[0]

        xf, h_ffn = _out_proj((out_a, out_b, out_c, out_d), w_out_b, xf, g_mix_post, g_ffn_pre,
                              li)
        xf = _ffn(xf, h_ffn, w_ff1_b, w_ff2_b, g_ffn_post, li)
    return xf.reshape(batch, seq, D_MODEL)
```

```python
import functools

import numpy as np
import jax
import jax.numpy as jnp
from jax import lax
from jax.experimental import pallas as pl
from jax.experimental.pallas import tpu as pltpu

F32 = jnp.float32
BF16 = jnp.bfloat16

D_MODEL = 2048
GROUP_WIDTH = 512
N_IN_SPLITS = 10
D_FF = 4 * D_MODEL
NORM_EPS = 1e-6
GMLP_CHUNK = 128
GMLP_GROUPS = 8
DIFF_HEADS = 4
DIL_HEADS = 8
HEAD_DIM = 64
DIL_PATTERNS = ((128, 1), (512, 4), (2048, 16))
DIL_BLOCK = 128
CONV_WIDTH = 31
N_ALIBI_HEADS = DIFF_HEADS + DIL_HEADS
ATTN_SCALE = HEAD_DIM ** -0.5
MASKED = -1e30
LOG2E = 1.4426950408889634
DIFF_ONES_ROWS = 16
DIFF_HEAD_GROUP = 2

LANES = 128
VMEM_LIMIT = 48 * 1024 * 1024
VMEM_LIMIT_BIG = 60 * 1024 * 1024

TM_PROJ = 1024
TM_OUT = 512
OUT_ROW_CHUNK = 256
TM_FFN = 1024
TF_FFN = 1024
FFN_MID_CHUNK = 512
FFN_OUT_CHUNK = 512
NORM_ROWS = 16
GMLP_ROWS = 1024
DIFF_TQ = 512
DIFF_TK = 512
CONV_ROWS = 512
CONV_HALO = 32
CONV_CHUNK = 128
SUBLANES = 8


def _alibi_slopes():
    i = np.arange(1, N_ALIBI_HEADS + 1, dtype=np.float64)
    s = 2.0 ** (-8.0 * i / N_ALIBI_HEADS)
    diff_idx = np.arange(0, N_ALIBI_HEADS, 3)
    dil_idx = np.array([j for j in range(N_ALIBI_HEADS) if j % 3 != 0])
    return s[diff_idx], s[dil_idx]


def _rms(x, g):
    return x * lax.rsqrt(jnp.mean(x * x, axis=-1, keepdims=True) + NORM_EPS) * g


def _params(sem, limit=VMEM_LIMIT):
    return pltpu.CompilerParams(dimension_semantics=sem, vmem_limit_bytes=limit)


def _layer_spec(shape, li):
    return pl.BlockSpec((None, *shape), lambda *ids: (li,) + (0,) * len(shape))


def _proj_kernel(x_hbm, g_ref, w_ref, of_ref, ob_ref, h_s, x_s, x_sem):
    i, j = pl.program_id(0), pl.program_id(1)
    nf, nb = 2 * GROUP_WIDTH, 3 * GROUP_WIDTH

    def x_copy(block):
        return pltpu.make_async_copy(x_hbm.at[pl.ds(block * TM_PROJ, TM_PROJ), :], x_s, x_sem)

    def project(col0, ncols, write):
        for c in range(ncols // GROUP_WIDTH):
            cols = slice(col0 + c * GROUP_WIDTH, col0 + (c + 1) * GROUP_WIDTH)
            write(slice(c * GROUP_WIDTH, (c + 1) * GROUP_WIDTH),
                  jnp.dot(h_s[...], w_ref[:, cols], preferred_element_type=F32))

    def write_f32(cols, y):
        of_ref[:, cols] = y

    def write_bf16(cols, y):
        ob_ref[:, cols] = y.astype(BF16)

    @pl.when(j == 0)
    def _():
        @pl.when(i == 0)
        def _():
            x_copy(0).start()
        x_copy(i).wait()
        h_s[...] = _rms(x_s[...], g_ref[...]).astype(BF16)

        @pl.when(i + 1 < pl.num_programs(0))
        def _():
            x_copy(i + 1).start()
        project(0, nf, write_f32)
        project(nf, nb, write_bf16)

    @pl.when(j == 1)
    def _():
        project(0, nb, write_bf16)
        project(nb, nf, write_f32)


def _in_proj(x, g, w, li):
    m = x.shape[0]
    tm = TM_PROJ
    half = N_IN_SPLITS * GROUP_WIDTH // 2
    return pl.pallas_call(
        _proj_kernel,
        grid=(m // tm, 2),
        in_specs=[
            pl.BlockSpec(memory_space=pl.ANY),
            _layer_spec((1, D_MODEL), li),
            pl.BlockSpec((D_MODEL, half), lambda i, j: (0, j)),
        ],
        out_specs=[
            pl.BlockSpec((tm, 2 * GROUP_WIDTH), lambda i, j: (i, j)),
            pl.BlockSpec((tm, 3 * GROUP_WIDTH), lambda i, j: (i, j)),
        ],
        out_shape=[
            jax.ShapeDtypeStruct((m, 4 * GROUP_WIDTH), F32),
            jax.ShapeDtypeStruct((m, 6 * GROUP_WIDTH), BF16),
        ],
        scratch_shapes=[pltpu.VMEM((tm, D_MODEL), BF16), pltpu.VMEM((tm, D_MODEL), F32),
                        pltpu.SemaphoreType.DMA(())],
        compiler_params=_params(("arbitrary", "arbitrary"), VMEM_LIMIT_BIG),
        name="in_proj",
    )(x, g, w)


def _gelu(x):
    return jax.nn.gelu(x, approximate=True)


def _gmlp_kernel(u_ref, v_ref, w_ref, b_ref, o_ref):
    c = GMLP_CHUNK
    t_idx = lax.broadcasted_iota(jnp.int32, w_ref.shape, 0)
    s_idx = lax.broadcasted_iota(jnp.int32, w_ref.shape, 1) % c
    w = jnp.where(s_idx <= t_idx, w_ref[...], 0.0).astype(BF16)
    bias = b_ref[...]
    lane = lax.broadcasted_iota(jnp.int32, (c, LANES), 1)
    low = lane < HEAD_DIM
    for ci in range(GMLP_ROWS // c):
        rows = slice(ci * c, (ci + 1) * c)
        u = _gelu(u_ref[rows, :])
        v = _gelu(v_ref[rows, :])
        mu = jnp.mean(v, axis=-1, keepdims=True)
        vc = v - mu
        v = vc * lax.rsqrt(jnp.mean(vc * vc, axis=-1, keepdims=True) + NORM_EPS)
        zs = []
        for p in range(GMLP_GROUPS // 2):
            vp = v[:, p * LANES:(p + 1) * LANES]
            rhs = jnp.concatenate([jnp.where(low, vp, 0.0), jnp.where(low, 0.0, vp)], axis=0)
            zs.append(jnp.dot(w[:, p * 2 * c:(p + 1) * 2 * c], rhs.astype(BF16),
                              preferred_element_type=F32))
        z = jnp.concatenate(zs, axis=1) + bias
        o_ref[rows, :] = (u * z).astype(BF16)


def _gmlp(of, w_t, bias_full, li):
    m = of.shape[0]
    tr = GMLP_ROWS
    return pl.pallas_call(
        _gmlp_kernel,
        grid=(m // tr,),
        in_specs=[
            pl.BlockSpec((tr, GROUP_WIDTH), lambda i: (i, 0)),
            pl.BlockSpec((tr, GROUP_WIDTH), lambda i: (i, 1)),
            _layer_spec((GMLP_CHUNK, GMLP_GROUPS * GMLP_CHUNK), li),
            _layer_spec((GMLP_CHUNK, GROUP_WIDTH), li),
        ],
        out_specs=pl.BlockSpec((tr, GROUP_WIDTH), lambda i: (i, 0)),
        out_shape=jax.ShapeDtypeStruct((m, GROUP_WIDTH), BF16),
        compiler_params=_params(("parallel",)),
        name="gmlp",
    )(of, of, w_t, bias_full)


def _stack_queries(q):
    lane = lax.broadcasted_iota(jnp.int32, q.shape, 1)
    low = lane < HEAD_DIM
    zero = jnp.zeros_like(q)
    q2 = jnp.concatenate([jnp.where(low, q, zero), jnp.where(low, zero, q)], axis=0)
    return (q2.astype(F32) * (ATTN_SCALE * LOG2E)).astype(BF16)


def _qk(q2, k):
    return lax.dot_general(q2, k, (((1,), (1,)), ((), ())), preferred_element_type=F32)


def _cast_specs(w, li, nsteps, step_of):
    _, rows, cols = w.shape
    slab = rows // nsteps
    assert slab * nsteps == rows and slab % 16 == 0
    in_spec = pl.BlockSpec((None, slab, cols), lambda *ids: (li, step_of(*ids), 0))
    out_spec = pl.BlockSpec((slab, cols), lambda *ids: (step_of(*ids), 0))
    return in_spec, out_spec, jax.ShapeDtypeStruct((rows, cols), BF16)


def _diff_kernel(q_ref, k_ref, v_ref, bkey_ref, bdiag_ref, lam_ref, g_ref, wf_ref, o_ref, wb_ref,
                 vt_s, acc_s, *, lam_init, slopes):
    wb_ref[...] = wf_ref[...].astype(BF16)
    qi = pl.program_id(1)
    tq, tk = DIFF_TQ, DIFF_TK
    seq = k_ref.shape[0]
    heads = range(DIFF_HEADS)
    hcols = lambda h: slice(h * LANES, (h + 1) * LANES)

    @pl.when(qi == 0)
    def _():
        for h in heads:
            for c in range(seq // tk):
                rows = slice(c * tk, (c + 1) * tk)
                vt_s[h, 0:LANES, rows] = v_ref[rows, hcols(h)].astype(F32).T.astype(BF16)
            vt_s[h, LANES:, :] = jnp.ones((DIFF_ONES_ROWS, seq), BF16)

    q2 = [_stack_queries(q_ref[:, hcols(h)]) for h in heads]
    acc_s[...] = jnp.zeros(acc_s.shape, F32)

    def step_group(hs, start, which, dist, m_prev):
        if which == 0:
            bias = {h: jnp.concatenate([bkey_ref[h]] * (2 * tq // LANES), axis=1) for h in hs}
        else:
            bias = {h: jnp.concatenate([bdiag_ref[h]] * 2, axis=1) for h in hs}
        sts = {h: _qk(k_ref[pl.ds(start, tk), hcols(h)], q2[h]) + bias[h]
               for h in hs}
        shifts = {h: (-slopes[h] * LOG2E) * dist for h in hs}
        m_next = {h: jnp.maximum(m_prev[h], jnp.max(sts[h], axis=0, keepdims=True) + shifts[h])
                  for h in hs}
        ps = {h: jnp.exp2(sts[h] - (m_next[h] - shifts[h])).astype(BF16) for h in hs}
        alphas = {h: jnp.exp2(m_prev[h] - m_next[h]) for h in hs}
        pvs = {h: jnp.dot(vt_s[h, :, pl.ds(start, tk)], ps[h], preferred_element_type=F32)
               for h in hs}
        for h in hs:
            acc_s[h] = alphas[h] * acc_s[h] + pvs[h]
        return m_next

    def step(start, which, dist, m_prev):
        m_next = {}
        for g in range(0, DIFF_HEADS, DIFF_HEAD_GROUP):
            m_next.update(step_group(range(g, g + DIFF_HEAD_GROUP), start, which, dist, m_prev))
        return tuple(m_next[h] for h in heads)

    def body(j, m_prev):
        return step(pl.multiple_of(j * tk, tk), 0, ((qi - j) * tq).astype(F32), m_prev)

    m0 = jnp.full((1, 2 * tq), MASKED, F32)
    m_prev = lax.fori_loop(0, qi, body, tuple(m0 for _ in heads))
    step(pl.multiple_of(qi * tq, tq), 1, 0.0, m_prev)

    lp = lam_ref[...]
    lam = (jnp.exp(jnp.sum(lp[0:1] * lp[1:2], axis=1, keepdims=True))
           - jnp.exp(jnp.sum(lp[2:3] * lp[3:4], axis=1, keepdims=True)) + lam_init)
    for h in heads:
        o = acc_s[h, 0:LANES, :] / acc_s[h, LANES:LANES + 1, :]
        o = o[:, :tq] - lam * o[:, tq:]
        y = o * lax.rsqrt(jnp.mean(o * o, axis=0, keepdims=True) + NORM_EPS) * g_ref[...]
        o_ref[:, hcols(h)] = (y * (1.0 - lam_init)).T.astype(BF16)


def _diff_bias_tiles(slopes):
    tq, tk = DIFF_TQ, DIFF_TK
    rq = np.arange(tq)[None, :].astype(np.float64)
    c = np.arange(tk)[:, None].astype(np.float64)
    key = np.zeros((DIFF_HEADS, tk, LANES), np.float64)
    diag = np.zeros((DIFF_HEADS, tk, tq), np.float64)
    for h, sl in enumerate(slopes):
        key[h] = sl * LOG2E * c
        diag[h] = np.where(rq - c >= 0, sl * LOG2E * c, MASKED)
    return key.astype(np.float32), diag.astype(np.float32)


def _diff_attention(ob, lam_p, subln_g, lam_init, batch, seq, w_cast, li):
    m = ob.shape[0]
    tq, tk = DIFF_TQ, DIFF_TK
    assert tq == tk and seq % tq == 0
    nq = seq // tq
    slopes, _ = _alibi_slopes()
    bias_key, bias_diag = map(jnp.asarray, _diff_bias_tiles(slopes))
    kern = functools.partial(_diff_kernel, lam_init=lam_init,
                             slopes=tuple(float(s) for s in slopes))
    w_in_spec, w_out_spec, w_shape = _cast_specs(w_cast, li, batch * nq, lambda b, i: b * nq + i)
    return pl.pallas_call(
        kern,
        grid=(batch, nq),
        in_specs=[
            pl.BlockSpec((tq, GROUP_WIDTH), lambda b, i: (b * nq + i, 0)),
            pl.BlockSpec((seq, GROUP_WIDTH), lambda b, i: (b, 1)),
            pl.BlockSpec((seq, GROUP_WIDTH), lambda b, i: (b, 2)),
            pl.BlockSpec((DIFF_HEADS, tk, LANES), lambda b, i: (0, 0, 0)),
            pl.BlockSpec((DIFF_HEADS, tk, tq), lambda b, i: (0, 0, 0)),
            _layer_spec((4, HEAD_DIM), li),
            _layer_spec((2 * HEAD_DIM, 1), li),
            w_in_spec,
        ],
        out_specs=[pl.BlockSpec((tq, GROUP_WIDTH), lambda b, i: (b * nq + i, 0)), w_out_spec],
        out_shape=[jax.ShapeDtypeStruct((m, GROUP_WIDTH), BF16), w_shape],
        scratch_shapes=[
            pltpu.VMEM((DIFF_HEADS, LANES + DIFF_ONES_ROWS, seq), BF16),
            pltpu.VMEM((DIFF_HEADS, LANES + DIFF_ONES_ROWS, 2 * tq), F32),
        ],
        compiler_params=_params(("parallel", "arbitrary"), VMEM_LIMIT_BIG),
        name="diff_attn",
    )(ob, ob, ob, bias_key, bias_diag, lam_p, subln_g, w_cast)


def _dil_kernel(q_ref, k_ref, v_ref, bias_ref, wf_ref, o_ref, wb_ref, qf, kf, vf, m_s, l_s, o_s,
                *, seq):
    wb_ref[...] = wf_ref[...].astype(BF16)
    n = DIL_BLOCK
    qf[...] = q_ref[...].astype(F32)
    kf[...] = k_ref[...].astype(F32)
    vf[...] = v_ref[...].astype(F32)
    lane = lax.broadcasted_iota(jnp.int32, (n, LANES), 1)
    low = lane < HEAD_DIM

    for pi, (window, dil) in enumerate(DIL_PATTERNS):
        nblk = seq // (n * dil)
        for r in range(dil):
            for c in range(nblk):
                cur = pl.ds(r + dil * n * c, n, stride=dil)
                q2 = _stack_queries(qf[cur, :])
                if c > 0:
                    prev = pl.ds(r + dil * n * (c - 1), n, stride=dil)
                    kb = jnp.concatenate([kf[prev, :], kf[cur, :]], axis=0).astype(BF16)
                    vb = jnp.concatenate([vf[prev, :], vf[cur, :]], axis=0).astype(BF16)
                    bias = bias_ref[pi]
                else:
                    kb = kf[cur, :].astype(BF16)
                    vb = vf[cur, :].astype(BF16)
                    bias = bias_ref[pi, :, n:]
                s = _qk(q2, kb) + bias
                mx = jnp.max(s, axis=1, keepdims=True)
                p = jnp.exp2(s - mx).astype(BF16)
                vb1 = jnp.concatenate([vb, jnp.ones(vb.shape, BF16)], axis=1)
                ol = jnp.dot(p, vb1, preferred_element_type=F32)
                o, l = ol[:, :LANES], ol[:, LANES:]
                o_s[pi, cur, :] = jnp.where(low, o[:n], o[n:])
                m_s[pi, cur, :] = jnp.where(low, mx[:n], mx[n:])
                l_s[pi, cur, :] = jnp.where(low, l[:n], l[n:])

    m_all = jnp.maximum(jnp.maximum(m_s[0], m_s[1]), m_s[2])
    num = jnp.zeros((seq, LANES), F32)
    den = jnp.zeros((seq, LANES), F32)
    for pi in range(len(DIL_PATTERNS)):
        w = jnp.exp2(m_s[pi] - m_all)
        num = num + w * o_s[pi]
        den = den + w * l_s[pi]
    o_ref[...] = (num / den).astype(BF16)


def _dil_bias_tiles(slopes):
    n = DIL_BLOCK
    qi = (np.arange(2 * n) % n)[:, None]
    ki = np.arange(2 * n)[None, :]
    step = n + qi - ki
    valid = (step >= 0) & (step <= n)
    tiles = np.zeros((len(DIL_PATTERNS), DIL_HEADS // 2, 2 * n, 2 * n), np.float64)
    for pi, (_, dil) in enumerate(DIL_PATTERNS):
        for p in range(DIL_HEADS // 2):
            sl = np.where(np.arange(2 * n) < n, slopes[2 * p], slopes[2 * p + 1])[:, None]
            tiles[pi, p] = np.where(valid, -sl * LOG2E * (step * dil), MASKED)
    return tiles.astype(np.float32)


def _dil_attention(ob, batch, seq, w_cast, li):
    m = ob.shape[0]
    _, slopes = _alibi_slopes()
    bias = jnp.asarray(_dil_bias_tiles(slopes))
    npairs = DIL_HEADS // 2
    npat = len(DIL_PATTERNS)
    col0 = 3 * GROUP_WIDTH // LANES
    kern = functools.partial(_dil_kernel, seq=seq)
    w_in_spec, w_out_spec, w_shape = _cast_specs(w_cast, li, batch * npairs,
                                                 lambda b, p: b * npairs + p)
    return pl.pallas_call(
        kern,
        grid=(batch, npairs),
        in_specs=[
            pl.BlockSpec((seq, LANES), lambda b, p: (b, col0 + p)),
            pl.BlockSpec((seq, LANES), lambda b, p: (b, col0 + npairs + p)),
            pl.BlockSpec((seq, LANES), lambda b, p: (b, col0 + 2 * npairs + p)),
            pl.BlockSpec((npat, None, 2 * DIL_BLOCK, 2 * DIL_BLOCK), lambda b, p: (0, p, 0, 0)),
            w_in_spec,
        ],
        out_specs=[pl.BlockSpec((seq, LANES), lambda b, p: (b, p)), w_out_spec],
        out_shape=[jax.ShapeDtypeStruct((m, GROUP_WIDTH), BF16), w_shape],
        scratch_shapes=[
            pltpu.VMEM((seq, LANES), F32),
            pltpu.VMEM((seq, LANES), F32),
            pltpu.VMEM((seq, LANES), F32),
            pltpu.VMEM((npat, seq, LANES), F32),
            pltpu.VMEM((npat, seq, LANES), F32),
            pltpu.VMEM((npat, seq, LANES), F32),
        ],
        compiler_params=_params(("parallel", "parallel")),
        name="dil_attn",
    )(ob, ob, ob, bias, w_cast)


def _conv_kernel(a_ref, g_ref, ap_ref, gp_ref, w_ref, b_ref, n_ref, *rest, ncast):
    wf_refs, o_ref, wb_refs = rest[:ncast], rest[ncast], rest[ncast + 1:2 * ncast + 1]
    h_s, y_s = rest[2 * ncast + 1:]
    for wf_ref, wb_ref in zip(wf_refs, wb_refs):
        wb_ref[...] = wf_ref[...].astype(BF16)
    i = pl.program_id(1)
    tr, halo = CONV_ROWS, CONV_HALO
    hp = ap_ref[...] * jax.nn.sigmoid(gp_ref[...])
    h_s[0:halo, :] = jnp.where(i > 0, hp, 0.0)
    h_s[halo:, :] = a_ref[...] * jax.nn.sigmoid(g_ref[...])
    _conv_rows(h_s, w_ref, b_ref, n_ref, y_s, o_ref, 0, tr)


def _conv_rows(h_s, w_ref, b_ref, n_ref, y_s, o_ref, row0, nrows):
    rc, sub, last, halo = CONV_CHUNK, SUBLANES, CONV_WIDTH - 1, CONV_HALO
    for lt in range(GROUP_WIDTH // LANES):
        lanes = slice(lt * LANES, (lt + 1) * LANES)
        for c in range(nrows // rc):
            base = row0 + c * rc
            y = jnp.zeros((rc, LANES), F32)
            for r in range(sub):
                a_r = None
                for q in range((last - r) // sub + 1):
                    d = sub * q + r
                    start = base + halo - sub - sub * q
                    win = h_s[start:start + rc + sub, lanes]
                    term = w_ref[last - d, :, lanes] * win.reshape(rc // sub + 1, sub, LANES)
                    a_r = term if a_r is None else a_r + term
                y = y + a_r.reshape(rc + sub, LANES)[sub - r:sub - r + rc, :]
            y_s[c * rc:(c + 1) * rc, lanes] = y
    y = _rms(y_s[0:nrows, :] + b_ref[...], n_ref[...])
    o_ref[row0:row0 + nrows, :] = (y * jax.nn.sigmoid(y)).astype(BF16)


def _conv(of, w, b, g, batch, seq, casts, li):
    m = of.shape[0]
    tr, halo = CONV_ROWS, CONV_HALO
    nt = seq // tr
    per = tr // halo
    cur = lambda col: (lambda bb, i: (bb * nt + i, col))
    prev = lambda col: (lambda bb, i: (jnp.maximum((bb * nt + i) * per - 1, 0), col))
    cast_specs = [_cast_specs(wc, lc, batch * nt, lambda bb, i: bb * nt + i) for wc, lc in casts]
    return pl.pallas_call(
        functools.partial(_conv_kernel, ncast=len(casts)),
        grid=(batch, nt),
        in_specs=[
            pl.BlockSpec((tr, GROUP_WIDTH), cur(2)),
            pl.BlockSpec((tr, GROUP_WIDTH), cur(3)),
            pl.BlockSpec((halo, GROUP_WIDTH), prev(2)),
            pl.BlockSpec((halo, GROUP_WIDTH), prev(3)),
            _layer_spec((CONV_WIDTH, SUBLANES, GROUP_WIDTH), li),
            _layer_spec((1, GROUP_WIDTH), li),
            _layer_spec((1, GROUP_WIDTH), li),
            *[cs[0] for cs in cast_specs],
        ],
        out_specs=[pl.BlockSpec((tr, GROUP_WIDTH), lambda bb, i: (bb * nt + i, 0)),
                   *[cs[1] for cs in cast_specs]],
        out_shape=[jax.ShapeDtypeStruct((m, GROUP_WIDTH), BF16), *[cs[2] for cs in cast_specs]],
        scratch_shapes=[pltpu.VMEM((halo + tr, GROUP_WIDTH), F32),
                        pltpu.VMEM((tr, GROUP_WIDTH), F32)],
        compiler_params=_params(("parallel", "parallel")),
        name="conformer_conv",
    )(of, of, of, of, w, b, g, *[wc for wc, _ in casts])


def _out_kernel(a_ref, b_ref, c_ref, d_ref, w_ref, x_ref, g_ref, gn_ref, o_ref, h_ref):
    gw = GROUP_WIDTH
    for c in range(TM_OUT // OUT_ROW_CHUNK):
        rows = slice(c * OUT_ROW_CHUNK, (c + 1) * OUT_ROW_CHUNK)
        y = jnp.dot(a_ref[rows, :], w_ref[0:gw, :], preferred_element_type=F32)
        y = y + jnp.dot(b_ref[rows, :], w_ref[gw:2 * gw, :], preferred_element_type=F32)
        y = y + jnp.dot(c_ref[rows, :], w_ref[2 * gw:3 * gw, :], preferred_element_type=F32)
        y = y + jnp.dot(d_ref[rows, :], w_ref[3 * gw:4 * gw, :], preferred_element_type=F32)
        xn = x_ref[rows, :] + _rms(y, g_ref[...])
        o_ref[rows, :] = xn
        h_ref[rows, :] = _rms(xn, gn_ref[...]).astype(BF16)


def _out_proj(mix, w, x, g, g_next, li):
    m = x.shape[0]
    tm = TM_OUT
    mix_spec = pl.BlockSpec((tm, GROUP_WIDTH), lambda i: (i, 0))
    row_spec = pl.BlockSpec((tm, D_MODEL), lambda i: (i, 0))
    gain_spec = _layer_spec((1, D_MODEL), li)
    return pl.pallas_call(
        _out_kernel,
        grid=(m // tm,),
        in_specs=[mix_spec, mix_spec, mix_spec, mix_spec,
                  pl.BlockSpec((D_MODEL, D_MODEL), lambda i: (0, 0), pipeline_mode=pl.Buffered(1)),
                  row_spec, gain_spec, gain_spec],
        out_specs=[row_spec, row_spec],
        out_shape=[jax.ShapeDtypeStruct((m, D_MODEL), F32),
                   jax.ShapeDtypeStruct((m, D_MODEL), BF16)],
        compiler_params=_params(("parallel",)),
        name="out_proj",
    )(*mix, w, x, g, g_next)


def _ffn_kernel(x_hbm, h_ref, w1_ref, w2_ref, gpost_ref, o_ref, x_s, x_sem):
    i, j = pl.program_id(0), pl.program_id(1)
    x_copy = pltpu.make_async_copy(x_hbm.at[pl.ds(i * TM_FFN, TM_FFN), :], x_s, x_sem)

    def step(first):
        for cf in range(TF_FFN // FFN_MID_CHUNK):
            mid = slice(cf * FFN_MID_CHUNK, (cf + 1) * FFN_MID_CHUNK)
            f = jnp.dot(h_ref[...], w1_ref[:, mid], preferred_element_type=F32)
            f = jnp.square(jnp.maximum(f, 0.0)).astype(BF16)
            for c in range(D_MODEL // FFN_OUT_CHUNK):
                cols = slice(c * FFN_OUT_CHUNK, (c + 1) * FFN_OUT_CHUNK)
                y = jnp.dot(f, w2_ref[mid, cols], preferred_element_type=F32)
                if first and cf == 0:
                    o_ref[:, cols] = y
                else:
                    o_ref[:, cols] += y

    @pl.when(j == 0)
    def _():
        x_copy.start()
        step(True)

    @pl.when(j > 0)
    def _():
        step(False)

    @pl.when(j == pl.num_programs(1) - 1)
    def _():
        x_copy.wait()
        for r in range(TM_FFN // NORM_ROWS):
            rows = slice(r * NORM_ROWS, (r + 1) * NORM_ROWS)
            o_ref[rows, :] = x_s[rows, :] + _rms(o_ref[rows, :], gpost_ref[...])


def _ffn(x, h, w1, w2, gpost, li):
    m = x.shape[0]
    tm, tf = TM_FFN, TF_FFN
    return pl.pallas_call(
        _ffn_kernel,
        grid=(m // tm, D_FF // tf),
        in_specs=[
            pl.BlockSpec(memory_space=pl.ANY),
            pl.BlockSpec((tm, D_MODEL), lambda i, j: (i, 0)),
            pl.BlockSpec((D_MODEL, tf), lambda i, j: (0, j)),
            pl.BlockSpec((tf, D_MODEL), lambda i, j: (j, 0)),
            _layer_spec((1, D_MODEL), li),
        ],
        out_specs=pl.BlockSpec((tm, D_MODEL), lambda i, j: (i, 0)),
        out_shape=jax.ShapeDtypeStruct((m, D_MODEL), F32),
        scratch_shapes=[pltpu.VMEM((tm, D_MODEL), F32), pltpu.SemaphoreType.DMA(())],
        compiler_params=_params(("arbitrary", "arbitrary"), VMEM_LIMIT_BIG),
        name="ffn",
    )(x, h, w1, w2, gpost)


def kernel(x, g_mix_pre, g_mix_post, w_in, gmlp_w, gmlp_b, diff_lam, diff_subln, conv_w, conv_b,
           conv_norm, w_out, g_ffn_pre, g_ffn_post, w_ff1, w_ff2):
    batch, seq, _ = x.shape
    depth = w_in.shape[0]
    xf = x.reshape(batch * seq, D_MODEL)
    rows = lambda v: v.reshape(depth, 1, -1)
    g_mix_pre, g_mix_post, g_ffn_pre, g_ffn_post = map(
        rows, (g_mix_pre, g_mix_post, g_ffn_pre, g_ffn_post))
    conv_b, conv_norm = rows(conv_b), rows(conv_norm)
    subln = diff_subln.reshape(depth, -1, 1)
    gmlp_w_t = gmlp_w.transpose(0, 2, 1, 3).reshape(depth, GMLP_CHUNK, GMLP_GROUPS * GMLP_CHUNK)
    gmlp_bias = jnp.repeat(gmlp_b.transpose(0, 2, 1), GROUP_WIDTH // GMLP_GROUPS, axis=2)
    conv_w8 = jnp.broadcast_to(conv_w[:, :, None, :],
                               (depth, CONV_WIDTH, SUBLANES, GROUP_WIDTH))
    w_in_b = w_in[0].astype(BF16)
    for li in range(depth):
        of, ob = _in_proj(xf, g_mix_pre, w_in_b, li)
        out_a = _gmlp(of, gmlp_w_t, gmlp_bias, li)

        lam_init = 0.8 - 0.6 * float(np.exp(-0.3 * li))
        out_b, w_ff1_b = _diff_attention(ob, diff_lam, subln, lam_init, batch, seq, w_ff1, li)
        out_c, w_ff2_b = _dil_attention(ob, batch, seq, w_ff2, li)
        casts = [(w_out, li)] + ([(w_in, li + 1)] if li + 1 < depth else [])
        out_d, w_out_b, *w_next = _conv(of, conv_w8, conv_b, conv_norm, batch, seq, casts, li)
        if w_next:
            w_in_b = w_next[0]

        xf, h_ffn = _out_proj((out_a, out_b, out_c, out_d), w_out_b, xf, g_mix_post, g_ffn_pre,
                              li)
        xf = _ffn(xf, h_ffn, w_ff1_b, w_ff2_b, g_ffn_post, li)
    return xf.reshape(batch, seq, D_MODEL)
```

```python
import functools

import numpy as np
import jax
import jax.numpy as jnp
from jax import lax
from jax.experimental import pallas as pl
from jax.experimental.pallas import tpu as pltpu

F32 = jnp.float32
BF16 = jnp.bfloat16

D_MODEL = 2048
GROUP_WIDTH = 512
N_IN_SPLITS = 10
D_FF = 4 * D_MODEL
NORM_EPS = 1e-6
GMLP_CHUNK = 128
GMLP_GROUPS = 8
DIFF_HEADS = 4
DIL_HEADS = 8
HEAD_DIM = 64
DIL_PATTERNS = ((128, 1), (512, 4), (2048, 16))
DIL_BLOCK = 128
CONV_WIDTH = 31
N_ALIBI_HEADS = DIFF_HEADS + DIL_HEADS
ATTN_SCALE = HEAD_DIM ** -0.5
MASKED = -1e30
LOG2E = 1.4426950408889634
DIFF_ONES_ROWS = 16
DIFF_HEAD_GROUP = 2

LANES = 128
VMEM_LIMIT = 48 * 1024 * 1024
VMEM_LIMIT_BIG = 60 * 1024 * 1024

TM_PROJ = 1024
TM_OUT = 512
OUT_ROW_CHUNK = 256
TM_FFN = 1024
TF_FFN = 1024
FFN_MID_CHUNK = 512
FFN_OUT_CHUNK = 512
NORM_ROWS = 16
GMLP_ROWS = 1024
DIFF_TQ = 512
DIFF_TK = 512
CONV_ROWS = 512
CONV_HALO = 32
CONV_CHUNK = 128
SUBLANES = 8


def _alibi_slopes():
    i = np.arange(1, N_ALIBI_HEADS + 1, dtype=np.float64)
    s = 2.0 ** (-8.0 * i / N_ALIBI_HEADS)
    diff_idx = np.arange(0, N_ALIBI_HEADS, 3)
    dil_idx = np.array([j for j in range(N_ALIBI_HEADS) if j % 3 != 0])
    return s[diff_idx], s[dil_idx]


def _rms(x, g):
    return x * lax.rsqrt(jnp.mean(x * x, axis=-1, keepdims=True) + NORM_EPS) * g


def _params(sem, limit=VMEM_LIMIT, flags=None):
    return pltpu.CompilerParams(dimension_semantics=sem, vmem_limit_bytes=limit, flags=flags)


ATTN_FLAGS = None


def _layer_spec(shape, li):
    return pl.BlockSpec((None, *shape), lambda *ids: (li,) + (0,) * len(shape))


def _proj_kernel(x_hbm, g_ref, w_ref, of_ref, ob_ref, h_s, x_s, x_sem):
    i, j = pl.program_id(0), pl.program_id(1)
    nf, nb = 2 * GROUP_WIDTH, 3 * GROUP_WIDTH

    def x_copy(block):
        return pltpu.make_async_copy(x_hbm.at[pl.ds(block * TM_PROJ, TM_PROJ), :], x_s, x_sem)

    def project(col0, ncols, write):
        for c in range(ncols // GROUP_WIDTH):
            cols = slice(col0 + c * GROUP_WIDTH, col0 + (c + 1) * GROUP_WIDTH)
            write(slice(c * GROUP_WIDTH, (c + 1) * GROUP_WIDTH),
                  jnp.dot(h_s[...], w_ref[:, cols], preferred_element_type=F32))

    def write_f32(cols, y):
        of_ref[:, cols] = y

    def write_bf16(cols, y):
        ob_ref[:, cols] = y.astype(BF16)

    @pl.when(j == 0)
    def _():
        @pl.when(i == 0)
        def _():
            x_copy(0).start()
        x_copy(i).wait()
        h_s[...] = _rms(x_s[...], g_ref[...]).astype(BF16)

        @pl.when(i + 1 < pl.num_programs(0))
        def _():
            x_copy(i + 1).start()
        project(0, nf, write_f32)
        project(nf, nb, write_bf16)

    @pl.when(j == 1)
    def _():
        project(0, nb, write_bf16)
        project(nb, nf, write_f32)


def _in_proj(x, g, w, li):
    m = x.shape[0]
    tm = TM_PROJ
    half = N_IN_SPLITS * GROUP_WIDTH // 2
    return pl.pallas_call(
        _proj_kernel,
        grid=(m // tm, 2),
        in_specs=[
            pl.BlockSpec(memory_space=pl.ANY),
            _layer_spec((1, D_MODEL), li),
            pl.BlockSpec((D_MODEL, half), lambda i, j: (0, j)),
        ],
        out_specs=[
            pl.BlockSpec((tm, 2 * GROUP_WIDTH), lambda i, j: (i, j)),
            pl.BlockSpec((tm, 3 * GROUP_WIDTH), lambda i, j: (i, j)),
        ],
        out_shape=[
            jax.ShapeDtypeStruct((m, 4 * GROUP_WIDTH), F32),
            jax.ShapeDtypeStruct((m, 6 * GROUP_WIDTH), BF16),
        ],
        scratch_shapes=[pltpu.VMEM((tm, D_MODEL), BF16), pltpu.VMEM((tm, D_MODEL), F32),
                        pltpu.SemaphoreType.DMA(())],
        compiler_params=_params(("arbitrary", "arbitrary"), VMEM_LIMIT_BIG),
        name="in_proj",
    )(x, g, w)


def _gelu(x):
    return jax.nn.gelu(x, approximate=True)


def _gmlp_kernel(u_ref, v_ref, w_ref, b_ref, o_ref):
    c = GMLP_CHUNK
    t_idx = lax.broadcasted_iota(jnp.int32, w_ref.shape, 0)
    s_idx = lax.broadcasted_iota(jnp.int32, w_ref.shape, 1) % c
    w = jnp.where(s_idx <= t_idx, w_ref[...], 0.0).astype(BF16)
    bias = b_ref[...]
    lane = lax.broadcasted_iota(jnp.int32, (c, LANES), 1)
    low = lane < HEAD_DIM
    for ci in range(GMLP_ROWS // c):
        rows = slice(ci * c, (ci + 1) * c)
        u = _gelu(u_ref[rows, :])
        v = _gelu(v_ref[rows, :])
        mu = jnp.mean(v, axis=-1, keepdims=True)
        vc = v - mu
        v = vc * lax.rsqrt(jnp.mean(vc * vc, axis=-1, keepdims=True) + NORM_EPS)
        zs = []
        for p in range(GMLP_GROUPS // 2):
            vp = v[:, p * LANES:(p + 1) * LANES]
            rhs = jnp.concatenate([jnp.where(low, vp, 0.0), jnp.where(low, 0.0, vp)], axis=0)
            zs.append(jnp.dot(w[:, p * 2 * c:(p + 1) * 2 * c], rhs.astype(BF16),
                              preferred_element_type=F32))
        z = jnp.concatenate(zs, axis=1) + bias
        o_ref[rows, :] = (u * z).astype(BF16)


def _gmlp(of, w_t, bias_full, li):
    m = of.shape[0]
    tr = GMLP_ROWS
    return pl.pallas_call(
        _gmlp_kernel,
        grid=(m // tr,),
        in_specs=[
            pl.BlockSpec((tr, GROUP_WIDTH), lambda i: (i, 0)),
            pl.BlockSpec((tr, GROUP_WIDTH), lambda i: (i, 1)),
            _layer_spec((GMLP_CHUNK, GMLP_GROUPS * GMLP_CHUNK), li),
            _layer_spec((GMLP_CHUNK, GROUP_WIDTH), li),
        ],
        out_specs=pl.BlockSpec((tr, GROUP_WIDTH), lambda i: (i, 0)),
        out_shape=jax.ShapeDtypeStruct((m, GROUP_WIDTH), BF16),
        compiler_params=_params(("parallel",)),
        name="gmlp",
    )(of, of, w_t, bias_full)


def _stack_queries(q):
    lane = lax.broadcasted_iota(jnp.int32, q.shape, 1)
    low = lane < HEAD_DIM
    zero = jnp.zeros_like(q)
    q2 = jnp.concatenate([jnp.where(low, q, zero), jnp.where(low, zero, q)], axis=0)
    return (q2.astype(F32) * (ATTN_SCALE * LOG2E)).astype(BF16)


def _qk(q2, k):
    return lax.dot_general(q2, k, (((1,), (1,)), ((), ())), preferred_element_type=F32)


def _cast_specs(w, li, nsteps, step_of):
    _, rows, cols = w.shape
    slab = rows // nsteps
    assert slab * nsteps == rows and slab % 16 == 0
    in_spec = pl.BlockSpec((None, slab, cols), lambda *ids: (li, step_of(*ids), 0))
    out_spec = pl.BlockSpec((slab, cols), lambda *ids: (step_of(*ids), 0))
    return in_spec, out_spec, jax.ShapeDtypeStruct((rows, cols), BF16)


def _diff_kernel(q_ref, k_ref, v_ref, bkey_ref, bdiag_ref, lam_ref, g_ref, wf_ref, o_ref, wb_ref,
                 vt_s, acc_s, *, lam_init, slopes):
    wb_ref[...] = wf_ref[...].astype(BF16)
    qi = pl.program_id(1)
    tq, tk = DIFF_TQ, DIFF_TK
    seq = k_ref.shape[0]
    heads = range(DIFF_HEADS)
    hcols = lambda h: slice(h * LANES, (h + 1) * LANES)

    @pl.when(qi == 0)
    def _():
        for h in heads:
            for c in range(seq // tk):
                rows = slice(c * tk, (c + 1) * tk)
                vt_s[h, 0:LANES, rows] = v_ref[rows, hcols(h)].astype(F32).T.astype(BF16)
            vt_s[h, LANES:, :] = jnp.ones((DIFF_ONES_ROWS, seq), BF16)

    q2 = [_stack_queries(q_ref[:, hcols(h)]) for h in heads]
    acc_s[...] = jnp.zeros(acc_s.shape, F32)

    def step_group(hs, start, which, dist, m_prev):
        if which == 0:
            bias = {h: jnp.concatenate([bkey_ref[h]] * (2 * tq // LANES), axis=1) for h in hs}
        else:
            bias = {h: jnp.concatenate([bdiag_ref[h]] * 2, axis=1) for h in hs}
        sts = {h: _qk(k_ref[pl.ds(start, tk), hcols(h)], q2[h]) + bias[h]
               for h in hs}
        shifts = {h: (-slopes[h] * LOG2E) * dist for h in hs}
        m_next = {h: jnp.maximum(m_prev[h], jnp.max(sts[h], axis=0, keepdims=True) + shifts[h])
                  for h in hs}
        ps = {h: jnp.exp2(sts[h] - (m_next[h] - shifts[h])).astype(BF16) for h in hs}
        alphas = {h: jnp.exp2(m_prev[h] - m_next[h]) for h in hs}
        pvs = {h: jnp.dot(vt_s[h, :, pl.ds(start, tk)], ps[h], preferred_element_type=F32)
               for h in hs}
        for h in hs:
            acc_s[h] = alphas[h] * acc_s[h] + pvs[h]
        return m_next

    def diag_group(hs, start, m_prev):
        hk, hq = tk // 2, tq // 2
        late = (slice(hq, tq), slice(tq + hq, 2 * tq))
        bias = {h: jnp.concatenate([bdiag_ref[h, 0:hk, :]] * 2, axis=1) for h in hs}
        sts = {h: _qk(k_ref[pl.ds(start, hk), hcols(h)], q2[h]) + bias[h] for h in hs}
        m_mid = {h: jnp.maximum(m_prev[h], jnp.max(sts[h], axis=0, keepdims=True)) for h in hs}
        ps = {h: jnp.exp2(sts[h] - m_mid[h]).astype(BF16) for h in hs}
        alphas = {h: jnp.exp2(m_prev[h] - m_mid[h]) for h in hs}
        pvs = {h: jnp.dot(vt_s[h, :, pl.ds(start, hk)], ps[h], preferred_element_type=F32)
               for h in hs}
        for h in hs:
            acc_s[h] = alphas[h] * acc_s[h] + pvs[h]

        start_b = pl.multiple_of(start + hk, hk)
        q_late = {h: jnp.concatenate([q2[h][late[0], :], q2[h][late[1], :]], axis=0) for h in hs}
        bias = {h: jnp.concatenate([bdiag_ref[h, hk:tk, hq:tq]] * 2, axis=1) for h in hs}
        sts = {h: _qk(k_ref[pl.ds(start_b, hk), hcols(h)], q_late[h]) + bias[h] for h in hs}
        m_late = {h: jnp.concatenate([m_mid[h][:, late[0]], m_mid[h][:, late[1]]], axis=1)
                  for h in hs}
        m_end = {h: jnp.maximum(m_late[h], jnp.max(sts[h], axis=0, keepdims=True)) for h in hs}
        ps = {h: jnp.exp2(sts[h] - m_end[h]).astype(BF16) for h in hs}
        alphas = {h: jnp.exp2(m_late[h] - m_end[h]) for h in hs}
        pvs = {h: jnp.dot(vt_s[h, :, pl.ds(start_b, hk)], ps[h], preferred_element_type=F32)
               for h in hs}
        for h in hs:
            for r, lanes in enumerate(late):
                part = slice(r * hq, (r + 1) * hq)
                acc_s[h, :, lanes] = alphas[h][:, part] * acc_s[h, :, lanes] + pvs[h][:, part]

    def step(start, which, dist, m_prev):
        m_next = {}
        for g in range(0, DIFF_HEADS, DIFF_HEAD_GROUP):
            m_next.update(step_group(range(g, g + DIFF_HEAD_GROUP), start, which, dist, m_prev))
        return tuple(m_next[h] for h in heads)

    def body(j, m_prev):
        return step(pl.multiple_of(j * tk, tk), 0, ((qi - j) * tq).astype(F32), m_prev)

    m0 = jnp.full((1, 2 * tq), MASKED, F32)
    m_prev = lax.fori_loop(0, qi, body, tuple(m0 for _ in heads))
    for g in range(0, DIFF_HEADS, DIFF_HEAD_GROUP):
        diag_group(range(g, g + DIFF_HEAD_GROUP), pl.multiple_of(qi * tq, tq), m_prev)

    lp = lam_ref[...]
    lam = (jnp.exp(jnp.sum(lp[0:1] * lp[1:2], axis=1, keepdims=True))
           - jnp.exp(jnp.sum(lp[2:3] * lp[3:4], axis=1, keepdims=True)) + lam_init)
    for h in heads:
        o = acc_s[h, 0:LANES, :] / acc_s[h, LANES:LANES + 1, :]
        o = o[:, :tq] - lam * o[:, tq:]
        y = o * lax.rsqrt(jnp.mean(o * o, axis=0, keepdims=True) + NORM_EPS) * g_ref[...]
        o_ref[:, hcols(h)] = (y * (1.0 - lam_init)).T.astype(BF16)


def _diff_bias_tiles(slopes):
    tq, tk = DIFF_TQ, DIFF_TK
    rq = np.arange(tq)[None, :].astype(np.float64)
    c = np.arange(tk)[:, None].astype(np.float64)
    key = np.zeros((DIFF_HEADS, tk, LANES), np.float64)
    diag = np.zeros((DIFF_HEADS, tk, tq), np.float64)
    for h, sl in enumerate(slopes):
        key[h] = sl * LOG2E * c
        diag[h] = np.where(rq - c >= 0, sl * LOG2E * c, MASKED)
    return key.astype(np.float32), diag.astype(np.float32)


def _diff_attention(ob, lam_p, subln_g, lam_init, batch, seq, w_cast, li):
    m = ob.shape[0]
    tq, tk = DIFF_TQ, DIFF_TK
    assert tq == tk and seq % tq == 0
    nq = seq // tq
    slopes, _ = _alibi_slopes()
    bias_key, bias_diag = map(jnp.asarray, _diff_bias_tiles(slopes))
    kern = functools.partial(_diff_kernel, lam_init=lam_init,
                             slopes=tuple(float(s) for s in slopes))
    w_in_spec, w_out_spec, w_shape = _cast_specs(w_cast, li, batch * nq, lambda b, i: b * nq + i)
    return pl.pallas_call(
        kern,
        grid=(batch, nq),
        in_specs=[
            pl.BlockSpec((tq, GROUP_WIDTH), lambda b, i: (b * nq + i, 0)),
            pl.BlockSpec((seq, GROUP_WIDTH), lambda b, i: (b, 1)),
            pl.BlockSpec((seq, GROUP_WIDTH), lambda b, i: (b, 2)),
            pl.BlockSpec((DIFF_HEADS, tk, LANES), lambda b, i: (0, 0, 0)),
            pl.BlockSpec((DIFF_HEADS, tk, tq), lambda b, i: (0, 0, 0)),
            _layer_spec((4, HEAD_DIM), li),
            _layer_spec((2 * HEAD_DIM, 1), li),
            w_in_spec,
        ],
        out_specs=[pl.BlockSpec((tq, GROUP_WIDTH), lambda b, i: (b * nq + i, 0)), w_out_spec],
        out_shape=[jax.ShapeDtypeStruct((m, GROUP_WIDTH), BF16), w_shape],
        scratch_shapes=[
            pltpu.VMEM((DIFF_HEADS, LANES + DIFF_ONES_ROWS, seq), BF16),
            pltpu.VMEM((DIFF_HEADS, LANES + DIFF_ONES_ROWS, 2 * tq), F32),
        ],
        compiler_params=_params(("parallel", "arbitrary"), VMEM_LIMIT_BIG, ATTN_FLAGS),
        name="diff_attn",
    )(ob, ob, ob, bias_key, bias_diag, lam_p, subln_g, w_cast)


def _dil_kernel(q_ref, k_ref, v_ref, bias_ref, wf_ref, o_ref, wb_ref, qf, kf, vf, m_s, l_s, o_s,
                *, seq):
    wb_ref[...] = wf_ref[...].astype(BF16)
    n = DIL_BLOCK
    qf[...] = q_ref[...].astype(F32)
    kf[...] = k_ref[...].astype(F32)
    vf[...] = v_ref[...].astype(F32)
    lane = lax.broadcasted_iota(jnp.int32, (n, LANES), 1)
    low = lane < HEAD_DIM

    for pi, (window, dil) in enumerate(DIL_PATTERNS):
        nblk = seq // (n * dil)
        for r in range(dil):
            for c in range(nblk):
                cur = pl.ds(r + dil * n * c, n, stride=dil)
                q2 = _stack_queries(qf[cur, :])
                if c > 0:
                    prev = pl.ds(r + dil * n * (c - 1), n, stride=dil)
                    kb = jnp.concatenate([kf[prev, :], kf[cur, :]], axis=0).astype(BF16)
                    vb = jnp.concatenate([vf[prev, :], vf[cur, :]], axis=0).astype(BF16)
                    bias = bias_ref[pi]
                else:
                    kb = kf[cur, :].astype(BF16)
                    vb = vf[cur, :].astype(BF16)
                    bias = bias_ref[pi, :, n:]
                s = _qk(q2, kb) + bias
                mx = jnp.max(s, axis=1, keepdims=True)
                p = jnp.exp2(s - mx).astype(BF16)
                vb1 = jnp.concatenate([vb, jnp.ones(vb.shape, BF16)], axis=1)
                ol = jnp.dot(p, vb1, preferred_element_type=F32)
                o, l = ol[:, :LANES], ol[:, LANES:]
                o_s[pi, cur, :] = jnp.where(low, o[:n], o[n:])
                m_s[pi, cur, :] = jnp.where(low, mx[:n], mx[n:])
                l_s[pi, cur, :] = jnp.where(low, l[:n], l[n:])

    m_all = jnp.maximum(jnp.maximum(m_s[0], m_s[1]), m_s[2])
    num = jnp.zeros((seq, LANES), F32)
    den = jnp.zeros((seq, LANES), F32)
    for pi in range(len(DIL_PATTERNS)):
        w = jnp.exp2(m_s[pi] - m_all)
        num = num + w * o_s[pi]
        den = den + w * l_s[pi]
    o_ref[...] = (num / den).astype(BF16)


def _dil_bias_tiles(slopes):
    n = DIL_BLOCK
    qi = (np.arange(2 * n) % n)[:, None]
    ki = np.arange(2 * n)[None, :]
    step = n + qi - ki
    valid = (step >= 0) & (step <= n)
    tiles = np.zeros((len(DIL_PATTERNS), DIL_HEADS // 2, 2 * n, 2 * n), np.float64)
    for pi, (_, dil) in enumerate(DIL_PATTERNS):
        for p in range(DIL_HEADS // 2):
            sl = np.where(np.arange(2 * n) < n, slopes[2 * p], slopes[2 * p + 1])[:, None]
            tiles[pi, p] = np.where(valid, -sl * LOG2E * (step * dil), MASKED)
    return tiles.astype(np.float32)


def _dil_attention(ob, batch, seq, w_cast, li):
    m = ob.shape[0]
    _, slopes = _alibi_slopes()
    bias = jnp.asarray(_dil_bias_tiles(slopes))
    npairs = DIL_HEADS // 2
    npat = len(DIL_PATTERNS)
    col0 = 3 * GROUP_WIDTH // LANES
    kern = functools.partial(_dil_kernel, seq=seq)
    w_in_spec, w_out_spec, w_shape = _cast_specs(w_cast, li, batch * npairs,
                                                 lambda b, p: b * npairs + p)
    return pl.pallas_call(
        kern,
        grid=(batch, npairs),
        in_specs=[
            pl.BlockSpec((seq, LANES), lambda b, p: (b, col0 + p)),
            pl.BlockSpec((seq, LANES), lambda b, p: (b, col0 + npairs + p)),
            pl.BlockSpec((seq, LANES), lambda b, p: (b, col0 + 2 * npairs + p)),
            pl.BlockSpec((npat, None, 2 * DIL_BLOCK, 2 * DIL_BLOCK), lambda b, p: (0, p, 0, 0)),
            w_in_spec,
        ],
        out_specs=[pl.BlockSpec((seq, LANES), lambda b, p: (b, p)), w_out_spec],
        out_shape=[jax.ShapeDtypeStruct((m, GROUP_WIDTH), BF16), w_shape],
        scratch_shapes=[
            pltpu.VMEM((seq, LANES), F32),
            pltpu.VMEM((seq, LANES), F32),
            pltpu.VMEM((seq, LANES), F32),
            pltpu.VMEM((npat, seq, LANES), F32),
            pltpu.VMEM((npat, seq, LANES), F32),
            pltpu.VMEM((npat, seq, LANES), F32),
        ],
        compiler_params=_params(("parallel", "parallel"), VMEM_LIMIT, ATTN_FLAGS),
        name="dil_attn",
    )(ob, ob, ob, bias, w_cast)


def _conv_kernel(a_ref, g_ref, ap_ref, gp_ref, w_ref, b_ref, n_ref, *rest, ncast):
    wf_refs, o_ref, wb_refs = rest[:ncast], rest[ncast], rest[ncast + 1:2 * ncast + 1]
    h_s, y_s = rest[2 * ncast + 1:]
    for wf_ref, wb_ref in zip(wf_refs, wb_refs):
        wb_ref[...] = wf_ref[...].astype(BF16)
    i = pl.program_id(1)
    tr, halo = CONV_ROWS, CONV_HALO
    hp = ap_ref[...] * jax.nn.sigmoid(gp_ref[...])
    h_s[0:halo, :] = jnp.where(i > 0, hp, 0.0)
    h_s[halo:, :] = a_ref[...] * jax.nn.sigmoid(g_ref[...])
    _conv_rows(h_s, w_ref, b_ref, n_ref, y_s, o_ref, 0, tr)


def _conv_rows(h_s, w_ref, b_ref, n_ref, y_s, o_ref, row0, nrows):
    rc, sub, last, halo = CONV_CHUNK, SUBLANES, CONV_WIDTH - 1, CONV_HALO
    for lt in range(GROUP_WIDTH // LANES):
        lanes = slice(lt * LANES, (lt + 1) * LANES)
        for c in range(nrows // rc):
            base = row0 + c * rc
            y = jnp.zeros((rc, LANES), F32)
            for r in range(sub):
                a_r = None
                for q in range((last - r) // sub + 1):
                    d = sub * q + r
                    start = base + halo - sub - sub * q
                    win = h_s[start:start + rc + sub, lanes]
                    term = w_ref[last - d, :, lanes] * win.reshape(rc // sub + 1, sub, LANES)
                    a_r = term if a_r is None else a_r + term
                y = y + a_r.reshape(rc + sub, LANES)[sub - r:sub - r + rc, :]
            y_s[c * rc:(c + 1) * rc, lanes] = y
    y = _rms(y_s[0:nrows, :] + b_ref[...], n_ref[...])
    o_ref[row0:row0 + nrows, :] = (y * jax.nn.sigmoid(y)).astype(BF16)


def _conv(of, w, b, g, batch, seq, casts, li):
    m = of.shape[0]
    tr, halo = CONV_ROWS, CONV_HALO
    nt = seq // tr
    per = tr // halo
    cur = lambda col: (lambda bb, i: (bb * nt + i, col))
    prev = lambda col: (lambda bb, i: (jnp.maximum((bb * nt + i) * per - 1, 0), col))
    cast_specs = [_cast_specs(wc, lc, batch * nt, lambda bb, i: bb * nt + i) for wc, lc in casts]
    return pl.pallas_call(
        functools.partial(_conv_kernel, ncast=len(casts)),
        grid=(batch, nt),
        in_specs=[
            pl.BlockSpec((tr, GROUP_WIDTH), cur(2)),
            pl.BlockSpec((tr, GROUP_WIDTH), cur(3)),
            pl.BlockSpec((halo, GROUP_WIDTH), prev(2)),
            pl.BlockSpec((halo, GROUP_WIDTH), prev(3)),
            _layer_spec((CONV_WIDTH, SUBLANES, GROUP_WIDTH), li),
            _layer_spec((1, GROUP_WIDTH), li),
            _layer_spec((1, GROUP_WIDTH), li),
            *[cs[0] for cs in cast_specs],
        ],
        out_specs=[pl.BlockSpec((tr, GROUP_WIDTH), lambda bb, i: (bb * nt + i, 0)),
                   *[cs[1] for cs in cast_specs]],
        out_shape=[jax.ShapeDtypeStruct((m, GROUP_WIDTH), BF16), *[cs[2] for cs in cast_specs]],
        scratch_shapes=[pltpu.VMEM((halo + tr, GROUP_WIDTH), F32),
                        pltpu.VMEM((tr, GROUP_WIDTH), F32)],
        compiler_params=_params(("parallel", "parallel")),
        name="conformer_conv",
    )(of, of, of, of, w, b, g, *[wc for wc, _ in casts])


def _out_kernel(a_ref, b_ref, c_ref, d_ref, w_ref, x_ref, g_ref, gn_ref, o_ref, h_ref):
    gw = GROUP_WIDTH
    for c in range(TM_OUT // OUT_ROW_CHUNK):
        rows = slice(c * OUT_ROW_CHUNK, (c + 1) * OUT_ROW_CHUNK)
        y = jnp.dot(a_ref[rows, :], w_ref[0:gw, :], preferred_element_type=F32)
        y = y + jnp.dot(b_ref[rows, :], w_ref[gw:2 * gw, :], preferred_element_type=F32)
        y = y + jnp.dot(c_ref[rows, :], w_ref[2 * gw:3 * gw, :], preferred_element_type=F32)
        y = y + jnp.dot(d_ref[rows, :], w_ref[3 * gw:4 * gw, :], preferred_element_type=F32)
        xn = x_ref[rows, :] + _rms(y, g_ref[...])
        o_ref[rows, :] = xn
        h_ref[rows, :] = _rms(xn, gn_ref[...]).astype(BF16)


def _out_proj(mix, w, x, g, g_next, li):
    m = x.shape[0]
    tm = TM_OUT
    mix_spec = pl.BlockSpec((tm, GROUP_WIDTH), lambda i: (i, 0))
    row_spec = pl.BlockSpec((tm, D_MODEL), lambda i: (i, 0))
    gain_spec = _layer_spec((1, D_MODEL), li)
    return pl.pallas_call(
        _out_kernel,
        grid=(m // tm,),
        in_specs=[mix_spec, mix_spec, mix_spec, mix_spec,
                  pl.BlockSpec((D_MODEL, D_MODEL), lambda i: (0, 0), pipeline_mode=pl.Buffered(1)),
                  row_spec, gain_spec, gain_spec],
        out_specs=[row_spec, row_spec],
        out_shape=[jax.ShapeDtypeStruct((m, D_MODEL), F32),
                   jax.ShapeDtypeStruct((m, D_MODEL), BF16)],
        compiler_params=_params(("parallel",)),
        name="out_proj",
    )(*mix, w, x, g, g_next)


def _ffn_kernel(x_hbm, h_ref, w1_ref, w2_ref, gpost_ref, o_ref, x_s, x_sem):
    i, j = pl.program_id(0), pl.program_id(1)
    x_copy = pltpu.make_async_copy(x_hbm.at[pl.ds(i * TM_FFN, TM_FFN), :], x_s, x_sem)

    def step(first):
        for cf in range(TF_FFN // FFN_MID_CHUNK):
            mid = slice(cf * FFN_MID_CHUNK, (cf + 1) * FFN_MID_CHUNK)
            f = jnp.dot(h_ref[...], w1_ref[:, mid], preferred_element_type=F32)
            f = jnp.square(jnp.maximum(f, 0.0)).astype(BF16)
            for c in range(D_MODEL // FFN_OUT_CHUNK):
                cols = slice(c * FFN_OUT_CHUNK, (c + 1) * FFN_OUT_CHUNK)
                y = jnp.dot(f, w2_ref[mid, cols], preferred_element_type=F32)
                if first and cf == 0:
                    o_ref[:, cols] = y
                else:
                    o_ref[:, cols] += y

    @pl.when(j == 0)
    def _():
        x_copy.start()
        step(True)

    @pl.when(j > 0)
    def _():
        step(False)

    @pl.when(j == pl.num_programs(1) - 1)
    def _():
        x_copy.wait()
        for r in range(TM_FFN // NORM_ROWS):
            rows = slice(r * NORM_ROWS, (r + 1) * NORM_ROWS)
            o_ref[rows, :] = x_s[rows, :] + _rms(o_ref[rows, :], gpost_ref[...])


def _ffn(x, h, w1, w2, gpost, li):
    m = x.shape[0]
    tm, tf = TM_FFN, TF_FFN
    return pl.pallas_call(
        _ffn_kernel,
        grid=(m // tm, D_FF // tf),
        in_specs=[
            pl.BlockSpec(memory_space=pl.ANY),
            pl.BlockSpec((tm, D_MODEL), lambda i, j: (i, 0)),
            pl.BlockSpec((D_MODEL, tf), lambda i, j: (0, j)),
            pl.BlockSpec((tf, D_MODEL), lambda i, j: (j, 0)),
            _layer_spec((1, D_MODEL), li),
        ],
        out_specs=pl.BlockSpec((tm, D_MODEL), lambda i, j: (i, 0)),
        out_shape=jax.ShapeDtypeStruct((m, D_MODEL), F32),
        scratch_shapes=[pltpu.VMEM((tm, D_MODEL), F32), pltpu.SemaphoreType.DMA(())],
        compiler_params=_params(("arbitrary", "arbitrary"), VMEM_LIMIT_BIG),
        name="ffn",
    )(x, h, w1, w2, gpost)


def kernel(x, g_mix_pre, g_mix_post, w_in, gmlp_w, gmlp_b, diff_lam, diff_subln, conv_w, conv_b,
           conv_norm, w_out, g_ffn_pre, g_ffn_post, w_ff1, w_ff2):
    batch, seq, _ = x.shape
    depth = w_in.shape[0]
    xf = x.reshape(batch * seq, D_MODEL)
    rows = lambda v: v.reshape(depth, 1, -1)
    g_mix_pre, g_mix_post, g_ffn_pre, g_ffn_post = map(
        rows, (g_mix_pre, g_mix_post, g_ffn_pre, g_ffn_post))
    conv_b, conv_norm = rows(conv_b), rows(conv_norm)
    subln = diff_subln.reshape(depth, -1, 1)
    gmlp_w_t = gmlp_w.transpose(0, 2, 1, 3).reshape(depth, GMLP_CHUNK, GMLP_GROUPS * GMLP_CHUNK)
    gmlp_bias = jnp.repeat(gmlp_b.transpose(0, 2, 1), GROUP_WIDTH // GMLP_GROUPS, axis=2)
    conv_w8 = jnp.broadcast_to(conv_w[:, :, None, :],
                               (depth, CONV_WIDTH, SUBLANES, GROUP_WIDTH))
    w_in_b = w_in[0].astype(BF16)
    for li in range(depth):
        of, ob = _in_proj(xf, g_mix_pre, w_in_b, li)
        out_a = _gmlp(of, gmlp_w_t, gmlp_bias, li)

        lam_init = 0.8 - 0.6 * float(np.exp(-0.3 * li))
        out_b, w_ff1_b = _diff_attention(ob, diff_lam, subln, lam_init, batch, seq, w_ff1, li)
        out_c, w_ff2_b = _dil_attention(ob, batch, seq, w_ff2, li)
        casts = [(w_out, li)] + ([(w_in, li + 1)] if li + 1 < depth else [])
        out_d, w_out_b, *w_next = _conv(of, conv_w8, conv_b, conv_norm, batch, seq, casts, li)
        if w_next:
            w_in_b = w_next[0]

        xf, h_ffn = _out_proj((out_a, out_b, out_c, out_d), w_out_b, xf, g_mix_post, g_ffn_pre,
                              li)
        xf = _ffn(xf, h_ffn, w_ff1_b, w_ff2_b, g_ffn_post, li)
    return xf.reshape(batch, seq, D_MODEL)
```
